```python
import jax, jax.numpy as jnp
from jax import lax
import numpy as np

D_MODEL = 1024
BATCH = 16
SEQ = 2048
DEPTH = 2

N_META = 16
BLOCK = 128
N_PAD = BLOCK - N_META
NORM_EPS = 1e-6
NEG = -1e30

SB_HEADS = 8
SB_DIM = 64
SB_WIDTH = SB_HEADS * SB_DIM

MLA_HEADS = 8
MLA_Q_LORA = 256
MLA_KV_LORA = 128
MLA_NOPE = 64
MLA_ROPE = 32
MLA_V = 64
MLA_WIDTH = MLA_HEADS * MLA_V
ROPE_BASE = 10000.0

SWA_HEADS = 16
SWA_KV_HEADS = 2
SWA_DIM = 64
SWA_WINDOW = 128
SWA_WIDTH = SWA_HEADS * SWA_DIM

EVEN_SPLITS = [SB_WIDTH, SB_WIDTH, SB_WIDTH, SB_WIDTH,
               MLA_Q_LORA, MLA_KV_LORA, MLA_ROPE, MLA_WIDTH]
EVEN_IN = sum(EVEN_SPLITS)
EVEN_OUT = SB_WIDTH + MLA_WIDTH
ODD_SPLITS = [SWA_WIDTH, SWA_KV_HEADS * SWA_DIM, SWA_KV_HEADS * SWA_DIM, SWA_WIDTH]
ODD_IN = sum(ODD_SPLITS)
ODD_OUT = SWA_WIDTH

kernel_name = "hybrid_stickbreak_mla_swa_meta"


def _offsets(sizes):
    return [int(o) for o in np.cumsum(sizes)[:-1]]


def rmsnorm(x, g):
    xf = x.astype(jnp.float32)
    y = xf * lax.rsqrt(jnp.mean(xf * xf, axis=-1, keepdims=True) + NORM_EPS)
    return (y * g.astype(jnp.float32)).astype(x.dtype)


def apply_rope(x, pos):
    half = x.shape[-1] // 2
    inv = ROPE_BASE ** (-jnp.arange(half, dtype=jnp.float32) / half)
    ang = pos.astype(jnp.float32)[:, None] * inv[None, :]
    cos = jnp.cos(ang)[:, None, :]
    sin = jnp.sin(ang)[:, None, :]
    x1 = x[..., :half].astype(jnp.float32)
    x2 = x[..., half:].astype(jnp.float32)
    return jnp.concatenate([x1 * cos - x2 * sin, x1 * sin + x2 * cos], axis=-1).astype(x.dtype)


def alibi_slopes(n_heads):
    return 2.0 ** (-8.0 * (jnp.arange(n_heads, dtype=jnp.float32) + 1.0) / n_heads)


def stick_breaking_attention(q, k, v):
    Lp = q.shape[1]
    pos = jnp.arange(Lp)
    scale = SB_DIM ** -0.5
    outs = []
    for i in range(Lp // BLOCK):
        q0, q1 = i * BLOCK, (i + 1) * BLOCK
        z = jnp.einsum('bthd,bshd->bhts', q[:, q0:q1], k[:, :q1]).astype(jnp.float32) * scale
        t_pos = pos[q0:q1][:, None]
        s_pos = pos[:q1][None, :]
        mask = (s_pos < t_pos) & (s_pos >= N_PAD)
        log_beta = jax.nn.log_sigmoid(z)
        log_1m = jnp.where(mask, log_beta - z, 0.0)
        suffix = lax.cumsum(log_1m, axis=3, reverse=True) - log_1m
        a = jnp.where(mask, jnp.exp(log_beta + suffix), 0.0)
        outs.append(jnp.einsum('bhts,bshd->bthd', a.astype(v.dtype), v[:, :q1]))
    return jnp.concatenate(outs, axis=1)


def causal_block_softmax_attention(q, k, v, scale):
    Lp = q.shape[1]
    pos = jnp.arange(Lp)
    outs = []
    for i in range(Lp // BLOCK):
        q0, q1 = i * BLOCK, (i + 1) * BLOCK
        s = jnp.einsum('bthd,bshd->bhts', q[:, q0:q1], k[:, :q1]).astype(jnp.float32) * scale
        mask = (pos[None, :q1] <= pos[q0:q1, None]) & (pos[None, :q1] >= N_PAD)
        p = jax.nn.softmax(jnp.where(mask, s, NEG), axis=-1)
        outs.append(jnp.einsum('bhts,bshd->bthd', p.astype(v.dtype), v[:, :q1]))
    return jnp.concatenate(outs, axis=1)


def sliding_window_sink_attention(q, k, v, sinks):
    B, Lp = q.shape[0], q.shape[1]
    nb = Lp // BLOCK
    G = SWA_HEADS // SWA_KV_HEADS
    K = SWA_KV_HEADS
    qb = q.reshape(B, nb, BLOCK, K, G, SWA_DIM)
    kb = k.reshape(B, nb, BLOCK, K, SWA_DIM)
    vb = v.reshape(B, nb, BLOCK, K, SWA_DIM)
    shift = ((0, 0), (1, 0), (0, 0), (0, 0), (0, 0))
    k_band = jnp.concatenate([jnp.pad(kb[:, :-1], shift), kb], axis=2)
    v_band = jnp.concatenate([jnp.pad(vb[:, :-1], shift), vb], axis=2)
    k_meta = k[:, N_PAD:BLOCK]
    v_meta = v[:, N_PAD:BLOCK]
    blk = jnp.arange(nb)[:, None] * BLOCK
    t_pos = blk + jnp.arange(BLOCK)[None, :]
    s_pos = blk - BLOCK + jnp.arange(2 * BLOCK)[None, :]
    m_pos = N_PAD + jnp.arange(N_META)
    d_band = t_pos[:, :, None] - s_pos[:, None, :]
    d_meta = t_pos[:, :, None] - m_pos[None, None, :]
    band_ok = (d_band >= 0) & (d_band < SWA_WINDOW) & (s_pos[:, None, :] >= BLOCK)
    meta_ok = d_meta >= 0
    slopes = alibi_slopes(SWA_HEADS).reshape(K, G)[:, :, None, None]
    scale = SWA_DIM ** -0.5
    s_band = (jnp.einsum('bnqkgd,bnskd->bnkgqs', qb, k_band).astype(jnp.float32) * scale
              - slopes * d_band.astype(jnp.float32)[:, None, None])
    s_band = jnp.where(band_ok[:, None, None], s_band, NEG)
    s_meta = (jnp.einsum('bnqkgd,bmkd->bnkgqm', qb, k_meta).astype(jnp.float32) * scale
              - slopes * d_meta.astype(jnp.float32)[:, None, None])
    s_meta = jnp.where(meta_ok[:, None, None], s_meta, NEG)
    sink = jnp.broadcast_to(sinks.astype(jnp.float32).reshape(K, G, 1, 1),
                            s_band.shape[:-1] + (1,))
    p = jax.nn.softmax(jnp.concatenate([s_band, s_meta, sink], axis=-1), axis=-1)
    S = 2 * BLOCK
    p_band = p[..., :S].astype(v.dtype)
    p_meta = p[..., S:S + N_META].astype(v.dtype)
    o = (jnp.einsum('bnkgqs,bnskd->bnqkgd', p_band, v_band)
         + jnp.einsum('bnkgqm,bmkd->bnqkgd', p_meta, v_meta))
    return o.reshape(B, Lp, SWA_WIDTH)


def even_layer(h, pos, w_in, q_norm_g, kv_norm_g, w_uq, w_ukv, w_out):
    B, Lp = h.shape[0], h.shape[1]
    proj = h @ w_in
    q_sb, k_sb, v_sb, g_sb, c_q, c_kv, k_r, g_mla = jnp.split(proj, _offsets(EVEN_SPLITS), axis=-1)
    shp = (B, Lp, SB_HEADS, SB_DIM)
    o_sb = stick_breaking_attention(q_sb.reshape(shp), k_sb.reshape(shp), v_sb.reshape(shp))
    o_sb = o_sb.reshape(B, Lp, SB_WIDTH) * jax.nn.silu(g_sb)
    qh = (rmsnorm(c_q, q_norm_g) @ w_uq).reshape(B, Lp, MLA_HEADS, MLA_NOPE + MLA_ROPE)
    q_nope, q_rope = qh[..., :MLA_NOPE], qh[..., MLA_NOPE:]
    kvh = (rmsnorm(c_kv, kv_norm_g) @ w_ukv).reshape(B, Lp, MLA_HEADS, MLA_NOPE + MLA_V)
    k_nope, v_mla = kvh[..., :MLA_NOPE], kvh[..., MLA_NOPE:]
    k_rope = apply_rope(k_r[:, :, None, :], pos)
    q_full = jnp.concatenate([q_nope, apply_rope(q_rope, pos)], axis=-1)
    k_full = jnp.concatenate(
        [k_nope, jnp.broadcast_to(k_rope, (B, Lp, MLA_HEADS, MLA_ROPE))], axis=-1)
    o_mla = causal_block_softmax_attention(q_full, k_full, v_mla, (MLA_NOPE + MLA_ROPE) ** -0.5)
    o_mla = o_mla.reshape(B, Lp, MLA_WIDTH) * jax.nn.silu(g_mla)
    return jnp.concatenate([o_sb, o_mla], axis=-1) @ w_out


def odd_layer(h, w_in, sinks, w_out):
    B, Lp = h.shape[0], h.shape[1]
    proj = h @ w_in
    q, k, v, g = jnp.split(proj, _offsets(ODD_SPLITS), axis=-1)
    o = sliding_window_sink_attention(
        q.reshape(B, Lp, SWA_HEADS, SWA_DIM),
        k.reshape(B, Lp, SWA_KV_HEADS, SWA_DIM),
        v.reshape(B, Lp, SWA_KV_HEADS, SWA_DIM), sinks)
    return (o * jax.nn.silu(g)) @ w_out


def setup_inputs(seed: int = 0) -> dict:
    key = jax.random.key(seed)
    ks = jax.random.split(key, 13)
    ne = (DEPTH + 1) // 2
    no = DEPTH // 2
    f32 = jnp.float32

    def w(k, shape, fan_in):
        return jax.random.normal(k, shape, f32) * (fan_in ** -0.5)

    def gain(k, shape):
        return 1.0 + 0.05 * jax.random.normal(k, shape, f32)

    return {
        "x": jax.random.normal(ks[0], (BATCH, SEQ, D_MODEL), f32),
        "meta": jax.random.normal(ks[1], (N_META, D_MODEL), f32),
        "norm_g": gain(ks[2], (DEPTH, D_MODEL)),
        "final_g": gain(ks[3], (D_MODEL,)),
        "ev_w_in": w(ks[4], (ne, D_MODEL, EVEN_IN), D_MODEL),
        "ev_q_norm_g": gain(ks[5], (ne, MLA_Q_LORA)),
        "ev_kv_norm_g": gain(ks[6], (ne, MLA_KV_LORA)),
        "ev_w_uq": w(ks[7], (ne, MLA_Q_LORA, MLA_HEADS * (MLA_NOPE + MLA_ROPE)), MLA_Q_LORA),
        "ev_w_ukv": w(ks[8], (ne, MLA_KV_LORA, MLA_HEADS * (MLA_NOPE + MLA_V)), MLA_KV_LORA),
        "ev_w_out": w(ks[9], (ne, EVEN_OUT, D_MODEL), EVEN_OUT),
        "od_w_in": w(ks[10], (no, D_MODEL, ODD_IN), D_MODEL),
        "od_sinks": 0.5 * jax.random.normal(ks[11], (no, SWA_HEADS), f32),
        "od_w_out": w(ks[12], (no, ODD_OUT, D_MODEL), ODD_OUT),
    }


def reference(x, meta, norm_g, final_g, ev_w_in, ev_q_norm_g, ev_kv_norm_g, ev_w_uq,
              ev_w_ukv, ev_w_out, od_w_in, od_sinks, od_w_out):
    B = x.shape[0]
    meta_b = jnp.broadcast_to(meta.astype(x.dtype)[None], (B, N_META, D_MODEL))
    pad = jnp.zeros((B, N_PAD, D_MODEL), x.dtype)
    h = jnp.concatenate([pad, meta_b, x], axis=1)
    pos = jnp.arange(h.shape[1]) - N_PAD
    for layer in range(DEPTH):
        hn = rmsnorm(h, norm_g[layer])
        if layer % 2 == 0:
            i = layer // 2
            h = h + even_layer(hn, pos, ev_w_in[i], ev_q_norm_g[i], ev_kv_norm_g[i],
                               ev_w_uq[i], ev_w_ukv[i], ev_w_out[i])
        else:
            i = layer // 2
            h = h + odd_layer(hn, od_w_in[i], od_sinks[i], od_w_out[i])
    return rmsnorm(h, final_g)[:, BLOCK:]
```

```python
import functools

import jax
import jax.numpy as jnp
from jax import lax
from jax.experimental import pallas as pl
from jax.experimental.pallas import tpu as pltpu

D_MODEL = 1024
SEQ = 2048
N_META = 16
BLOCK = 128
N_PAD = BLOCK - N_META
LP = SEQ + BLOCK
NB = LP // BLOCK
NORM_EPS = 1e-6
NEG = -1e30

SB_HEADS = 8
SB_DIM = 64
SB_WIDTH = SB_HEADS * SB_DIM
MLA_HEADS = 8
MLA_Q_LORA = 256
MLA_KV_LORA = 128
MLA_NOPE = 64
MLA_ROPE = 32
MLA_V = 64
MLA_WIDTH = MLA_HEADS * MLA_V
ROPE_BASE = 10000.0
SWA_HEADS = 16
SWA_KV_HEADS = 2
SWA_DIM = 64
SWA_WIDTH = SWA_HEADS * SWA_DIM

LANES = 128
ROW_TILE = 544
VMEM_LIMIT = 48 * 1024 * 1024

BF16 = jnp.bfloat16
F32 = jnp.float32


def _dot(a, b):
    return jnp.dot(a, b, preferred_element_type=F32)


def _dot_nt(a, b):
    return lax.dot_general(a, b, (((1,), (1,)), ((), ())), preferred_element_type=F32)


def _rms(x, g):
    ms = jnp.mean(x * x, axis=-1, keepdims=True)
    return x * lax.rsqrt(ms + NORM_EPS) * g


def _silu(g):
    return g * (1.0 / (1.0 + jnp.exp(-g)))


def _params(sem):
    return pltpu.CompilerParams(dimension_semantics=sem, vmem_limit_bytes=VMEM_LIMIT)


def _proj0_kernel(x_ref, g_ref, w_ref, qg_ref, kvg_ref, wuq_ref, wukv_ref,
                  cq_ref, sq_ref, ck_ref, sk_ref,
                  qsb_ref, ksb_ref, vsb_ref, gate_ref, qm_ref, km_ref, vm_ref):
    xn = _rms(x_ref[0], g_ref[...]).astype(BF16)
    qsb_ref[0] = _dot(xn, w_ref[:, 0:512]).astype(BF16)
    ksb_ref[0] = _dot(xn, w_ref[:, 512:1024]).astype(BF16)
    vsb_ref[0] = _dot(xn, w_ref[:, 1024:1536]).astype(BF16)
    gate_ref[0] = _dot(xn, w_ref[:, 1536:2560]).astype(BF16)
    lat = _dot(xn, w_ref[:, 2560:3200])
    cqn = _rms(lat[:, 0:256], qg_ref[...]).astype(BF16)
    ckvn = _rms(lat[:, 256:384], kvg_ref[...]).astype(BF16)
    k_rope = lat[:, 384:512] * ck_ref[...] + lat[:, 512:640] * sk_ref[...]
    cq = cq_ref[...]
    sq = sq_ref[...]
    for h in range(MLA_HEADS):
        lo, hi = h * LANES, (h + 1) * LANES
        q1 = _dot(cqn, wuq_ref[:, lo:hi])
        q2 = _dot(cqn, wuq_ref[:, 1024 + lo:1024 + hi])
        qm_ref[0, :, lo:hi] = (q1 * cq + q2 * sq).astype(BF16)
        km_ref[0, :, lo:hi] = (_dot(ckvn, wukv_ref[:, lo:hi]) + k_rope).astype(BF16)
    vm_ref[0] = _dot(ckvn, wukv_ref[:, 1024:1536]).astype(BF16)


def _proj0(h0, g, w, qg, kvg, wuq, wukv, cq, sq, ck, sk):
    B = h0.shape[0]
    nt = LP // ROW_TILE
    row = lambda width: pl.BlockSpec((1, ROW_TILE, width), lambda b, t: (b, t, 0))
    full = lambda a: pl.BlockSpec(a.shape, lambda b, t: (0,) * a.ndim)
    tab = pl.BlockSpec((ROW_TILE, LANES), lambda b, t: (t, 0))
    out = lambda width: jax.ShapeDtypeStruct((B, LP, width), BF16)
    return pl.pallas_call(
        _proj0_kernel,
        grid=(B, nt),
        in_specs=[row(D_MODEL), full(g), full(w), full(qg), full(kvg), full(wuq), full(wukv),
                  tab, tab, tab, tab],
        out_specs=[row(512), row(512), row(512), row(1024), row(1024), row(1024), row(512)],
        out_shape=[out(512), out(512), out(512), out(1024), out(1024), out(1024), out(512)],
        compiler_params=_params(("parallel", "parallel")),
        name="proj0",
    )(h0, g, w, qg, kvg, wuq, wukv, cq, sq, ck, sk)


def _sb_kernel(q_ref, k_ref, v_ref, g_ref, u_ref, o_ref, acc_ref, c_ref):
    i = pl.program_id(2)
    lane = lax.broadcasted_iota(jnp.int32, (BLOCK, LANES), 1)
    row = lax.broadcasted_iota(jnp.int32, (BLOCK, BLOCK), 0)
    col = lax.broadcasted_iota(jnp.int32, (BLOCK, BLOCK), 1)
    q = q_ref[0]
    zero = jnp.zeros_like(q)
    q_heads = (jnp.where(lane < SB_DIM, q, zero), jnp.where(lane >= SB_DIM, q, zero))
    acc_ref[...] = jnp.zeros_like(acc_ref)
    c_ref[...] = jnp.zeros_like(c_ref)
    u = u_ref[...]

    def block(j, mask):
        start = pl.multiple_of(j * BLOCK, BLOCK)
        k = k_ref[0, pl.ds(start, BLOCK), :]
        v = v_ref[0, pl.ds(start, BLOCK), :]
        for hh in range(2):
            z = _dot_nt(q_heads[hh], k)
            sp = jnp.log(1.0 + jnp.exp(-jnp.abs(z)))
            log_1m = -(jnp.maximum(z, 0.0) + sp)
            log_beta = z + log_1m
            if mask is not None:
                log_1m = jnp.where(mask, log_1m, 0.0)
            su = _dot(log_1m.astype(BF16), u)
            c = c_ref[hh]
            a = jnp.exp(log_beta + su[:, :BLOCK] + c)
            if mask is not None:
                a = jnp.where(mask, a, 0.0)
            acc_ref[hh] += _dot(a.astype(BF16), v)
            c_ref[hh] = c + su[:, BLOCK:]

    block(i, (col < row) & (col + i * BLOCK >= N_PAD))

    def mid(idx, carry):
        block(i - 1 - idx, None)
        return carry

    lax.fori_loop(0, jnp.maximum(i - 1, 0), mid, 0)

    @pl.when(i >= 1)
    def _():
        block(0, col >= N_PAD)

    o = jnp.where(lane < SB_DIM, acc_ref[0], acc_ref[1])
    o_ref[0] = (o * _silu(g_ref[0].astype(F32))).astype(BF16)


def _sb_attn(q, k, v, gate, u):
    B = q.shape[0]
    qspec = pl.BlockSpec((1, BLOCK, LANES), lambda b, p, i: (b, i, p))
    kspec = pl.BlockSpec((1, LP, LANES), lambda b, p, i: (b, 0, p))
    return pl.pallas_call(
        _sb_kernel,
        grid=(B, SB_HEADS // 2, NB),
        in_specs=[qspec, kspec, kspec, qspec, pl.BlockSpec(u.shape, lambda b, p, i: (0, 0))],
        out_specs=qspec,
        out_shape=jax.ShapeDtypeStruct((B, LP, SB_WIDTH), BF16),
        scratch_shapes=[pltpu.VMEM((2, BLOCK, LANES), F32), pltpu.VMEM((2, BLOCK, LANES), F32)],
        compiler_params=_params(("parallel", "parallel", "arbitrary")),
        name="sb_attn",
    )(q, k, v, gate, u)


def _mla_kernel(q_ref, k_ref, v_ref, g_ref, o_ref, acc_ref, m_ref, l_ref):
    i = pl.program_id(2)
    lane = lax.broadcasted_iota(jnp.int32, (BLOCK, LANES), 1)
    row = lax.broadcasted_iota(jnp.int32, (BLOCK, BLOCK), 0)
    col = lax.broadcasted_iota(jnp.int32, (BLOCK, BLOCK), 1)
    acc_ref[...] = jnp.zeros_like(acc_ref)
    m_ref[...] = jnp.full_like(m_ref, NEG)
    l_ref[...] = jnp.zeros_like(l_ref)

    def block(j, mask):
        start = pl.multiple_of(j * BLOCK, BLOCK)
        v = v_ref[0, pl.ds(start, BLOCK), :]
        for hh in range(2):
            k = k_ref[0, pl.ds(start, BLOCK), hh * LANES:(hh + 1) * LANES]
            s = _dot_nt(q_ref[0, :, hh * LANES:(hh + 1) * LANES], k)
            if mask is not None:
                s = jnp.where(mask, s, NEG)
            m_old = m_ref[hh]
            m_new = jnp.maximum(m_old, jnp.max(s, axis=-1, keepdims=True))
            p = jnp.exp(s - m_new)
            alpha = jnp.exp(m_old - m_new)
            l_ref[hh] = alpha * l_ref[hh] + jnp.sum(p, axis=-1, keepdims=True)
            acc_ref[hh] = alpha * acc_ref[hh] + _dot(p.astype(BF16), v)
            m_ref[hh] = m_new

    block(i, (col <= row) & (col + i * BLOCK >= N_PAD))

    def mid(idx, carry):
        block(idx + 1, None)
        return carry

    lax.fori_loop(0, jnp.maximum(i - 1, 0), mid, 0)

    @pl.when(i >= 1)
    def _():
        block(0, col >= N_PAD)

    o = jnp.where(lane < MLA_V, acc_ref[0] / l_ref[0], acc_ref[1] / l_ref[1])
    o_ref[0] = (o * _silu(g_ref[0].astype(F32))).astype(BF16)


def _mla_attn(q, k, v, gate):
    B = q.shape[0]
    pairs = MLA_HEADS // 2
    return pl.pallas_call(
        _mla_kernel,
        grid=(B, pairs, NB),
        in_specs=[pl.BlockSpec((1, BLOCK, 2 * LANES), lambda b, p, i: (b, i, p)),
                  pl.BlockSpec((1, LP, 2 * LANES), lambda b, p, i: (b, 0, p)),
                  pl.BlockSpec((1, LP, LANES), lambda b, p, i: (b, 0, p)),
                  pl.BlockSpec((1, BLOCK, LANES), lambda b, p, i: (b, i, p + SB_HEADS // 2))],
        out_specs=pl.BlockSpec((1, BLOCK, LANES), lambda b, p, i: (b, i, p)),
        out_shape=jax.ShapeDtypeStruct((B, LP, MLA_WIDTH), BF16),
        scratch_shapes=[pltpu.VMEM((2, BLOCK, LANES), F32), pltpu.VMEM((2, BLOCK, LANES), F32),
                        pltpu.VMEM((2, BLOCK, LANES), F32)],
        compiler_params=_params(("parallel", "parallel", "arbitrary")),
        name="mla_attn",
    )(q, k, v, gate)


def _mid_kernel(osb_ref, omla_ref, h_ref, wo_ref, g_ref, w_ref, h1_ref, q_ref, kv_ref, gate_ref):
    y = _dot(osb_ref[0], wo_ref[0:512, :]) + _dot(omla_ref[0], wo_ref[512:1024, :])
    h1 = h_ref[0] + y
    h1_ref[0] = h1
    xn = _rms(h1, g_ref[...]).astype(BF16)
    q_ref[0] = _dot(xn, w_ref[:, 0:1024]).astype(BF16)
    kv_ref[0] = _dot(xn, w_ref[:, 1024:1536]).astype(BF16)
    gate_ref[0] = _dot(xn, w_ref[:, 1536:2560]).astype(BF16)


def _mid(osb, omla, h0, wo, g, w):
    B = h0.shape[0]
    nt = LP // ROW_TILE
    row = lambda width: pl.BlockSpec((1, ROW_TILE, width), lambda b, t: (b, t, 0))
    full = lambda a: pl.BlockSpec(a.shape, lambda b, t: (0,) * a.ndim)
    return pl.pallas_call(
        _mid_kernel,
        grid=(B, nt),
        in_specs=[row(512), row(512), row(D_MODEL), full(wo), full(g), full(w)],
        out_specs=[row(D_MODEL), row(1024), row(512), row(1024)],
        out_shape=[jax.ShapeDtypeStruct((B, LP, D_MODEL), F32),
                   jax.ShapeDtypeStruct((B, LP, 1024), BF16),
                   jax.ShapeDtypeStruct((B, LP, 512), BF16),
                   jax.ShapeDtypeStruct((B, LP, 1024), BF16)],
        compiler_params=_params(("parallel", "parallel")),
        name="mid",
    )(osb, omla, h0, wo, g, w)


def _swa_kernel(sink_ref, q_ref, kvp_ref, kvc_ref, kvm_ref, g_ref, o_ref):
    n = pl.program_id(1) + 1
    lane = lax.broadcasted_iota(jnp.int32, (BLOCK, LANES), 1)
    row = lax.broadcasted_iota(jnp.int32, (BLOCK, BLOCK), 0)
    col = lax.broadcasted_iota(jnp.int32, (BLOCK, BLOCK), 1)
    d_cur = (row - col).astype(F32)
    d_prev = d_cur + float(BLOCK)
    d_meta = (row - col + n * BLOCK).astype(F32)
    cur_ok = col <= row
    prev_ok = col > row + jnp.where(n > 1, 0, BLOCK)
    meta_ok = col >= N_PAD
    group = SWA_HEADS // SWA_KV_HEADS
    for pair in range(SWA_HEADS // 2):
        kh = (2 * pair) // group
        k_lo, v_lo = 2 * kh * SWA_DIM, (2 * SWA_KV_HEADS + 2 * kh) * SWA_DIM
        kp, vp = kvp_ref[0, :, k_lo:k_lo + LANES], kvp_ref[0, :, v_lo:v_lo + LANES]
        kc, vc = kvc_ref[0, :, k_lo:k_lo + LANES], kvc_ref[0, :, v_lo:v_lo + LANES]
        km, vm = kvm_ref[0, :, k_lo:k_lo + LANES], kvm_ref[0, :, v_lo:v_lo + LANES]
        q = q_ref[0, :, pair * LANES:(pair + 1) * LANES]
        zero = jnp.zeros_like(q)
        outs = []
        for hh in range(2):
            head = 2 * pair + hh
            slope = 2.0 ** (-8.0 * (head + 1.0) / SWA_HEADS)
            qh = jnp.where(lane < SWA_DIM, q, zero) if hh == 0 else jnp.where(lane >= SWA_DIM, q, zero)
            s_prev = jnp.where(prev_ok, _dot_nt(qh, kp) - slope * d_prev, NEG)
            s_cur = jnp.where(cur_ok, _dot_nt(qh, kc) - slope * d_cur, NEG)
            s_meta = jnp.where(meta_ok, _dot_nt(qh, km) - slope * d_meta, NEG)
            sink = sink_ref[head]
            m = jnp.maximum(jnp.max(jnp.maximum(jnp.maximum(s_prev, s_cur), s_meta),
                                    axis=-1, keepdims=True), sink)
            p_prev, p_cur, p_meta = jnp.exp(s_prev - m), jnp.exp(s_cur - m), jnp.exp(s_meta - m)
            denom = jnp.sum(p_prev + p_cur + p_meta, axis=-1, keepdims=True) + jnp.exp(sink - m)
            o = _dot(p_prev.astype(BF16), vp) + _dot(p_cur.astype(BF16), vc) + _dot(p_meta.astype(BF16), vm)
            outs.append(o / denom)
        o = jnp.where(lane < SWA_DIM, outs[0], outs[1])
        gate = g_ref[0, :, pair * LANES:(pair + 1) * LANES].astype(F32)
        o_ref[0, :, pair * LANES:(pair + 1) * LANES] = (o * _silu(gate)).astype(BF16)


def _swa_attn(sinks, q, kv, gate):
    B = q.shape[0]
    kvw = kv.shape[-1]
    qspec = pl.BlockSpec((1, BLOCK, SWA_WIDTH), lambda b, n, s: (b, n + 1, 0))
    grid_spec = pltpu.PrefetchScalarGridSpec(
        num_scalar_prefetch=1,
        grid=(B, NB - 1),
        in_specs=[qspec,
                  pl.BlockSpec((1, BLOCK, kvw), lambda b, n, s: (b, n, 0)),
                  pl.BlockSpec((1, BLOCK, kvw), lambda b, n, s: (b, n + 1, 0)),
                  pl.BlockSpec((1, BLOCK, kvw), lambda b, n, s: (b, 0, 0)),
                  qspec],
        out_specs=pl.BlockSpec((1, BLOCK, SWA_WIDTH), lambda b, n, s: (b, n, 0)),
    )
    return pl.pallas_call(
        _swa_kernel,
        grid_spec=grid_spec,
        out_shape=jax.ShapeDtypeStruct((B, SEQ, SWA_WIDTH), BF16),
        compiler_params=_params(("parallel", "parallel")),
        name="swa_attn",
    )(sinks, q, kv, kv, kv, gate)


def _final_kernel(o_ref, h_ref, wo_ref, g_ref, out_ref):
    out_ref[0] = _rms(h_ref[0] + _dot(o_ref[0], wo_ref[...]), g_ref[...])


def _final(o1, h1, wo, g):
    B = o1.shape[0]
    return pl.pallas_call(
        _final_kernel,
        grid=(B, SEQ // BLOCK),
        in_specs=[pl.BlockSpec((1, BLOCK, SWA_WIDTH), lambda b, t: (b, t, 0)),
                  pl.BlockSpec((1, BLOCK, D_MODEL), lambda b, t: (b, t + 1, 0)),
                  pl.BlockSpec(wo.shape, lambda b, t: (0, 0)),
                  pl.BlockSpec(g.shape, lambda b, t: (0, 0))],
        out_specs=pl.BlockSpec((1, BLOCK, D_MODEL), lambda b, t: (b, t, 0)),
        out_shape=jax.ShapeDtypeStruct((B, SEQ, D_MODEL), F32),
        compiler_params=_params(("parallel", "parallel")),
        name="final",
    )(o1, h1, wo, g)


def _layer0_weights(w_in, w_uq, w_ukv):
    q, k, v, g_sb = (w_in[:, i * 512:(i + 1) * 512] for i in range(4))
    c_q, c_kv = w_in[:, 2048:2304], w_in[:, 2304:2432]
    k_r, g_mla = w_in[:, 2432:2464], w_in[:, 2464:2976]
    half = MLA_ROPE // 2
    r1, r2 = k_r[:, :half], k_r[:, half:]
    z = lambda n: jnp.zeros((D_MODEL, n), w_in.dtype)
    kr_blk = jnp.concatenate([z(MLA_NOPE), r1, r2, z(LANES - MLA_NOPE - MLA_ROPE)], axis=1)
    kr_rot = jnp.concatenate([z(MLA_NOPE), -r2, r1, z(LANES - MLA_NOPE - MLA_ROPE)], axis=1)
    w0 = jnp.concatenate([q * (SB_DIM ** -0.5), k, v, g_sb, g_mla, c_q, c_kv, kr_blk, kr_rot], axis=1)

    uq = w_uq.reshape(MLA_Q_LORA, MLA_HEADS, MLA_NOPE + MLA_ROPE)
    nope, u1, u2 = uq[..., :MLA_NOPE], uq[..., MLA_NOPE:MLA_NOPE + half], uq[..., MLA_NOPE + half:]
    zq = lambda n: jnp.zeros((MLA_Q_LORA, MLA_HEADS, n), w_uq.dtype)
    uq_main = jnp.concatenate([nope, u1, u2, zq(LANES - MLA_NOPE - MLA_ROPE)], axis=-1)
    uq_rot = jnp.concatenate([zq(MLA_NOPE), -u2, u1, zq(LANES - MLA_NOPE - MLA_ROPE)], axis=-1)
    wuq = jnp.concatenate([uq_main.reshape(MLA_Q_LORA, -1), uq_rot.reshape(MLA_Q_LORA, -1)], axis=1)

    ukv = w_ukv.reshape(MLA_KV_LORA, MLA_HEADS, MLA_NOPE + MLA_V)
    k_nope = jnp.concatenate([ukv[..., :MLA_NOPE],
                              jnp.zeros((MLA_KV_LORA, MLA_HEADS, LANES - MLA_NOPE), w_ukv.dtype)], axis=-1)
    wukv = jnp.concatenate([k_nope.reshape(MLA_KV_LORA, -1),
                            ukv[..., MLA_NOPE:].reshape(MLA_KV_LORA, -1)], axis=1)
    return w0.astype(BF16), wuq.astype(BF16), wukv.astype(BF16)


def _layer1_weights(w_in):
    q, g = w_in[:, :1024], w_in[:, 1280:2304]
    k0, k1 = w_in[:, 1024:1088], w_in[:, 1088:1152]
    v0, v1 = w_in[:, 1152:1216], w_in[:, 1216:1280]
    w1 = jnp.concatenate([q * (SWA_DIM ** -0.5), k0, k0, k1, k1, v0, v0, v1, v1, g], axis=1)
    return w1.astype(BF16)


def _rope_tables():
    half = MLA_ROPE // 2
    pos = (jnp.arange(LP) - N_PAD).astype(F32)
    inv = ROPE_BASE ** (-jnp.arange(half, dtype=F32) / half)
    ang = pos[:, None] * inv[None, :]
    cos, sin = jnp.cos(ang), jnp.sin(ang)
    ones = jnp.ones((LP, MLA_NOPE), F32)
    z = lambda n: jnp.zeros((LP, n), F32)
    c = jnp.concatenate([ones, cos, cos, z(LANES - MLA_NOPE - MLA_ROPE)], axis=1)
    s = jnp.concatenate([z(MLA_NOPE), sin, sin, z(LANES - MLA_NOPE - MLA_ROPE)], axis=1)
    scale = (MLA_NOPE + MLA_ROPE) ** -0.5
    return c * scale, s * scale, c, s


def _suffix_matrix():
    r = lax.broadcasted_iota(jnp.int32, (BLOCK, 2 * BLOCK), 0)
    c = lax.broadcasted_iota(jnp.int32, (BLOCK, 2 * BLOCK), 1)
    return ((r > c) | (c >= BLOCK)).astype(BF16)


def kernel(x, meta, norm_g, final_g, ev_w_in, ev_q_norm_g, ev_kv_norm_g, ev_w_uq, ev_w_ukv,
           ev_w_out, od_w_in, od_sinks, od_w_out):
    B = x.shape[0]
    meta_b = jnp.broadcast_to(meta.astype(x.dtype)[None], (B, N_META, D_MODEL))
    h0 = jnp.concatenate([jnp.zeros((B, N_PAD, D_MODEL), x.dtype), meta_b, x], axis=1)

    w0, wuq, wukv = _layer0_weights(ev_w_in[0], ev_w_uq[0], ev_w_ukv[0])
    w1 = _layer1_weights(od_w_in[0])
    cq, sq, ck, sk = _rope_tables()

    q_sb, k_sb, v_sb, gate0, q_mla, k_mla, v_mla = _proj0(
        h0, norm_g[0:1], w0, ev_q_norm_g[0:1], ev_kv_norm_g[0:1], wuq, wukv, cq, sq, ck, sk)
    o_sb = _sb_attn(q_sb, k_sb, v_sb, gate0, _suffix_matrix())
    o_mla = _mla_attn(q_mla, k_mla, v_mla, gate0)
    h1, q1, kv1, gate1 = _mid(o_sb, o_mla, h0, ev_w_out[0].astype(BF16), norm_g[1:2], w1)
    o1 = _swa_attn(od_sinks[0], q1, kv1, gate1)
    return _final(o1, h1, od_w_out[0].astype(BF16), final_g[None, :])
```

```python
import math

import jax
import jax.numpy as jnp
from jax import lax
from jax.experimental import pallas as pl
from jax.experimental.pallas import tpu as pltpu

D_MODEL = 1024
SEQ = 2048
N_META = 16
BLOCK = 128
N_PAD = BLOCK - N_META
NORM_EPS = 1e-6
NEG = -1e30

SB_HEADS = 8
SB_DIM = 64
SB_WIDTH = SB_HEADS * SB_DIM
MLA_HEADS = 8
MLA_Q_LORA = 256
MLA_KV_LORA = 128
MLA_NOPE = 64
MLA_ROPE = 32
MLA_V = 64
MLA_WIDTH = MLA_HEADS * MLA_V
ROPE_BASE = 10000.0
SWA_HEADS = 16
SWA_KV_HEADS = 2
SWA_DIM = 64
SWA_WIDTH = SWA_HEADS * SWA_DIM

LANES = 128
ROW_TILE = 512
ATT_TILE = 256
VMEM_LIMIT = 48 * 1024 * 1024
LOG2E = math.log2(math.e)
SKEW = 1
SOFTPLUS_CLAMP = 64.0

BF16 = jnp.bfloat16
F32 = jnp.float32


def _dot(a, b):
    return jnp.dot(a, b, preferred_element_type=F32)


def _dot_nt(a, b):
    return lax.dot_general(a, b, (((1,), (1,)), ((), ())), preferred_element_type=F32)


def _rms(x, g):
    ms = jnp.mean(x * x, axis=-1, keepdims=True)
    return x * lax.rsqrt(ms + NORM_EPS) * g


def _silu(g):
    return g * (1.0 / (1.0 + jnp.exp(-g)))


def _params(sem):
    return pltpu.CompilerParams(dimension_semantics=sem, vmem_limit_bytes=VMEM_LIMIT)


def _iota(shape, dim):
    return lax.broadcasted_iota(jnp.int32, shape, dim)


def _split_pair(x):
    lane = _iota(x.shape, 1)
    zero = jnp.zeros_like(x)
    return jnp.where(lane < LANES // 2, x, zero), jnp.where(lane >= LANES // 2, x, zero)


def _merge_pair(lo, hi):
    lane = _iota(lo.shape, 1)
    return jnp.where(lane < LANES // 2, lo, hi)


def _proj0_kernel(x_ref, g_ref, w_ref, qg_ref, kvg_ref, wuq_ref, wukv_ref,
                  cq_ref, sq_ref, ck_ref, sk_ref,
                  qsb_ref, ksb_ref, vsb_ref, gate_ref, qm_ref, km_ref, vm_ref):
    xn = _rms(x_ref[0], g_ref[...]).astype(BF16)
    qsb_ref[0] = (_dot(xn, w_ref[:, 0:512]) * LOG2E).astype(BF16)
    ksb_ref[0] = _dot(xn, w_ref[:, 512:1024]).astype(BF16)
    vsb_ref[0] = _dot(xn, w_ref[:, 1024:1536]).astype(BF16)
    gate_ref[0] = _dot(xn, w_ref[:, 1536:2560]).astype(BF16)
    lat = _dot(xn, w_ref[:, 2560:3200])
    cqn = _rms(lat[:, 0:256], qg_ref[...]).astype(BF16)
    ckvn = _rms(lat[:, 256:384], kvg_ref[...]).astype(BF16)
    k_rope = lat[:, 384:512] * ck_ref[...] + lat[:, 512:640] * sk_ref[...]
    cq = cq_ref[...]
    sq = sq_ref[...]
    for h in range(MLA_HEADS):
        lo, hi = h * LANES, (h + 1) * LANES
        q1 = _dot(cqn, wuq_ref[:, lo:hi])
        q2 = _dot(cqn, wuq_ref[:, 1024 + lo:1024 + hi])
        qm_ref[0, :, lo:hi] = (q1 * cq + q2 * sq).astype(BF16)
        km_ref[0, :, lo:hi] = (_dot(ckvn, wukv_ref[:, lo:hi]) + k_rope).astype(BF16)
    vm_ref[0] = _dot(ckvn, wukv_ref[:, 1024:1536]).astype(BF16)


def _proj0(h, tile, g, w, qg, kvg, wuq, wukv, tables):
    B, L, _ = h.shape
    row = lambda width: pl.BlockSpec((1, tile, width), lambda b, t: (b, t, 0))
    full = lambda a: pl.BlockSpec(a.shape, lambda b, t: (0,) * a.ndim)
    tab = pl.BlockSpec((tile, LANES), lambda b, t: (t, 0))
    out = lambda width: jax.ShapeDtypeStruct((B, L, width), BF16)
    return pl.pallas_call(
        _proj0_kernel,
        grid=(B, L // tile),
        in_specs=[row(D_MODEL), full(g), full(w), full(qg), full(kvg), full(wuq), full(wukv),
                  tab, tab, tab, tab],
        out_specs=[row(512), row(512), row(512), row(1024), row(1024), row(1024), row(512)],
        out_shape=[out(512), out(512), out(512), out(1024), out(1024), out(1024), out(512)],
        compiler_params=_params(("parallel", "parallel")),
        name="proj0",
    )(h, g, w, qg, kvg, wuq, wukv, *tables)


def _sb_chunk(qh_ref, k, v, u, mask, acc_ref, c_ref):
    keys = k.shape[0]
    sl = lambda h: slice((h // 2) * LANES, (h // 2 + 1) * LANES)

    def scores(h):
        return _dot_nt(qh_ref[h], k[:, sl(h)])

    def suffix(z):
        sp = jnp.maximum(jnp.log2(1.0 + jnp.exp2(jnp.minimum(z, SOFTPLUS_CLAMP))), z)
        if mask is not None:
            sp = jnp.where(mask, sp, 0.0)
        return _dot(sp.astype(BF16), u)

    def weights(h, z, sfx):
        c = c_ref[h]
        a = jnp.exp2(z + sfx + jnp.concatenate([c] * (keys // LANES), axis=1))
        if mask is not None:
            a = jnp.where(mask, a, 0.0)
        acc_ref[h] += _dot(a.astype(BF16), v[:, sl(h)])
        c_ref[h] = c + jnp.broadcast_to(sfx[:, 0:1], c.shape)

    zs, sfxs = {}, {}
    for step in range(SB_HEADS + 2 * SKEW):
        if step < SB_HEADS:
            zs[step] = scores(step)
        if 0 <= step - SKEW < SB_HEADS:
            sfxs[step - SKEW] = suffix(zs[step - SKEW])
        if 0 <= step - 2 * SKEW < SB_HEADS:
            weights(step - 2 * SKEW, zs.pop(step - 2 * SKEW), sfxs.pop(step - 2 * SKEW))


def _sb_prologue(q_ref, qh_ref, acc_ref, c_ref):
    for p in range(SB_HEADS // 2):
        qh_ref[2 * p], qh_ref[2 * p + 1] = _split_pair(q_ref[0, :, p * LANES:(p + 1) * LANES])
    acc_ref[...] = jnp.zeros_like(acc_ref)
    c_ref[...] = jnp.zeros_like(c_ref)


def _sb_epilogue(acc_ref, g_ref, o_ref):
    for p in range(SB_HEADS // 2):
        sl = slice(p * LANES, (p + 1) * LANES)
        o = _merge_pair(acc_ref[2 * p], acc_ref[2 * p + 1])
        o_ref[0, :, sl] = (o * _silu(g_ref[0, :, sl].astype(F32))).astype(BF16)


def _sb_kernel(q_ref, k_ref, v_ref, km_ref, vm_ref, g_ref, u_ref, um_ref, o_ref, qh_ref, acc_ref, c_ref):
    i = pl.program_id(1)
    T = ATT_TILE
    _sb_prologue(q_ref, qh_ref, acc_ref, c_ref)

    def real_chunk(j, mask):
        start = pl.multiple_of(j * T, T)
        _sb_chunk(qh_ref, k_ref[0, pl.ds(start, T), :], v_ref[0, pl.ds(start, T), :],
                  u_ref[...], mask, acc_ref, c_ref)

    real_chunk(i, _iota((T, T), 1) < _iota((T, T), 0))

    def earlier(idx, carry):
        real_chunk(i - 1 - idx, None)
        return carry

    lax.fori_loop(0, i, earlier, 0)
    _sb_chunk(qh_ref, km_ref[0], vm_ref[0], um_ref[...], _iota((T, BLOCK), 1) >= N_PAD, acc_ref, c_ref)
    _sb_epilogue(acc_ref, g_ref, o_ref)


def _sb_attn(q, k, v, k_meta, v_meta, gate, u, u_meta):
    B, L, _ = q.shape
    T = ATT_TILE
    tile = pl.BlockSpec((1, T, SB_WIDTH), lambda b, i: (b, i, 0))
    seq = pl.BlockSpec((1, L, SB_WIDTH), lambda b, i: (b, 0, 0))
    meta = pl.BlockSpec((1, BLOCK, SB_WIDTH), lambda b, i: (0, 0, 0))
    const = lambda a: pl.BlockSpec(a.shape, lambda b, i: (0, 0))
    return pl.pallas_call(
        _sb_kernel,
        grid=(B, L // T),
        in_specs=[tile, seq, seq, meta, meta, tile, const(u), const(u_meta)],
        out_specs=tile,
        out_shape=jax.ShapeDtypeStruct((B, L, SB_WIDTH), BF16),
        scratch_shapes=[pltpu.VMEM((SB_HEADS, T, LANES), BF16), pltpu.VMEM((SB_HEADS, T, LANES), F32),
                        pltpu.VMEM((SB_HEADS, T, LANES), F32)],
        compiler_params=_params(("parallel", "arbitrary")),
        name="sb_attn",
    )(q, k, v, k_meta, v_meta, gate, u, u_meta)


def _sb_meta_kernel(q_ref, k_ref, v_ref, g_ref, u_ref, o_ref, qh_ref, acc_ref, c_ref):
    _sb_prologue(q_ref, qh_ref, acc_ref, c_ref)
    row, col = _iota((BLOCK, BLOCK), 0), _iota((BLOCK, BLOCK), 1)
    _sb_chunk(qh_ref, k_ref[0], v_ref[0], u_ref[...], (col < row) & (col >= N_PAD), acc_ref, c_ref)
    _sb_epilogue(acc_ref, g_ref, o_ref)


def _sb_meta(q, k, v, gate, u_meta):
    blk = pl.BlockSpec((1, BLOCK, SB_WIDTH), lambda i: (0, 0, 0))
    return pl.pallas_call(
        _sb_meta_kernel,
        grid=(1,),
        in_specs=[blk, blk, blk, blk, pl.BlockSpec(u_meta.shape, lambda i: (0, 0))],
        out_specs=blk,
        out_shape=jax.ShapeDtypeStruct((1, BLOCK, SB_WIDTH), BF16),
        scratch_shapes=[pltpu.VMEM((SB_HEADS, BLOCK, LANES), BF16), pltpu.VMEM((SB_HEADS, BLOCK, LANES), F32),
                        pltpu.VMEM((SB_HEADS, BLOCK, LANES), F32)],
        compiler_params=_params(("arbitrary",)),
        name="sb_meta",
    )(q, k, v, gate, u_meta)


def _mla_chunk(q_ref, k, v, mask, acc_ref, m_ref):
    keys = k.shape[0]
    ones = jnp.ones((keys, LANES), BF16)

    def scores(h):
        hs = slice(h * LANES, (h + 1) * LANES)
        s = _dot_nt(q_ref[0, :, hs], k[:, hs])
        return s if mask is None else jnp.where(mask, s, NEG)

    def probs(h, s):
        m_old = m_ref[h]
        m_new = jnp.maximum(m_old, jnp.max(s, axis=-1, keepdims=True))
        m_ref[h] = m_new
        p = jnp.exp2(s - jnp.concatenate([m_new] * (keys // LANES), axis=1))
        return p.astype(BF16), jnp.exp2(m_old - m_new)

    def update(h, p, alpha):
        vs = slice((h // 2) * LANES, (h // 2 + 1) * LANES)
        pv = _dot(p, jnp.concatenate([v[:, vs], ones], axis=1))
        acc_ref[h] = jnp.concatenate([alpha, alpha], axis=1) * acc_ref[h] + pv

    ss, ps = {}, {}
    for step in range(MLA_HEADS + 2 * SKEW):
        if step < MLA_HEADS:
            ss[step] = scores(step)
        if 0 <= step - SKEW < MLA_HEADS:
            ps[step - SKEW] = probs(step - SKEW, ss.pop(step - SKEW))
        if 0 <= step - 2 * SKEW < MLA_HEADS:
            update(step - 2 * SKEW, *ps.pop(step - 2 * SKEW))


def _mla_finish(acc_ref, g_ref, o_ref):
    for p in range(MLA_HEADS // 2):
        sl = slice(p * LANES, (p + 1) * LANES)
        lo = acc_ref[2 * p, :, :LANES] / acc_ref[2 * p, :, LANES:]
        hi = acc_ref[2 * p + 1, :, :LANES] / acc_ref[2 * p + 1, :, LANES:]
        gate = g_ref[0, :, MLA_WIDTH + p * LANES:MLA_WIDTH + (p + 1) * LANES].astype(F32)
        o_ref[0, :, sl] = (_merge_pair(lo, hi) * _silu(gate)).astype(BF16)


def _mla_kernel(q_ref, k_ref, v_ref, km_ref, vm_ref, g_ref, o_ref, acc_ref, m_ref):
    i = pl.program_id(1)
    T = ATT_TILE
    acc_ref[...] = jnp.zeros_like(acc_ref)
    m_ref[...] = jnp.full_like(m_ref, NEG)

    def real_chunk(j, mask):
        start = pl.multiple_of(j * T, T)
        _mla_chunk(q_ref, k_ref[0, pl.ds(start, T), :], v_ref[0, pl.ds(start, T), :], mask, acc_ref, m_ref)

    real_chunk(i, _iota((T, T), 1) <= _iota((T, T), 0))

    def earlier(idx, carry):
        real_chunk(idx, None)
        return carry

    lax.fori_loop(0, i, earlier, 0)
    _mla_chunk(q_ref, km_ref[0], vm_ref[0], _iota((T, BLOCK), 1) >= N_PAD, acc_ref, m_ref)
    _mla_finish(acc_ref, g_ref, o_ref)


def _mla_attn(q, k, v, k_meta, v_meta, gate):
    B, L, _ = q.shape
    T = ATT_TILE
    qk_w = MLA_HEADS * LANES
    return pl.pallas_call(
        _mla_kernel,
        grid=(B, L // T),
        in_specs=[pl.BlockSpec((1, T, qk_w), lambda b, i: (b, i, 0)),
                  pl.BlockSpec((1, L, qk_w), lambda b, i: (b, 0, 0)),
                  pl.BlockSpec((1, L, MLA_WIDTH), lambda b, i: (b, 0, 0)),
                  pl.BlockSpec((1, BLOCK, qk_w), lambda b, i: (0, 0, 0)),
                  pl.BlockSpec((1, BLOCK, MLA_WIDTH), lambda b, i: (0, 0, 0)),
                  pl.BlockSpec((1, T, 2 * MLA_WIDTH), lambda b, i: (b, i, 0))],
        out_specs=pl.BlockSpec((1, T, MLA_WIDTH), lambda b, i: (b, i, 0)),
        out_shape=jax.ShapeDtypeStruct((B, L, MLA_WIDTH), BF16),
        scratch_shapes=[pltpu.VMEM((MLA_HEADS, T, 2 * LANES), F32), pltpu.VMEM((MLA_HEADS, T, LANES), F32)],
        compiler_params=_params(("parallel", "arbitrary")),
        name="mla_attn",
    )(q, k, v, k_meta, v_meta, gate)


def _mla_meta_kernel(q_ref, k_ref, v_ref, g_ref, o_ref, acc_ref, m_ref):
    acc_ref[...] = jnp.zeros_like(acc_ref)
    m_ref[...] = jnp.full_like(m_ref, NEG)
    row, col = _iota((BLOCK, BLOCK), 0), _iota((BLOCK, BLOCK), 1)
    _mla_chunk(q_ref, k_ref[0], v_ref[0], (col <= row) & (col >= N_PAD), acc_ref, m_ref)
    _mla_finish(acc_ref, g_ref, o_ref)


def _mla_meta(q, k, v, gate):
    qk_w = MLA_HEADS * LANES
    blk = lambda w: pl.BlockSpec((1, BLOCK, w), lambda i: (0, 0, 0))
    return pl.pallas_call(
        _mla_meta_kernel,
        grid=(1,),
        in_specs=[blk(qk_w), blk(qk_w), blk(MLA_WIDTH), blk(2 * MLA_WIDTH)],
        out_specs=blk(MLA_WIDTH),
        out_shape=jax.ShapeDtypeStruct((1, BLOCK, MLA_WIDTH), BF16),
        scratch_shapes=[pltpu.VMEM((MLA_HEADS, BLOCK, 2 * LANES), F32), pltpu.VMEM((MLA_HEADS, BLOCK, LANES), F32)],
        compiler_params=_params(("arbitrary",)),
        name="mla_meta",
    )(q, k, v, gate)


def _mid_kernel(osb_ref, omla_ref, h_ref, wo_ref, g_ref, w_ref, h1_ref, q_ref, kv_ref, gate_ref):
    y = _dot(osb_ref[0], wo_ref[0:512, :]) + _dot(omla_ref[0], wo_ref[512:1024, :])
    h1 = h_ref[0] + y
    h1_ref[0] = h1
    xn = _rms(h1, g_ref[...]).astype(BF16)
    q_ref[0] = (_dot(xn, w_ref[:, 0:1024]) * LOG2E).astype(BF16)
    kv_ref[0] = _dot(xn, w_ref[:, 1024:1536]).astype(BF16)
    gate_ref[0] = _dot(xn, w_ref[:, 1536:2560]).astype(BF16)


def _mid(osb, omla, h, tile, wo, g, w):
    B, L, _ = h.shape
    row = lambda width: pl.BlockSpec((1, tile, width), lambda b, t: (b, t, 0))
    full = lambda a: pl.BlockSpec(a.shape, lambda b, t: (0,) * a.ndim)
    return pl.pallas_call(
        _mid_kernel,
        grid=(B, L // tile),
        in_specs=[row(512), row(512), row(D_MODEL), full(wo), full(g), full(w)],
        out_specs=[row(D_MODEL), row(1024), row(512), row(1024)],
        out_shape=[jax.ShapeDtypeStruct((B, L, D_MODEL), F32),
                   jax.ShapeDtypeStruct((B, L, 1024), BF16),
                   jax.ShapeDtypeStruct((B, L, 512), BF16),
                   jax.ShapeDtypeStruct((B, L, 1024), BF16)],
        compiler_params=_params(("parallel", "parallel")),
        name="mid",
    )(osb, omla, h, wo, g, w)


def _swa_kernel(sink_ref, q_ref, kvp_ref, kvc_ref, kvm_ref, g_ref, bb_ref, mb_ref, o_ref):
    n = pl.program_id(1)
    in_cur = _iota((BLOCK, BLOCK), 1) <= _iota((BLOCK, BLOCK), 0)
    no_prev = jnp.where(n > 0, 0.0, -NEG)
    blocks_before = (n + 1).astype(F32) * float(BLOCK)
    group = SWA_HEADS // SWA_KV_HEADS
    k_sl = lambda h: slice(2 * (h // group) * SWA_DIM, 2 * (h // group) * SWA_DIM + LANES)
    v_sl = lambda h: slice((2 * SWA_KV_HEADS + 2 * (h // group)) * SWA_DIM,
                           (2 * SWA_KV_HEADS + 2 * (h // group)) * SWA_DIM + LANES)

    def scores(h):
        qh = _split_pair(q_ref[0, :, (h // 2) * LANES:(h // 2 + 1) * LANES])[h % 2]
        k_band = jnp.concatenate([kvp_ref[0, :, k_sl(h)], kvc_ref[0, :, k_sl(h)]], axis=0)
        return _dot_nt(qh, k_band), _dot_nt(qh, kvm_ref[0, :, k_sl(h)])

    def probs(h, z, z_meta):
        slope = 2.0 ** (-8.0 * (h + 1.0) / SWA_HEADS) * LOG2E
        s_band = jnp.where(in_cur, z[:, BLOCK:], z[:, :BLOCK] - no_prev) - bb_ref[h]
        s_meta = z_meta - mb_ref[h] - slope * blocks_before
        sink = sink_ref[h] * LOG2E
        m = jnp.maximum(jnp.max(jnp.maximum(s_band, s_meta), axis=-1, keepdims=True), sink)
        p_band, p_meta = jnp.exp2(s_band - m), jnp.exp2(s_meta - m)
        denom = jnp.sum(p_band + p_meta, axis=-1, keepdims=True) + jnp.exp2(sink - m)
        p_split = jnp.concatenate([jnp.where(in_cur, 0.0, p_band), jnp.where(in_cur, p_band, 0.0)], axis=1)
        return p_split.astype(BF16), p_meta.astype(BF16), 1.0 / denom

    def values(h, p_split, p_meta, inv):
        v_band = jnp.concatenate([kvp_ref[0, :, v_sl(h)], kvc_ref[0, :, v_sl(h)]], axis=0)
        return (_dot(p_split, v_band) + _dot(p_meta, kvm_ref[0, :, v_sl(h)])) * inv

    zs, ps, outs = {}, {}, {}
    for step in range(SWA_HEADS + 2 * SKEW):
        if step < SWA_HEADS:
            zs[step] = scores(step)
        if 0 <= step - SKEW < SWA_HEADS:
            ps[step - SKEW] = probs(step - SKEW, *zs.pop(step - SKEW))
        h = step - 2 * SKEW
        if 0 <= h < SWA_HEADS:
            outs[h] = values(h, *ps.pop(h))
            if h % 2 == 1:
                sl = slice((h // 2) * LANES, (h // 2 + 1) * LANES)
                o = _merge_pair(outs.pop(h - 1), outs.pop(h))
                o_ref[0, :, sl] = (o * _silu(g_ref[0, :, sl].astype(F32))).astype(BF16)


def _swa_bias_tables():
    row, col = _iota((BLOCK, BLOCK), 0), _iota((BLOCK, BLOCK), 1)
    slopes = (2.0 ** (-8.0 * (jnp.arange(SWA_HEADS, dtype=F32) + 1.0) / SWA_HEADS) * LOG2E)[:, None, None]
    dist = jnp.where(col <= row, row - col, row - col + BLOCK).astype(F32)
    meta = jnp.where(col >= N_PAD, 0.0, -NEG) + slopes * (row - col).astype(F32)
    return slopes * dist, meta


def _swa_attn(sinks, q, kv, kv_meta, gate):
    B, L, _ = q.shape
    kvw = kv.shape[-1]
    band_bias, meta_bias = _swa_bias_tables()
    qspec = pl.BlockSpec((1, BLOCK, SWA_WIDTH), lambda b, n, s: (b, n, 0))
    table = pl.BlockSpec((SWA_HEADS, BLOCK, BLOCK), lambda b, n, s: (0, 0, 0))
    grid_spec = pltpu.PrefetchScalarGridSpec(
        num_scalar_prefetch=1,
        grid=(B, L // BLOCK),
        in_specs=[qspec,
                  pl.BlockSpec((1, BLOCK, kvw), lambda b, n, s: (b, jnp.maximum(n - 1, 0), 0)),
                  pl.BlockSpec((1, BLOCK, kvw), lambda b, n, s: (b, n, 0)),
                  pl.BlockSpec((1, BLOCK, kvw), lambda b, n, s: (0, 0, 0)),
                  qspec, table, table],
        out_specs=qspec,
    )
    return pl.pallas_call(
        _swa_kernel,
        grid_spec=grid_spec,
        out_shape=jax.ShapeDtypeStruct((B, L, SWA_WIDTH), BF16),
        compiler_params=_params(("parallel", "parallel")),
        name="swa_attn",
    )(sinks, q, kv, kv, kv_meta, gate, band_bias, meta_bias)


def _final_kernel(o_ref, h_ref, wo_ref, g_ref, out_ref):
    out_ref[0] = _rms(h_ref[0] + _dot(o_ref[0], wo_ref[...]), g_ref[...])


def _final(o1, h1, wo, g):
    B, L, _ = o1.shape
    row = pl.BlockSpec((1, ROW_TILE, D_MODEL), lambda b, t: (b, t, 0))
    const = lambda a: pl.BlockSpec(a.shape, lambda b, t: (0, 0))
    return pl.pallas_call(
        _final_kernel,
        grid=(B, L // ROW_TILE),
        in_specs=[row, row, const(wo), const(g)],
        out_specs=row,
        out_shape=jax.ShapeDtypeStruct((B, L, D_MODEL), F32),
        compiler_params=_params(("parallel", "parallel")),
        name="final",
    )(o1, h1, wo, g)


def _layer0_weights(w_in, w_uq, w_ukv):
    q, k, v, g_sb = (w_in[:, i * 512:(i + 1) * 512] for i in range(4))
    c_q, c_kv = w_in[:, 2048:2304], w_in[:, 2304:2432]
    k_r, g_mla = w_in[:, 2432:2464], w_in[:, 2464:2976]
    half = MLA_ROPE // 2
    r1, r2 = k_r[:, :half], k_r[:, half:]
    z = lambda n: jnp.zeros((D_MODEL, n), w_in.dtype)
    kr_blk = jnp.concatenate([z(MLA_NOPE), r1, r2, z(LANES - MLA_NOPE - MLA_ROPE)], axis=1)
    kr_rot = jnp.concatenate([z(MLA_NOPE), -r2, r1, z(LANES - MLA_NOPE - MLA_ROPE)], axis=1)
    w0 = jnp.concatenate([q * (SB_DIM ** -0.5), k, v, g_sb, g_mla, c_q, c_kv, kr_blk, kr_rot], axis=1)

    uq = w_uq.reshape(MLA_Q_LORA, MLA_HEADS, MLA_NOPE + MLA_ROPE)
    nope, u1, u2 = uq[..., :MLA_NOPE], uq[..., MLA_NOPE:MLA_NOPE + half], uq[..., MLA_NOPE + half:]
    zq = lambda n: jnp.zeros((MLA_Q_LORA, MLA_HEADS, n), w_uq.dtype)
    uq_main = jnp.concatenate([nope, u1, u2, zq(LANES - MLA_NOPE - MLA_ROPE)], axis=-1)
    uq_rot = jnp.concatenate([zq(MLA_NOPE), -u2, u1, zq(LANES - MLA_NOPE - MLA_ROPE)], axis=-1)
    wuq = jnp.concatenate([uq_main.reshape(MLA_Q_LORA, -1), uq_rot.reshape(MLA_Q_LORA, -1)], axis=1)

    ukv = w_ukv.reshape(MLA_KV_LORA, MLA_HEADS, MLA_NOPE + MLA_V)
    k_nope = jnp.concatenate([ukv[..., :MLA_NOPE],
                              jnp.zeros((MLA_KV_LORA, MLA_HEADS, LANES - MLA_NOPE), w_ukv.dtype)], axis=-1)
    wukv = jnp.concatenate([k_nope.reshape(MLA_KV_LORA, -1),
                            ukv[..., MLA_NOPE:].reshape(MLA_KV_LORA, -1)], axis=1)
    return w0.astype(BF16), wuq.astype(BF16), wukv.astype(BF16)


def _layer1_weights(w_in):
    q, g = w_in[:, :1024], w_in[:, 1280:2304]
    k0, k1 = w_in[:, 1024:1088], w_in[:, 1088:1152]
    v0, v1 = w_in[:, 1152:1216], w_in[:, 1216:1280]
    w1 = jnp.concatenate([q * (SWA_DIM ** -0.5), k0, k0, k1, k1, v0, v0, v1, v1, g], axis=1)
    return w1.astype(BF16)


def _rope_tables():
    half = MLA_ROPE // 2
    pos = jnp.arange(N_META + SEQ).astype(F32)
    inv = ROPE_BASE ** (-jnp.arange(half, dtype=F32) / half)
    ang = pos[:, None] * inv[None, :]
    cos, sin = jnp.cos(ang), jnp.sin(ang)
    n = pos.shape[0]
    z = lambda w: jnp.zeros((n, w), F32)
    c = jnp.concatenate([jnp.ones((n, MLA_NOPE), F32), cos, cos, z(LANES - MLA_NOPE - MLA_ROPE)], axis=1)
    s = jnp.concatenate([z(MLA_NOPE), sin, sin, z(LANES - MLA_NOPE - MLA_ROPE)], axis=1)
    scale = (MLA_NOPE + MLA_ROPE) ** -0.5 * LOG2E
    tabs = (c * scale, s * scale, c, s)
    pad = lambda t: jnp.concatenate([jnp.zeros((N_PAD, LANES), F32), t[:N_META]], axis=0)
    return tuple(pad(t) for t in tabs), tuple(t[N_META:] for t in tabs)


def _suffix_matrix(n):
    return jnp.where(_iota((n, n), 0) >= _iota((n, n), 1), -1.0, 0.0).astype(BF16)


def kernel(x, meta, norm_g, final_g, ev_w_in, ev_q_norm_g, ev_kv_norm_g, ev_w_uq, ev_w_ukv,
           ev_w_out, od_w_in, od_sinks, od_w_out):
    w0, wuq, wukv = _layer0_weights(ev_w_in[0], ev_w_uq[0], ev_w_ukv[0])
    w1 = _layer1_weights(od_w_in[0])
    wo0, wo1 = ev_w_out[0].astype(BF16), od_w_out[0].astype(BF16)
    tabs_meta, tabs_real = _rope_tables()
    u, u_meta = _suffix_matrix(ATT_TILE), _suffix_matrix(BLOCK)
    l0 = (norm_g[0:1], w0, ev_q_norm_g[0:1], ev_kv_norm_g[0:1], wuq, wukv)

    hm = jnp.concatenate([jnp.zeros((N_PAD, D_MODEL), x.dtype), meta.astype(x.dtype)], axis=0)[None]
    qsb_m, ksb_m, vsb_m, gate_m, qm_m, km_m, vm_m = _proj0(hm, BLOCK, *l0, tabs_meta)
    osb_m = _sb_meta(qsb_m, ksb_m, vsb_m, gate_m, u_meta)
    omla_m = _mla_meta(qm_m, km_m, vm_m, gate_m)
    _, _, kv1_m, _ = _mid(osb_m, omla_m, hm, BLOCK, wo0, norm_g[1:2], w1)

    q_sb, k_sb, v_sb, gate0, q_mla, k_mla, v_mla = _proj0(x, ROW_TILE, *l0, tabs_real)
    o_sb = _sb_attn(q_sb, k_sb, v_sb, ksb_m, vsb_m, gate0, u, u_meta)
    o_mla = _mla_attn(q_mla, k_mla, v_mla, km_m, vm_m, gate0)
    h1, q1, kv1, gate1 = _mid(o_sb, o_mla, x, ROW_TILE, wo0, norm_g[1:2], w1)
    o1 = _swa_attn(od_sinks[0], q1, kv1, kv1_m, gate1)
    return _final(o1, h1, wo1, final_g[None, :])
```

```python
import math

import jax
import jax.numpy as jnp
from jax import lax
from jax.experimental import pallas as pl
from jax.experimental.pallas import tpu as pltpu

D_MODEL = 1024
SEQ = 2048
N_META = 16
BLOCK = 128
N_PAD = BLOCK - N_META
NORM_EPS = 1e-6
NEG = -1e30

SB_HEADS = 8
SB_DIM = 64
SB_WIDTH = SB_HEADS * SB_DIM
MLA_HEADS = 8
MLA_Q_LORA = 256
MLA_KV_LORA = 128
MLA_NOPE = 64
MLA_ROPE = 32
MLA_V = 64
MLA_WIDTH = MLA_HEADS * MLA_V
ROPE_BASE = 10000.0
SWA_HEADS = 16
SWA_KV_HEADS = 2
SWA_DIM = 64
SWA_WIDTH = SWA_HEADS * SWA_DIM

LANES = 128
ROW_TILE = 512
ATT_TILE = 256
ATT_BATCH = 2
VMEM_LIMIT = 48 * 1024 * 1024
LOG2E = math.log2(math.e)
SKEW = 1
SOFTPLUS_CLAMP = 64.0

BF16 = jnp.bfloat16
F32 = jnp.float32


def _dot(a, b):
    return jnp.dot(a, b, preferred_element_type=F32)


def _dot_nt(a, b):
    return lax.dot_general(a, b, (((1,), (1,)), ((), ())), preferred_element_type=F32)


def _rms(x, g):
    ms = jnp.mean(x * x, axis=-1, keepdims=True)
    return x * lax.rsqrt(ms + NORM_EPS) * g


def _silu(g):
    return g * (1.0 / (1.0 + jnp.exp(-g)))


def _params(sem):
    return pltpu.CompilerParams(dimension_semantics=sem, vmem_limit_bytes=VMEM_LIMIT)


def _iota(shape, dim):
    return lax.broadcasted_iota(jnp.int32, shape, dim)


def _split_pair(x):
    lane = _iota(x.shape, 1)
    zero = jnp.zeros_like(x)
    return jnp.where(lane < LANES // 2, x, zero), jnp.where(lane >= LANES // 2, x, zero)


def _merge_pair(lo, hi):
    lane = _iota(lo.shape, 1)
    return jnp.where(lane < LANES // 2, lo, hi)


def _proj0_kernel(x_ref, g_ref, w_ref, qg_ref, kvg_ref, wuq_ref, wukv_ref,
                  cq_ref, sq_ref, ck_ref, sk1_ref, sk2_ref,
                  qsb_ref, ksb_ref, vsb_ref, gate_ref, qm_ref, km_ref, vm_ref):
    rows = x_ref.shape[1]
    halves = [slice(0, rows // 2), slice(rows // 2, rows)] if rows >= 2 * BLOCK else [slice(0, rows)]
    qk_w = MLA_HEADS * LANES
    for r in halves:
        xn = _rms(x_ref[0, r, :], g_ref[...]).astype(BF16)
        qsb_ref[0, r, :] = (_dot(xn, w_ref[:, 0:512]) * LOG2E).astype(BF16)
        ksb_ref[0, r, :] = _dot(xn, w_ref[:, 512:1024]).astype(BF16)
        vsb_ref[0, r, :] = _dot(xn, w_ref[:, 1024:1536]).astype(BF16)
        gate_ref[0, r, :] = _dot(xn, w_ref[:, 1536:2560]).astype(BF16)
        lat = _dot(xn, w_ref[:, 2560:3072])
        cqn = _rms(lat[:, 0:256], qg_ref[...]).astype(BF16)
        ckvn = _rms(lat[:, 256:384], kvg_ref[...]).astype(BF16)
        kr = lat[:, 384:512]
        k_rope = (kr * ck_ref[r, :] + pltpu.roll(kr, LANES - MLA_ROPE // 2, 1) * sk1_ref[r, :]
                  + pltpu.roll(kr, MLA_ROPE // 2, 1) * sk2_ref[r, :])
        q_all = _dot(cqn, wuq_ref[...])
        kv_all = _dot(ckvn, wukv_ref[...])
        cq = cq_ref[r, :]
        sq = sq_ref[r, :]
        for h in range(MLA_HEADS):
            lo, hi = h * LANES, (h + 1) * LANES
            qm_ref[0, r, lo:hi] = (q_all[:, lo:hi] * cq + q_all[:, qk_w + lo:qk_w + hi] * sq).astype(BF16)
            km_ref[0, r, lo:hi] = (kv_all[:, lo:hi] + k_rope).astype(BF16)
        vm_ref[0, r, :] = kv_all[:, qk_w:].astype(BF16)


def _proj0(h, tile, g, w, qg, kvg, wuq, wukv, tables):
    B, L, _ = h.shape
    row = lambda width: pl.BlockSpec((1, tile, width), lambda b, t: (b, t, 0))
    full = lambda a: pl.BlockSpec(a.shape, lambda b, t: (0,) * a.ndim)
    tab = pl.BlockSpec((tile, LANES), lambda b, t: (t, 0))
    out = lambda width: jax.ShapeDtypeStruct((B, L, width), BF16)
    return pl.pallas_call(
        _proj0_kernel,
        grid=(B, L // tile),
        in_specs=[row(D_MODEL), full(g), full(w), full(qg), full(kvg), full(wuq), full(wukv),
                  tab, tab, tab, tab, tab],
        out_specs=[row(512), row(512), row(512), row(1024), row(1024), row(1024), row(512)],
        out_shape=[out(512), out(512), out(512), out(1024), out(1024), out(1024), out(512)],
        compiler_params=_params(("parallel", "parallel")),
        name="proj0",
    )(h, g, w, qg, kvg, wuq, wukv, *tables)


def _sb_chunk(qh_ref, kv, u, mask, acc_ref, c_ref):
    keys = kv[0][0].shape[0]
    items = [(b, h) for b in range(len(kv)) for h in range(SB_HEADS)]
    sl = lambda h: slice((h // 2) * LANES, (h // 2 + 1) * LANES)

    def scores(b, h):
        return _dot_nt(qh_ref[b, h], kv[b][0][:, sl(h)])

    def suffix(z):
        sp = jnp.maximum(jnp.log2(1.0 + jnp.exp2(jnp.minimum(z, SOFTPLUS_CLAMP))), z)
        if mask is not None:
            sp = jnp.where(mask, sp, 0.0)
        return _dot(sp.astype(BF16), u)

    def weights(b, h, z, sfx):
        c = c_ref[b, h]
        a = jnp.exp2(z + sfx + jnp.concatenate([c] * (keys // LANES), axis=1))
        if mask is not None:
            a = jnp.where(mask, a, 0.0)
        acc_ref[b, h] += _dot(a.astype(BF16), kv[b][1][:, sl(h)])
        c_ref[b, h] = c + jnp.broadcast_to(sfx[:, 0:1], c.shape)

    zs, sfxs = {}, {}
    for step in range(len(items) + 2 * SKEW):
        if step < len(items):
            zs[step] = scores(*items[step])
        if 0 <= step - SKEW < len(items):
            sfxs[step - SKEW] = suffix(zs[step - SKEW])
        if 0 <= step - 2 * SKEW < len(items):
            weights(*items[step - 2 * SKEW], zs.pop(step - 2 * SKEW), sfxs.pop(step - 2 * SKEW))


def _sb_prologue(q_ref, qh_ref, acc_ref, c_ref):
    for b in range(q_ref.shape[0]):
        for p in range(SB_HEADS // 2):
            qh_ref[b, 2 * p], qh_ref[b, 2 * p + 1] = _split_pair(q_ref[b, :, p * LANES:(p + 1) * LANES])
    acc_ref[...] = jnp.zeros_like(acc_ref)
    c_ref[...] = jnp.zeros_like(c_ref)


def _sb_epilogue(acc_ref, g_ref, o_ref):
    for b in range(o_ref.shape[0]):
        for p in range(SB_HEADS // 2):
            sl = slice(p * LANES, (p + 1) * LANES)
            o = _merge_pair(acc_ref[b, 2 * p], acc_ref[b, 2 * p + 1])
            o_ref[b, :, sl] = (o * _silu(g_ref[b, :, sl].astype(F32))).astype(BF16)


def _sb_kernel(q_ref, k_ref, v_ref, km_ref, vm_ref, g_ref, u_ref, um_ref, o_ref, qh_ref, acc_ref, c_ref):
    i = pl.program_id(1)
    T = ATT_TILE
    nb = q_ref.shape[0]
    _sb_prologue(q_ref, qh_ref, acc_ref, c_ref)

    def real_chunk(j, mask):
        start = pl.multiple_of(j * T, T)
        kv = [(k_ref[b, pl.ds(start, T), :], v_ref[b, pl.ds(start, T), :]) for b in range(nb)]
        _sb_chunk(qh_ref, kv, u_ref[...], mask, acc_ref, c_ref)

    real_chunk(i, _iota((T, T), 1) < _iota((T, T), 0))

    def earlier(idx, carry):
        real_chunk(i - 1 - idx, None)
        return carry

    lax.fori_loop(0, i, earlier, 0)
    _sb_chunk(qh_ref, [(km_ref[0], vm_ref[0])] * nb, um_ref[...], _iota((T, BLOCK), 1) >= N_PAD, acc_ref, c_ref)
    _sb_epilogue(acc_ref, g_ref, o_ref)


def _sb_attn(q, k, v, k_meta, v_meta, gate, u, u_meta):
    B, L, _ = q.shape
    T, nb = ATT_TILE, ATT_BATCH
    tile = pl.BlockSpec((nb, T, SB_WIDTH), lambda b, i: (b, i, 0))
    seq = pl.BlockSpec((nb, L, SB_WIDTH), lambda b, i: (b, 0, 0))
    meta = pl.BlockSpec((1, BLOCK, SB_WIDTH), lambda b, i: (0, 0, 0))
    const = lambda a: pl.BlockSpec(a.shape, lambda b, i: (0, 0))
    return pl.pallas_call(
        _sb_kernel,
        grid=(B // nb, L // T),
        in_specs=[tile, seq, seq, meta, meta, tile, const(u), const(u_meta)],
        out_specs=tile,
        out_shape=jax.ShapeDtypeStruct((B, L, SB_WIDTH), BF16),
        scratch_shapes=[pltpu.VMEM((nb, SB_HEADS, T, LANES), BF16), pltpu.VMEM((nb, SB_HEADS, T, LANES), F32),
                        pltpu.VMEM((nb, SB_HEADS, T, LANES), F32)],
        compiler_params=_params(("parallel", "arbitrary")),
        name="sb_attn",
    )(q, k, v, k_meta, v_meta, gate, u, u_meta)


def _sb_meta_kernel(q_ref, k_ref, v_ref, g_ref, u_ref, o_ref, qh_ref, acc_ref, c_ref):
    _sb_prologue(q_ref, qh_ref, acc_ref, c_ref)
    row, col = _iota((BLOCK, BLOCK), 0), _iota((BLOCK, BLOCK), 1)
    _sb_chunk(qh_ref, [(k_ref[0], v_ref[0])], u_ref[...], (col < row) & (col >= N_PAD), acc_ref, c_ref)
    _sb_epilogue(acc_ref, g_ref, o_ref)


def _sb_meta(q, k, v, gate, u_meta):
    blk = pl.BlockSpec((1, BLOCK, SB_WIDTH), lambda i: (0, 0, 0))
    return pl.pallas_call(
        _sb_meta_kernel,
        grid=(1,),
        in_specs=[blk, blk, blk, blk, pl.BlockSpec(u_meta.shape, lambda i: (0, 0))],
        out_specs=blk,
        out_shape=jax.ShapeDtypeStruct((1, BLOCK, SB_WIDTH), BF16),
        scratch_shapes=[pltpu.VMEM((1, SB_HEADS, BLOCK, LANES), BF16), pltpu.VMEM((1, SB_HEADS, BLOCK, LANES), F32),
                        pltpu.VMEM((1, SB_HEADS, BLOCK, LANES), F32)],
        compiler_params=_params(("arbitrary",)),
        name="sb_meta",
    )(q, k, v, gate, u_meta)


def _mla_chunk(q_ref, kv, mask, acc_ref, m_ref):
    keys = kv[0][0].shape[0]
    items = [(b, h) for b in range(len(kv)) for h in range(MLA_HEADS)]
    ones = jnp.ones((keys, LANES), BF16)

    def scores(b, h):
        hs = slice(h * LANES, (h + 1) * LANES)
        s = _dot_nt(q_ref[b, :, hs], kv[b][0][:, hs])
        return s if mask is None else jnp.where(mask, s, NEG)

    def probs(b, h, s):
        m_old = m_ref[b, h]
        m_new = jnp.maximum(m_old, jnp.max(s, axis=-1, keepdims=True))
        m_ref[b, h] = m_new
        p = jnp.exp2(s - jnp.concatenate([m_new] * (keys // LANES), axis=1))
        return p.astype(BF16), jnp.exp2(m_old - m_new)

    def update(b, h, p, alpha):
        vs = slice((h // 2) * LANES, (h // 2 + 1) * LANES)
        pv = _dot(p, jnp.concatenate([kv[b][1][:, vs], ones], axis=1))
        acc_ref[b, h] = jnp.concatenate([alpha, alpha], axis=1) * acc_ref[b, h] + pv

    ss, ps = {}, {}
    for step in range(len(items) + 2 * SKEW):
        if step < len(items):
            ss[step] = scores(*items[step])
        if 0 <= step - SKEW < len(items):
            ps[step - SKEW] = probs(*items[step - SKEW], ss.pop(step - SKEW))
        if 0 <= step - 2 * SKEW < len(items):
            update(*items[step - 2 * SKEW], *ps.pop(step - 2 * SKEW))


def _mla_finish(acc_ref, g_ref, o_ref):
    for b in range(o_ref.shape[0]):
        for p in range(MLA_HEADS // 2):
            sl = slice(p * LANES, (p + 1) * LANES)
            lo = acc_ref[b, 2 * p, :, :LANES] / acc_ref[b, 2 * p, :, LANES:]
            hi = acc_ref[b, 2 * p + 1, :, :LANES] / acc_ref[b, 2 * p + 1, :, LANES:]
            gate = g_ref[b, :, MLA_WIDTH + p * LANES:MLA_WIDTH + (p + 1) * LANES].astype(F32)
            o_ref[b, :, sl] = (_merge_pair(lo, hi) * _silu(gate)).astype(BF16)


def _mla_kernel(q_ref, k_ref, v_ref, km_ref, vm_ref, g_ref, o_ref, acc_ref, m_ref):
    i = pl.program_id(1)
    T = ATT_TILE
    nb = q_ref.shape[0]
    acc_ref[...] = jnp.zeros_like(acc_ref)
    m_ref[...] = jnp.full_like(m_ref, NEG)

    def real_chunk(j, mask):
        start = pl.multiple_of(j * T, T)
        kv = [(k_ref[b, pl.ds(start, T), :], v_ref[b, pl.ds(start, T), :]) for b in range(nb)]
        _mla_chunk(q_ref, kv, mask, acc_ref, m_ref)

    real_chunk(i, _iota((T, T), 1) <= _iota((T, T), 0))

    def earlier(idx, carry):
        real_chunk(idx, None)
        return carry

    lax.fori_loop(0, i, earlier, 0)
    _mla_chunk(q_ref, [(km_ref[0], vm_ref[0])] * nb, _iota((T, BLOCK), 1) >= N_PAD, acc_ref, m_ref)
    _mla_finish(acc_ref, g_ref, o_ref)


def _mla_attn(q, k, v, k_meta, v_meta, gate):
    B, L, _ = q.shape
    T, nb = ATT_TILE, ATT_BATCH
    qk_w = MLA_HEADS * LANES
    return pl.pallas_call(
        _mla_kernel,
        grid=(B // nb, L // T),
        in_specs=[pl.BlockSpec((nb, T, qk_w), lambda b, i: (b, i, 0)),
                  pl.BlockSpec((nb, L, qk_w), lambda b, i: (b, 0, 0)),
                  pl.BlockSpec((nb, L, MLA_WIDTH), lambda b, i: (b, 0, 0)),
                  pl.BlockSpec((1, BLOCK, qk_w), lambda b, i: (0, 0, 0)),
                  pl.BlockSpec((1, BLOCK, MLA_WIDTH), lambda b, i: (0, 0, 0)),
                  pl.BlockSpec((nb, T, 2 * MLA_WIDTH), lambda b, i: (b, i, 0))],
        out_specs=pl.BlockSpec((nb, T, MLA_WIDTH), lambda b, i: (b, i, 0)),
        out_shape=jax.ShapeDtypeStruct((B, L, MLA_WIDTH), BF16),
        scratch_shapes=[pltpu.VMEM((nb, MLA_HEADS, T, 2 * LANES), F32), pltpu.VMEM((nb, MLA_HEADS, T, LANES), F32)],
        compiler_params=_params(("parallel", "arbitrary")),
        name="mla_attn",
    )(q, k, v, k_meta, v_meta, gate)


def _mla_meta_kernel(q_ref, k_ref, v_ref, g_ref, o_ref, acc_ref, m_ref):
    acc_ref[...] = jnp.zeros_like(acc_ref)
    m_ref[...] = jnp.full_like(m_ref, NEG)
    row, col = _iota((BLOCK, BLOCK), 0), _iota((BLOCK, BLOCK), 1)
    _mla_chunk(q_ref, [(k_ref[0], v_ref[0])], (col <= row) & (col >= N_PAD), acc_ref, m_ref)
    _mla_finish(acc_ref, g_ref, o_ref)


def _mla_meta(q, k, v, gate):
    qk_w = MLA_HEADS * LANES
    blk = lambda w: pl.BlockSpec((1, BLOCK, w), lambda i: (0, 0, 0))
    return pl.pallas_call(
        _mla_meta_kernel,
        grid=(1,),
        in_specs=[blk(qk_w), blk(qk_w), blk(MLA_WIDTH), blk(2 * MLA_WIDTH)],
        out_specs=blk(MLA_WIDTH),
        out_shape=jax.ShapeDtypeStruct((1, BLOCK, MLA_WIDTH), BF16),
        scratch_shapes=[pltpu.VMEM((1, MLA_HEADS, BLOCK, 2 * LANES), F32),
                        pltpu.VMEM((1, MLA_HEADS, BLOCK, LANES), F32)],
        compiler_params=_params(("arbitrary",)),
        name="mla_meta",
    )(q, k, v, gate)


def _mid_kernel(osb_ref, omla_ref, h_ref, wo_ref, g_ref, w_ref, h1_ref, q_ref, kv_ref, gate_ref):
    rows = h_ref.shape[1]
    halves = [slice(0, rows // 2), slice(rows // 2, rows)] if rows >= 2 * BLOCK else [slice(0, rows)]
    ys = [_dot(osb_ref[0, r, :], wo_ref[0:512, :]) + _dot(omla_ref[0, r, :], wo_ref[512:1024, :]) for r in halves]
    for r, y in zip(halves, ys):
        h1 = h_ref[0, r, :] + y
        h1_ref[0, r, :] = h1
        xn = _rms(h1, g_ref[...]).astype(BF16)
        q_ref[0, r, :] = (_dot(xn, w_ref[:, 0:1024]) * LOG2E).astype(BF16)
        kv_ref[0, r, :] = _dot(xn, w_ref[:, 1024:1280]).astype(BF16)
        gate_ref[0, r, :] = _dot(xn, w_ref[:, 1280:2304]).astype(BF16)


def _mid(osb, omla, h, tile, wo, g, w):
    B, L, _ = h.shape
    row = lambda width: pl.BlockSpec((1, tile, width), lambda b, t: (b, t, 0))
    full = lambda a: pl.BlockSpec(a.shape, lambda b, t: (0,) * a.ndim)
    return pl.pallas_call(
        _mid_kernel,
        grid=(B, L // tile),
        in_specs=[row(512), row(512), row(D_MODEL), full(wo), full(g), full(w)],
        out_specs=[row(D_MODEL), row(1024), row(256), row(1024)],
        out_shape=[jax.ShapeDtypeStruct((B, L, D_MODEL), F32),
                   jax.ShapeDtypeStruct((B, L, 1024), BF16),
                   jax.ShapeDtypeStruct((B, L, 256), BF16),
                   jax.ShapeDtypeStruct((B, L, 1024), BF16)],
        compiler_params=_params(("parallel", "parallel")),
        name="mid",
    )(osb, omla, h, wo, g, w)


def _swa_kernel(sink_ref, q_ref, kvp_ref, kvc_ref, kvm_ref, g_ref, bb_ref, mb_ref, h_ref, wo_ref, fg_ref,
                out_ref, o_scr):
    n = pl.program_id(1)
    first = jnp.where(n > 0, 0, 1)
    in_cur = _iota((BLOCK, BLOCK), 1) <= _iota((BLOCK, BLOCK), 0)
    blocks_before = (n + 1).astype(F32) * float(BLOCK)
    pairs = SWA_HEADS // 2
    items = [(b, p, hh) for b in range(q_ref.shape[0]) for p in range(pairs) for hh in range(2)]
    k_sl, v_sl = slice(0, LANES), slice(LANES, 2 * LANES)

    def scores(b, p, hh):
        qh = _split_pair(q_ref[b, :, p * LANES:(p + 1) * LANES])[hh]
        k_band = jnp.concatenate([kvp_ref[b, :, k_sl], kvc_ref[b, :, k_sl]], axis=0)
        return _dot_nt(qh, k_band), _dot_nt(qh, kvm_ref[0, :, k_sl])

    def probs(h, z, z_meta):
        slope = 2.0 ** (-8.0 * (h + 1.0) / SWA_HEADS) * LOG2E
        s_band = jnp.where(in_cur, z[:, BLOCK:], z[:, :BLOCK]) - bb_ref[first, h]
        s_meta = z_meta - mb_ref[h] - slope * blocks_before
        sink = sink_ref[h] * LOG2E
        m = jnp.maximum(jnp.max(jnp.maximum(s_band, s_meta), axis=-1, keepdims=True), sink)
        p_band, p_meta = jnp.exp2(s_band - m), jnp.exp2(s_meta - m)
        denom = jnp.sum(p_band + p_meta, axis=-1, keepdims=True) + jnp.exp2(sink - m)
        p_all = jnp.concatenate([jnp.where(in_cur, 0.0, p_band), jnp.where(in_cur, p_band, 0.0), p_meta], axis=1)
        return p_all.astype(BF16), 1.0 / denom

    def values(b, p_all, inv):
        v_all = jnp.concatenate([kvp_ref[b, :, v_sl], kvc_ref[b, :, v_sl], kvm_ref[0, :, v_sl]], axis=0)
        return _dot(p_all, v_all) * inv

    zs, ps, outs = {}, {}, {}
    for step in range(len(items) + 2 * SKEW):
        if step < len(items):
            zs[step] = scores(*items[step])
        if 0 <= step - SKEW < len(items):
            _, p, hh = items[step - SKEW]
            ps[step - SKEW] = probs(p + pairs * hh, *zs.pop(step - SKEW))
        t = step - 2 * SKEW
        if 0 <= t < len(items):
            b, p, hh = items[t]
            outs[t] = values(b, *ps.pop(t))
            if hh == 1:
                sl = slice(p * LANES, (p + 1) * LANES)
                o = _merge_pair(outs.pop(t - 1), outs.pop(t))
                o_scr[b, :, sl] = (o * _silu(g_ref[b, :, sl].astype(F32))).astype(BF16)
                if p == pairs - 1:
                    out_ref[b] = _rms(h_ref[b] + _dot(o_scr[b], wo_ref[...]), fg_ref[...])


def _swa_bias_tables():
    row, col = _iota((BLOCK, BLOCK), 0), _iota((BLOCK, BLOCK), 1)
    slopes = (2.0 ** (-8.0 * (jnp.arange(SWA_HEADS, dtype=F32) + 1.0) / SWA_HEADS) * LOG2E)[:, None, None]
    band = slopes * jnp.where(col <= row, row - col, row - col + BLOCK).astype(F32)
    band_first = band + jnp.where(col <= row, 0.0, -NEG)
    meta = jnp.where(col >= N_PAD, 0.0, -NEG) + slopes * (row - col).astype(F32)
    return jnp.stack([band, band_first]), meta


def _swa_attn(sinks, q, kv, kv_meta, gate, h1, wo, final_g):
    B, L, _ = q.shape
    kvw = kv.shape[-1]
    nb = ATT_BATCH
    band_bias, meta_bias = _swa_bias_tables()
    row = lambda w: pl.BlockSpec((nb, BLOCK, w), lambda b, n, s: (b, n, 0))
    const = lambda a: pl.BlockSpec(a.shape, lambda b, n, s: (0,) * a.ndim)
    grid_spec = pltpu.PrefetchScalarGridSpec(
        num_scalar_prefetch=1,
        grid=(B // nb, L // BLOCK),
        in_specs=[row(SWA_WIDTH),
                  pl.BlockSpec((nb, BLOCK, kvw), lambda b, n, s: (b, jnp.maximum(n - 1, 0), 0)),
                  row(kvw), const(kv_meta), row(SWA_WIDTH), const(band_bias), const(meta_bias),
                  row(D_MODEL), const(wo), const(final_g)],
        out_specs=row(D_MODEL),
        scratch_shapes=[pltpu.VMEM((nb, BLOCK, SWA_WIDTH), BF16)],
    )
    return pl.pallas_call(
        _swa_kernel,
        grid_spec=grid_spec,
        out_shape=jax.ShapeDtypeStruct((B, L, D_MODEL), F32),
        compiler_params=_params(("parallel", "parallel")),
        name="swa_attn",
    )(sinks, q, kv, kv, kv_meta, gate, band_bias, meta_bias, h1, wo, final_g)


def _layer0_weights(w_in, w_uq, w_ukv):
    q, k, v, g_sb = (w_in[:, i * 512:(i + 1) * 512] for i in range(4))
    c_q, c_kv = w_in[:, 2048:2304], w_in[:, 2304:2432]
    k_r, g_mla = w_in[:, 2432:2464], w_in[:, 2464:2976]
    half = MLA_ROPE // 2
    r1, r2 = k_r[:, :half], k_r[:, half:]
    z = lambda n: jnp.zeros((D_MODEL, n), w_in.dtype)
    kr_blk = jnp.concatenate([z(MLA_NOPE), r1, r2, z(LANES - MLA_NOPE - MLA_ROPE)], axis=1)
    w0 = jnp.concatenate([q * (SB_DIM ** -0.5), k, v, g_sb, g_mla, c_q, c_kv, kr_blk], axis=1)

    uq = w_uq.reshape(MLA_Q_LORA, MLA_HEADS, MLA_NOPE + MLA_ROPE)
    nope, u1, u2 = uq[..., :MLA_NOPE], uq[..., MLA_NOPE:MLA_NOPE + half], uq[..., MLA_NOPE + half:]
    zq = lambda n: jnp.zeros((MLA_Q_LORA, MLA_HEADS, n), w_uq.dtype)
    uq_main = jnp.concatenate([nope, u1, u2, zq(LANES - MLA_NOPE - MLA_ROPE)], axis=-1)
    uq_rot = jnp.concatenate([zq(MLA_NOPE), -u2, u1, zq(LANES - MLA_NOPE - MLA_ROPE)], axis=-1)
    wuq = jnp.concatenate([uq_main.reshape(MLA_Q_LORA, -1), uq_rot.reshape(MLA_Q_LORA, -1)], axis=1)

    ukv = w_ukv.reshape(MLA_KV_LORA, MLA_HEADS, MLA_NOPE + MLA_V)
    k_nope = jnp.concatenate([ukv[..., :MLA_NOPE],
                              jnp.zeros((MLA_KV_LORA, MLA_HEADS, LANES - MLA_NOPE), w_ukv.dtype)], axis=-1)
    wukv = jnp.concatenate([k_nope.reshape(MLA_KV_LORA, -1),
                            ukv[..., MLA_NOPE:].reshape(MLA_KV_LORA, -1)], axis=1)
    return w0.astype(BF16), wuq.astype(BF16), wukv.astype(BF16)


def _pair_heads(w, axis):
    shape = w.shape
    w = w.reshape(shape[:axis] + (SWA_KV_HEADS, SWA_HEADS // SWA_KV_HEADS, SWA_DIM) + shape[axis + 1:])
    return jnp.swapaxes(w, axis, axis + 1).reshape(shape)


def _layer1_weights(w_in, w_out):
    q, kv, g = w_in[:, :1024], w_in[:, 1024:1280], w_in[:, 1280:2304]
    w1 = jnp.concatenate([_pair_heads(q * (SWA_DIM ** -0.5), 1), kv, _pair_heads(g, 1)], axis=1)
    return w1.astype(BF16), _pair_heads(w_out, 0).astype(BF16)


def _rope_tables():
    half = MLA_ROPE // 2
    pos = jnp.arange(N_META + SEQ).astype(F32)
    inv = ROPE_BASE ** (-jnp.arange(half, dtype=F32) / half)
    ang = pos[:, None] * inv[None, :]
    cos, sin = jnp.cos(ang), jnp.sin(ang)
    n = pos.shape[0]
    z = lambda w: jnp.zeros((n, w), F32)
    tail = LANES - MLA_NOPE - MLA_ROPE
    c = jnp.concatenate([jnp.ones((n, MLA_NOPE), F32), cos, cos, z(tail)], axis=1)
    s = jnp.concatenate([z(MLA_NOPE), sin, sin, z(tail)], axis=1)
    s1 = jnp.concatenate([z(MLA_NOPE), -sin, z(half), z(tail)], axis=1)
    s2 = jnp.concatenate([z(MLA_NOPE), z(half), sin, z(tail)], axis=1)
    scale = (MLA_NOPE + MLA_ROPE) ** -0.5 * LOG2E
    tabs = (c * scale, s * scale, c, s1, s2)
    pad = lambda t: jnp.concatenate([jnp.zeros((N_PAD, LANES), F32), t[:N_META]], axis=0)
    return tuple(pad(t) for t in tabs), tuple(t[N_META:] for t in tabs)


def _suffix_matrix(n):
    return jnp.where(_iota((n, n), 0) >= _iota((n, n), 1), -1.0, 0.0).astype(BF16)


def kernel(x, meta, norm_g, final_g, ev_w_in, ev_q_norm_g, ev_kv_norm_g, ev_w_uq, ev_w_ukv,
           ev_w_out, od_w_in, od_sinks, od_w_out):
    w0, wuq, wukv = _layer0_weights(ev_w_in[0], ev_w_uq[0], ev_w_ukv[0])
    w1, wo1 = _layer1_weights(od_w_in[0], od_w_out[0])
    wo0 = ev_w_out[0].astype(BF16)
    tabs_meta, tabs_real = _rope_tables()
    u, u_meta = _suffix_matrix(ATT_TILE), _suffix_matrix(BLOCK)
    l0 = (norm_g[0:1], w0, ev_q_norm_g[0:1], ev_kv_norm_g[0:1], wuq, wukv)

    hm = jnp.concatenate([jnp.zeros((N_PAD, D_MODEL), x.dtype), meta.astype(x.dtype)], axis=0)[None]
    qsb_m, ksb_m, vsb_m, gate_m, qm_m, km_m, vm_m = _proj0(hm, BLOCK, *l0, tabs_meta)
    osb_m = _sb_meta(qsb_m, ksb_m, vsb_m, gate_m, u_meta)
    omla_m = _mla_meta(qm_m, km_m, vm_m, gate_m)
    _, _, kv1_m, _ = _mid(osb_m, omla_m, hm, BLOCK, wo0, norm_g[1:2], w1)

    q_sb, k_sb, v_sb, gate0, q_mla, k_mla, v_mla = _proj0(x, ROW_TILE, *l0, tabs_real)
    o_sb = _sb_attn(q_sb, k_sb, v_sb, ksb_m, vsb_m, gate0, u, u_meta)
    o_mla = _mla_attn(q_mla, k_mla, v_mla, km_m, vm_m, gate0)
    h1, q1, kv1, gate1 = _mid(o_sb, o_mla, x, ROW_TILE, wo0, norm_g[1:2], w1)
    return _swa_attn(od_sinks[0], q1, kv1, kv1_m, gate1, h1, wo1, final_g[None, :])
```

```python
import math

import jax
import jax.numpy as jnp
from jax import lax
from jax.experimental import pallas as pl
from jax.experimental.pallas import tpu as pltpu

D_MODEL = 1024
SEQ = 2048
N_META = 16
BLOCK = 128
N_PAD = BLOCK - N_META
NORM_EPS = 1e-6
NEG = -1e30

SB_HEADS = 8
SB_DIM = 64
SB_WIDTH = SB_HEADS * SB_DIM
MLA_HEADS = 8
MLA_Q_LORA = 256
MLA_KV_LORA = 128
MLA_NOPE = 64
MLA_ROPE = 32
MLA_V = 64
MLA_WIDTH = MLA_HEADS * MLA_V
ROPE_BASE = 10000.0
SWA_HEADS = 16
SWA_KV_HEADS = 2
SWA_DIM = 64
SWA_WIDTH = SWA_HEADS * SWA_DIM

LANES = 128
ROW_TILE = 512
ATT_TILE = 256
ATT_BATCH = 2
VMEM_LIMIT = 48 * 1024 * 1024
LOG2E = math.log2(math.e)
SKEW = 1
DEAD_CARRY = -256.0
SOFTPLUS_CLAMP = 64.0

BF16 = jnp.bfloat16
F32 = jnp.float32


def _dot(a, b):
    return jnp.dot(a, b, preferred_element_type=F32)


def _dot_nt(a, b):
    return lax.dot_general(a, b, (((1,), (1,)), ((), ())), preferred_element_type=F32)


def _rms(x, g):
    ms = jnp.mean(x * x, axis=-1, keepdims=True)
    return x * lax.rsqrt(ms + NORM_EPS) * g


def _silu(g):
    return g * (1.0 / (1.0 + jnp.exp(-g)))


def _params(sem):
    return pltpu.CompilerParams(dimension_semantics=sem, vmem_limit_bytes=VMEM_LIMIT)


def _iota(shape, dim):
    return lax.broadcasted_iota(jnp.int32, shape, dim)


def _split_pair(x):
    lane = _iota(x.shape, 1)
    zero = jnp.zeros_like(x)
    return jnp.where(lane < LANES // 2, x, zero), jnp.where(lane >= LANES // 2, x, zero)


def _merge_pair(lo, hi):
    lane = _iota(lo.shape, 1)
    return jnp.where(lane < LANES // 2, lo, hi)


def _proj0_kernel(x_ref, g_ref, w_ref, qg_ref, kvg_ref, wuq_ref, wukv_ref,
                  cq_ref, sq_ref, ck_ref, sk1_ref, sk2_ref,
                  qsb_ref, ksb_ref, vsb_ref, gate_ref, qm_ref, km_ref, vm_ref):
    rows = x_ref.shape[1]
    halves = [slice(0, rows // 2), slice(rows // 2, rows)] if rows >= 2 * BLOCK else [slice(0, rows)]
    qk_w = MLA_HEADS * LANES
    for r in halves:
        xn = _rms(x_ref[0, r, :], g_ref[...]).astype(BF16)
        qsb_ref[0, r, :] = (_dot(xn, w_ref[:, 0:512]) * LOG2E).astype(BF16)
        ksb_ref[0, r, :] = _dot(xn, w_ref[:, 512:1024]).astype(BF16)
        vsb_ref[0, r, :] = _dot(xn, w_ref[:, 1024:1536]).astype(BF16)
        gate_ref[0, r, :] = _dot(xn, w_ref[:, 1536:2560]).astype(BF16)
        lat = _dot(xn, w_ref[:, 2560:3072])
        cqn = _rms(lat[:, 0:256], qg_ref[...]).astype(BF16)
        ckvn = _rms(lat[:, 256:384], kvg_ref[...]).astype(BF16)
        kr = lat[:, 384:512]
        k_rope = (kr * ck_ref[r, :] + pltpu.roll(kr, LANES - MLA_ROPE // 2, 1) * sk1_ref[r, :]
                  + pltpu.roll(kr, MLA_ROPE // 2, 1) * sk2_ref[r, :])
        q_all = _dot(cqn, wuq_ref[...])
        kv_all = _dot(ckvn, wukv_ref[...])
        cq = cq_ref[r, :]
        sq = sq_ref[r, :]
        for h in range(MLA_HEADS):
            lo, hi = h * LANES, (h + 1) * LANES
            qm_ref[0, r, lo:hi] = (q_all[:, lo:hi] * cq + q_all[:, qk_w + lo:qk_w + hi] * sq).astype(BF16)
            km_ref[0, r, lo:hi] = (kv_all[:, lo:hi] + k_rope).astype(BF16)
        vm_ref[0, r, :] = kv_all[:, qk_w:].astype(BF16)


def _proj0(h, tile, g, w, qg, kvg, wuq, wukv, tables):
    B, L, _ = h.shape
    row = lambda width: pl.BlockSpec((1, tile, width), lambda b, t: (b, t, 0))
    full = lambda a: pl.BlockSpec(a.shape, lambda b, t: (0,) * a.ndim)
    tab = pl.BlockSpec((tile, LANES), lambda b, t: (t, 0))
    out = lambda width: jax.ShapeDtypeStruct((B, L, width), BF16)
    return pl.pallas_call(
        _proj0_kernel,
        grid=(B, L // tile),
        in_specs=[row(D_MODEL), full(g), full(w), full(qg), full(kvg), full(wuq), full(wukv),
                  tab, tab, tab, tab, tab],
        out_specs=[row(512), row(512), row(512), row(1024), row(1024), row(1024), row(512)],
        out_shape=[out(512), out(512), out(512), out(1024), out(1024), out(1024), out(512)],
        compiler_params=_params(("parallel", "parallel")),
        name="proj0",
    )(h, g, w, qg, kvg, wuq, wukv, *tables)


def _sb_chunk(qh_ref, kv, u, mask, acc_ref, c_ref):
    keys = kv[0][0].shape[0]
    items = [(b, h) for b in range(len(kv)) for h in range(SB_HEADS)]
    sl = lambda h: slice((h // 2) * LANES, (h // 2 + 1) * LANES)

    def scores(b, h):
        return _dot_nt(qh_ref[b, h], kv[b][0][:, sl(h)])

    def suffix(z):
        sp = jnp.maximum(jnp.log2(1.0 + jnp.exp2(jnp.minimum(z, SOFTPLUS_CLAMP))), z)
        if mask is not None:
            sp = jnp.where(mask, sp, 0.0)
        return _dot(sp.astype(BF16), u)

    def weights(b, h, z, sfx):
        c = c_ref[b, h]
        a = jnp.exp2(z + sfx + jnp.concatenate([c] * (keys // LANES), axis=1))
        if mask is not None:
            a = jnp.where(mask, a, 0.0)
        acc_ref[b, h] += _dot(a.astype(BF16), kv[b][1][:, sl(h)])
        c_ref[b, h] = c + jnp.broadcast_to(sfx[:, 0:1], c.shape)

    zs, sfxs = {}, {}
    for step in range(len(items) + 2 * SKEW):
        if step < len(items):
            zs[step] = scores(*items[step])
        if 0 <= step - SKEW < len(items):
            sfxs[step - SKEW] = suffix(zs[step - SKEW])
        if 0 <= step - 2 * SKEW < len(items):
            weights(*items[step - 2 * SKEW], zs.pop(step - 2 * SKEW), sfxs.pop(step - 2 * SKEW))


def _sb_prologue(q_ref, qh_ref, acc_ref, c_ref):
    for b in range(q_ref.shape[0]):
        for p in range(SB_HEADS // 2):
            qh_ref[b, 2 * p], qh_ref[b, 2 * p + 1] = _split_pair(q_ref[b, :, p * LANES:(p + 1) * LANES])
    acc_ref[...] = jnp.zeros_like(acc_ref)
    c_ref[...] = jnp.zeros_like(c_ref)


def _sb_epilogue(acc_ref, g_ref, o_ref):
    for b in range(o_ref.shape[0]):
        for p in range(SB_HEADS // 2):
            sl = slice(p * LANES, (p + 1) * LANES)
            o = _merge_pair(acc_ref[b, 2 * p], acc_ref[b, 2 * p + 1])
            o_ref[b, :, sl] = (o * _silu(g_ref[b, :, sl].astype(F32))).astype(BF16)


def _sb_kernel(q_ref, k_ref, v_ref, km_ref, vm_ref, g_ref, u_ref, um_ref, o_ref, qh_ref, acc_ref, c_ref):
    i = pl.program_id(1)
    T = ATT_TILE
    nb = q_ref.shape[0]
    _sb_prologue(q_ref, qh_ref, acc_ref, c_ref)

    def real_chunk(j, mask):
        start = pl.multiple_of(j * T, T)
        kv = [(k_ref[b, pl.ds(start, T), :], v_ref[b, pl.ds(start, T), :]) for b in range(nb)]
        _sb_chunk(qh_ref, kv, u_ref[...], mask, acc_ref, c_ref)

    real_chunk(i, _iota((T, T), 1) < _iota((T, T), 0))

    def alive():
        return jnp.max(c_ref[...]) > DEAD_CARRY

    def earlier(state):
        idx, _ = state
        real_chunk(i - 1 - idx, None)
        return idx + 1, alive()

    _, still_alive = lax.while_loop(lambda state: (state[0] < i) & state[1], earlier, (jnp.int32(0), alive()))

    @pl.when(still_alive)
    def _():
        _sb_chunk(qh_ref, [(km_ref[0], vm_ref[0])] * nb, um_ref[...], _iota((T, BLOCK), 1) >= N_PAD,
                  acc_ref, c_ref)

    _sb_epilogue(acc_ref, g_ref, o_ref)


def _sb_attn(q, k, v, k_meta, v_meta, gate, u, u_meta):
    B, L, _ = q.shape
    T, nb = ATT_TILE, ATT_BATCH
    tile = pl.BlockSpec((nb, T, SB_WIDTH), lambda b, i: (b, i, 0))
    seq = pl.BlockSpec((nb, L, SB_WIDTH), lambda b, i: (b, 0, 0))
    meta = pl.BlockSpec((1, BLOCK, SB_WIDTH), lambda b, i: (0, 0, 0))
    const = lambda a: pl.BlockSpec(a.shape, lambda b, i: (0, 0))
    return pl.pallas_call(
        _sb_kernel,
        grid=(B // nb, L // T),
        in_specs=[tile, seq, seq, meta, meta, tile, const(u), const(u_meta)],
        out_specs=tile,
        out_shape=jax.ShapeDtypeStruct((B, L, SB_WIDTH), BF16),
        scratch_shapes=[pltpu.VMEM((nb, SB_HEADS, T, LANES), BF16), pltpu.VMEM((nb, SB_HEADS, T, LANES), F32),
                        pltpu.VMEM((nb, SB_HEADS, T, LANES), F32)],
        compiler_params=_params(("parallel", "arbitrary")),
        name="sb_attn",
    )(q, k, v, k_meta, v_meta, gate, u, u_meta)


def _sb_meta_kernel(q_ref, k_ref, v_ref, g_ref, u_ref, o_ref, qh_ref, acc_ref, c_ref):
    _sb_prologue(q_ref, qh_ref, acc_ref, c_ref)
    row, col = _iota((BLOCK, BLOCK), 0), _iota((BLOCK, BLOCK), 1)
    _sb_chunk(qh_ref, [(k_ref[0], v_ref[0])], u_ref[...], (col < row) & (col >= N_PAD), acc_ref, c_ref)
    _sb_epilogue(acc_ref, g_ref, o_ref)


def _sb_meta(q, k, v, gate, u_meta):
    blk = pl.BlockSpec((1, BLOCK, SB_WIDTH), lambda i: (0, 0, 0))
    return pl.pallas_call(
        _sb_meta_kernel,
        grid=(1,),
        in_specs=[blk, blk, blk, blk, pl.BlockSpec(u_meta.shape, lambda i: (0, 0))],
        out_specs=blk,
        out_shape=jax.ShapeDtypeStruct((1, BLOCK, SB_WIDTH), BF16),
        scratch_shapes=[pltpu.VMEM((1, SB_HEADS, BLOCK, LANES), BF16), pltpu.VMEM((1, SB_HEADS, BLOCK, LANES), F32),
                        pltpu.VMEM((1, SB_HEADS, BLOCK, LANES), F32)],
        compiler_params=_params(("arbitrary",)),
        name="sb_meta",
    )(q, k, v, gate, u_meta)


def _mla_chunk(q_ref, kv, mask, acc_ref, m_ref):
    keys = kv[0][0].shape[0]
    items = [(b, h) for b in range(len(kv)) for h in range(MLA_HEADS)]
    ones = jnp.ones((keys, LANES), BF16)

    def scores(b, h):
        hs = slice(h * LANES, (h + 1) * LANES)
        s = _dot_nt(q_ref[b, :, hs], kv[b][0][:, hs])
        return s if mask is None else jnp.where(mask, s, NEG)

    def probs(b, h, s):
        m_old = m_ref[b, h]
        m_new = jnp.maximum(m_old, jnp.max(s, axis=-1, keepdims=True))
        m_ref[b, h] = m_new
        p = jnp.exp2(s - jnp.concatenate([m_new] * (keys // LANES), axis=1))
        return p.astype(BF16), jnp.exp2(m_old - m_new)

    def update(b, h, p, alpha):
        vs = slice((h // 2) * LANES, (h // 2 + 1) * LANES)
        pv = _dot(p, jnp.concatenate([kv[b][1][:, vs], ones], axis=1))
        acc_ref[b, h] = jnp.concatenate([alpha, alpha], axis=1) * acc_ref[b, h] + pv

    ss, ps = {}, {}
    for step in range(len(items) + 2 * SKEW):
        if step < len(items):
            ss[step] = scores(*items[step])
        if 0 <= step - SKEW < len(items):
            ps[step - SKEW] = probs(*items[step - SKEW], ss.pop(step - SKEW))
        if 0 <= step - 2 * SKEW < len(items):
            update(*items[step - 2 * SKEW], *ps.pop(step - 2 * SKEW))


def _mla_finish(acc_ref, g_ref, o_ref):
    for b in range(o_ref.shape[0]):
        for p in range(MLA_HEADS // 2):
            sl = slice(p * LANES, (p + 1) * LANES)
            lo = acc_ref[b, 2 * p, :, :LANES] / acc_ref[b, 2 * p, :, LANES:]
            hi = acc_ref[b, 2 * p + 1, :, :LANES] / acc_ref[b, 2 * p + 1, :, LANES:]
            gate = g_ref[b, :, MLA_WIDTH + p * LANES:MLA_WIDTH + (p + 1) * LANES].astype(F32)
            o_ref[b, :, sl] = (_merge_pair(lo, hi) * _silu(gate)).astype(BF16)


def _mla_kernel(q_ref, k_ref, v_ref, km_ref, vm_ref, g_ref, o_ref, acc_ref, m_ref):
    i = pl.program_id(1)
    T = ATT_TILE
    nb = q_ref.shape[0]
    acc_ref[...] = jnp.zeros_like(acc_ref)
    m_ref[...] = jnp.full_like(m_ref, NEG)

    def real_chunk(j, mask):
        start = pl.multiple_of(j * T, T)
        kv = [(k_ref[b, pl.ds(start, T), :], v_ref[b, pl.ds(start, T), :]) for b in range(nb)]
        _mla_chunk(q_ref, kv, mask, acc_ref, m_ref)

    real_chunk(i, _iota((T, T), 1) <= _iota((T, T), 0))

    def earlier(idx, carry):
        real_chunk(idx, None)
        return carry

    lax.fori_loop(0, i, earlier, 0)
    _mla_chunk(q_ref, [(km_ref[0], vm_ref[0])] * nb, _iota((T, BLOCK), 1) >= N_PAD, acc_ref, m_ref)
    _mla_finish(acc_ref, g_ref, o_ref)


def _mla_attn(q, k, v, k_meta, v_meta, gate):
    B, L, _ = q.shape
    T, nb = ATT_TILE, ATT_BATCH
    qk_w = MLA_HEADS * LANES
    return pl.pallas_call(
        _mla_kernel,
        grid=(B // nb, L // T),
        in_specs=[pl.BlockSpec((nb, T, qk_w), lambda b, i: (b, i, 0)),
                  pl.BlockSpec((nb, L, qk_w), lambda b, i: (b, 0, 0)),
                  pl.BlockSpec((nb, L, MLA_WIDTH), lambda b, i: (b, 0, 0)),
                  pl.BlockSpec((1, BLOCK, qk_w), lambda b, i: (0, 0, 0)),
                  pl.BlockSpec((1, BLOCK, MLA_WIDTH), lambda b, i: (0, 0, 0)),
                  pl.BlockSpec((nb, T, 2 * MLA_WIDTH), lambda b, i: (b, i, 0))],
        out_specs=pl.BlockSpec((nb, T, MLA_WIDTH), lambda b, i: (b, i, 0)),
        out_shape=jax.ShapeDtypeStruct((B, L, MLA_WIDTH), BF16),
        scratch_shapes=[pltpu.VMEM((nb, MLA_HEADS, T, 2 * LANES), F32), pltpu.VMEM((nb, MLA_HEADS, T, LANES), F32)],
        compiler_params=_params(("parallel", "arbitrary")),
        name="mla_attn",
    )(q, k, v, k_meta, v_meta, gate)


def _mla_meta_kernel(q_ref, k_ref, v_ref, g_ref, o_ref, acc_ref, m_ref):
    acc_ref[...] = jnp.zeros_like(acc_ref)
    m_ref[...] = jnp.full_like(m_ref, NEG)
    row, col = _iota((BLOCK, BLOCK), 0), _iota((BLOCK, BLOCK), 1)
    _mla_chunk(q_ref, [(k_ref[0], v_ref[0])], (col <= row) & (col >= N_PAD), acc_ref, m_ref)
    _mla_finish(acc_ref, g_ref, o_ref)


def _mla_meta(q, k, v, gate):
    qk_w = MLA_HEADS * LANES
    blk = lambda w: pl.BlockSpec((1, BLOCK, w), lambda i: (0, 0, 0))
    return pl.pallas_call(
        _mla_meta_kernel,
        grid=(1,),
        in_specs=[blk(qk_w), blk(qk_w), blk(MLA_WIDTH), blk(2 * MLA_WIDTH)],
        out_specs=blk(MLA_WIDTH),
        out_shape=jax.ShapeDtypeStruct((1, BLOCK, MLA_WIDTH), BF16),
        scratch_shapes=[pltpu.VMEM((1, MLA_HEADS, BLOCK, 2 * LANES), F32),
                        pltpu.VMEM((1, MLA_HEADS, BLOCK, LANES), F32)],
        compiler_params=_params(("arbitrary",)),
        name="mla_meta",
    )(q, k, v, gate)


def _mid_kernel(osb_ref, omla_ref, h_ref, wo_ref, g_ref, w_ref, h1_ref, q_ref, kv_ref, gate_ref):
    rows = h_ref.shape[1]
    halves = [slice(0, rows // 2), slice(rows // 2, rows)] if rows >= 2 * BLOCK else [slice(0, rows)]
    ys = [_dot(osb_ref[0, r, :], wo_ref[0:512, :]) + _dot(omla_ref[0, r, :], wo_ref[512:1024, :]) for r in halves]
    for r, y in zip(halves, ys):
        h1 = h_ref[0, r, :] + y
        h1_ref[0, r, :] = h1
        xn = _rms(h1, g_ref[...]).astype(BF16)
        q_ref[0, r, :] = (_dot(xn, w_ref[:, 0:1024]) * LOG2E).astype(BF16)
        kv_ref[0, r, :] = _dot(xn, w_ref[:, 1024:1280]).astype(BF16)
        gate_ref[0, r, :] = _dot(xn, w_ref[:, 1280:2304]).astype(BF16)


def _mid(osb, omla, h, tile, wo, g, w):
    B, L, _ = h.shape
    row = lambda width: pl.BlockSpec((1, tile, width), lambda b, t: (b, t, 0))
    full = lambda a: pl.BlockSpec(a.shape, lambda b, t: (0,) * a.ndim)
    return pl.pallas_call(
        _mid_kernel,
        grid=(B, L // tile),
        in_specs=[row(512), row(512), row(D_MODEL), full(wo), full(g), full(w)],
        out_specs=[row(D_MODEL), row(1024), row(256), row(1024)],
        out_shape=[jax.ShapeDtypeStruct((B, L, D_MODEL), F32),
                   jax.ShapeDtypeStruct((B, L, 1024), BF16),
                   jax.ShapeDtypeStruct((B, L, 256), BF16),
                   jax.ShapeDtypeStruct((B, L, 1024), BF16)],
        compiler_params=_params(("parallel", "parallel")),
        name="mid",
    )(osb, omla, h, wo, g, w)


def _swa_kernel(sink_ref, q_ref, kvp_ref, kvc_ref, kvm_ref, g_ref, bb_ref, mb_ref, h_ref, wo_ref, fg_ref,
                out_ref, o_scr):
    n = pl.program_id(1)
    first = jnp.where(n > 0, 0, 1)
    in_cur = _iota((BLOCK, BLOCK), 1) <= _iota((BLOCK, BLOCK), 0)
    blocks_before = (n + 1).astype(F32) * float(BLOCK)
    pairs = SWA_HEADS // 2
    items = [(b, p, hh) for b in range(q_ref.shape[0]) for p in range(pairs) for hh in range(2)]
    k_sl, v_sl = slice(0, LANES), slice(LANES, 2 * LANES)

    def scores(b, p, hh):
        qh = _split_pair(q_ref[b, :, p * LANES:(p + 1) * LANES])[hh]
        k_band = jnp.concatenate([kvp_ref[b, :, k_sl], kvc_ref[b, :, k_sl]], axis=0)
        return _dot_nt(qh, k_band), _dot_nt(qh, kvm_ref[0, :, k_sl])

    def probs(h, z, z_meta):
        slope = 2.0 ** (-8.0 * (h + 1.0) / SWA_HEADS) * LOG2E
        s_band = jnp.where(in_cur, z[:, BLOCK:], z[:, :BLOCK]) - bb_ref[first, h]
        s_meta = z_meta - mb_ref[h] - slope * blocks_before
        sink = sink_ref[h] * LOG2E
        m = jnp.maximum(jnp.max(jnp.maximum(s_band, s_meta), axis=-1, keepdims=True), sink)
        p_band, p_meta = jnp.exp2(s_band - m), jnp.exp2(s_meta - m)
        denom = jnp.sum(p_band + p_meta, axis=-1, keepdims=True) + jnp.exp2(sink - m)
        p_all = jnp.concatenate([jnp.where(in_cur, 0.0, p_band), jnp.where(in_cur, p_band, 0.0), p_meta], axis=1)
        return p_all.astype(BF16), 1.0 / denom

    def values(b, p_all, inv):
        v_all = jnp.concatenate([kvp_ref[b, :, v_sl], kvc_ref[b, :, v_sl], kvm_ref[0, :, v_sl]], axis=0)
        return _dot(p_all, v_all) * inv

    zs, ps, outs = {}, {}, {}
    for step in range(len(items) + 2 * SKEW):
        if step < len(items):
            zs[step] = scores(*items[step])
        if 0 <= step - SKEW < len(items):
            _, p, hh = items[step - SKEW]
            ps[step - SKEW] = probs(p + pairs * hh, *zs.pop(step - SKEW))
        t = step - 2 * SKEW
        if 0 <= t < len(items):
            b, p, hh = items[t]
            outs[t] = values(b, *ps.pop(t))
            if hh == 1:
                sl = slice(p * LANES, (p + 1) * LANES)
                o = _merge_pair(outs.pop(t - 1), outs.pop(t))
                o_scr[b, :, sl] = (o * _silu(g_ref[b, :, sl].astype(F32))).astype(BF16)
                if p == pairs - 1:
                    out_ref[b] = _rms(h_ref[b] + _dot(o_scr[b], wo_ref[...]), fg_ref[...])


def _swa_bias_tables():
    row, col = _iota((BLOCK, BLOCK), 0), _iota((BLOCK, BLOCK), 1)
    slopes = (2.0 ** (-8.0 * (jnp.arange(SWA_HEADS, dtype=F32) + 1.0) / SWA_HEADS) * LOG2E)[:, None, None]
    band = slopes * jnp.where(col <= row, row - col, row - col + BLOCK).astype(F32)
    band_first = band + jnp.where(col <= row, 0.0, -NEG)
    meta = jnp.where(col >= N_PAD, 0.0, -NEG) + slopes * (row - col).astype(F32)
    return jnp.stack([band, band_first]), meta


def _swa_attn(sinks, q, kv, kv_meta, gate, h1, wo, final_g):
    B, L, _ = q.shape
    kvw = kv.shape[-1]
    nb = ATT_BATCH
    band_bias, meta_bias = _swa_bias_tables()
    row = lambda w: pl.BlockSpec((nb, BLOCK, w), lambda b, n, s: (b, n, 0))
    const = lambda a: pl.BlockSpec(a.shape, lambda b, n, s: (0,) * a.ndim)
    grid_spec = pltpu.PrefetchScalarGridSpec(
        num_scalar_prefetch=1,
        grid=(B // nb, L // BLOCK),
        in_specs=[row(SWA_WIDTH),
                  pl.BlockSpec((nb, BLOCK, kvw), lambda b, n, s: (b, jnp.maximum(n - 1, 0), 0)),
                  row(kvw), const(kv_meta), row(SWA_WIDTH), const(band_bias), const(meta_bias),
                  row(D_MODEL), const(wo), const(final_g)],
        out_specs=row(D_MODEL),
        scratch_shapes=[pltpu.VMEM((nb, BLOCK, SWA_WIDTH), BF16)],
    )
    return pl.pallas_call(
        _swa_kernel,
        grid_spec=grid_spec,
        out_shape=jax.ShapeDtypeStruct((B, L, D_MODEL), F32),
        compiler_params=_params(("parallel", "parallel")),
        name="swa_attn",
    )(sinks, q, kv, kv, kv_meta, gate, band_bias, meta_bias, h1, wo, final_g)


def _layer0_weights(w_in, w_uq, w_ukv):
    q, k, v, g_sb = (w_in[:, i * 512:(i + 1) * 512] for i in range(4))
    c_q, c_kv = w_in[:, 2048:2304], w_in[:, 2304:2432]
    k_r, g_mla = w_in[:, 2432:2464], w_in[:, 2464:2976]
    half = MLA_ROPE // 2
    r1, r2 = k_r[:, :half], k_r[:, half:]
    z = lambda n: jnp.zeros((D_MODEL, n), w_in.dtype)
    kr_blk = jnp.concatenate([z(MLA_NOPE), r1, r2, z(LANES - MLA_NOPE - MLA_ROPE)], axis=1)
    w0 = jnp.concatenate([q * (SB_DIM ** -0.5), k, v, g_sb, g_mla, c_q, c_kv, kr_blk], axis=1)

    uq = w_uq.reshape(MLA_Q_LORA, MLA_HEADS, MLA_NOPE + MLA_ROPE)
    nope, u1, u2 = uq[..., :MLA_NOPE], uq[..., MLA_NOPE:MLA_NOPE + half], uq[..., MLA_NOPE + half:]
    zq = lambda n: jnp.zeros((MLA_Q_LORA, MLA_HEADS, n), w_uq.dtype)
    uq_main = jnp.concatenate([nope, u1, u2, zq(LANES - MLA_NOPE - MLA_ROPE)], axis=-1)
    uq_rot = jnp.concatenate([zq(MLA_NOPE), -u2, u1, zq(LANES - MLA_NOPE - MLA_ROPE)], axis=-1)
    wuq = jnp.concatenate([uq_main.reshape(MLA_Q_LORA, -1), uq_rot.reshape(MLA_Q_LORA, -1)], axis=1)

    ukv = w_ukv.reshape(MLA_KV_LORA, MLA_HEADS, MLA_NOPE + MLA_V)
    k_nope = jnp.concatenate([ukv[..., :MLA_NOPE],
                              jnp.zeros((MLA_KV_LORA, MLA_HEADS, LANES - MLA_NOPE), w_ukv.dtype)], axis=-1)
    wukv = jnp.concatenate([k_nope.reshape(MLA_KV_LORA, -1),
                            ukv[..., MLA_NOPE:].reshape(MLA_KV_LORA, -1)], axis=1)
    return w0.astype(BF16), wuq.astype(BF16), wukv.astype(BF16)


def _pair_heads(w, axis):
    shape = w.shape
    w = w.reshape(shape[:axis] + (SWA_KV_HEADS, SWA_HEADS // SWA_KV_HEADS, SWA_DIM) + shape[axis + 1:])
    return jnp.swapaxes(w, axis, axis + 1).reshape(shape)


def _layer1_weights(w_in, w_out):
    q, kv, g = w_in[:, :1024], w_in[:, 1024:1280], w_in[:, 1280:2304]
    w1 = jnp.concatenate([_pair_heads(q * (SWA_DIM ** -0.5), 1), kv, _pair_heads(g, 1)], axis=1)
    return w1.astype(BF16), _pair_heads(w_out, 0).astype(BF16)


def _rope_tables():
    half = MLA_ROPE // 2
    pos = jnp.arange(N_META + SEQ).astype(F32)
    inv = ROPE_BASE ** (-jnp.arange(half, dtype=F32) / half)
    ang = pos[:, None] * inv[None, :]
    cos, sin = jnp.cos(ang), jnp.sin(ang)
    n = pos.shape[0]
    z = lambda w: jnp.zeros((n, w), F32)
    tail = LANES - MLA_NOPE - MLA_ROPE
    c = jnp.concatenate([jnp.ones((n, MLA_NOPE), F32), cos, cos, z(tail)], axis=1)
    s = jnp.concatenate([z(MLA_NOPE), sin, sin, z(tail)], axis=1)
    s1 = jnp.concatenate([z(MLA_NOPE), -sin, z(half), z(tail)], axis=1)
    s2 = jnp.concatenate([z(MLA_NOPE), z(half), sin, z(tail)], axis=1)
    scale = (MLA_NOPE + MLA_ROPE) ** -0.5 * LOG2E
    tabs = (c * scale, s * scale, c, s1, s2)
    pad = lambda t: jnp.concatenate([jnp.zeros((N_PAD, LANES), F32), t[:N_META]], axis=0)
    return tuple(pad(t) for t in tabs), tuple(t[N_META:] for t in tabs)


def _suffix_matrix(n):
    return jnp.where(_iota((n, n), 0) >= _iota((n, n), 1), -1.0, 0.0).astype(BF16)


def kernel(x, meta, norm_g, final_g, ev_w_in, ev_q_norm_g, ev_kv_norm_g, ev_w_uq, ev_w_ukv,
           ev_w_out, od_w_in, od_sinks, od_w_out):
    w0, wuq, wukv = _layer0_weights(ev_w_in[0], ev_w_uq[0], ev_w_ukv[0])
    w1, wo1 = _layer1_weights(od_w_in[0], od_w_out[0])
    wo0 = ev_w_out[0].astype(BF16)
    tabs_meta, tabs_real = _rope_tables()
    u, u_meta = _suffix_matrix(ATT_TILE), _suffix_matrix(BLOCK)
    l0 = (norm_g[0:1], w0, ev_q_norm_g[0:1], ev_kv_norm_g[0:1], wuq, wukv)

    hm = jnp.concatenate([jnp.zeros((N_PAD, D_MODEL), x.dtype), meta.astype(x.dtype)], axis=0)[None]
    qsb_m, ksb_m, vsb_m, gate_m, qm_m, km_m, vm_m = _proj0(hm, BLOCK, *l0, tabs_meta)
    osb_m = _sb_meta(qsb_m, ksb_m, vsb_m, gate_m, u_meta)
    omla_m = _mla_meta(qm_m, km_m, vm_m, gate_m)
    _, _, kv1_m, _ = _mid(osb_m, omla_m, hm, BLOCK, wo0, norm_g[1:2], w1)

    q_sb, k_sb, v_sb, gate0, q_mla, k_mla, v_mla = _proj0(x, ROW_TILE, *l0, tabs_real)
    o_sb = _sb_attn(q_sb, k_sb, v_sb, ksb_m, vsb_m, gate0, u, u_meta)
    o_mla = _mla_attn(q_mla, k_mla, v_mla, km_m, vm_m, gate0)
    h1, q1, kv1, gate1 = _mid(o_sb, o_mla, x, ROW_TILE, wo0, norm_g[1:2], w1)
    return _swa_attn(od_sinks[0], q1, kv1, kv1_m, gate1, h1, wo1, final_g[None, :])
```

```python
import math

import jax
import jax.numpy as jnp
from jax import lax
from jax.experimental import pallas as pl
from jax.experimental.pallas import tpu as pltpu

D_MODEL = 1024
SEQ = 2048
N_META = 16
BLOCK = 128
N_PAD = BLOCK - N_META
NORM_EPS = 1e-6
NEG = -1e30

SB_HEADS = 8
SB_DIM = 64
SB_WIDTH = SB_HEADS * SB_DIM
MLA_HEADS = 8
MLA_Q_LORA = 256
MLA_KV_LORA = 128
MLA_NOPE = 64
MLA_ROPE = 32
MLA_V = 64
MLA_WIDTH = MLA_HEADS * MLA_V
ROPE_BASE = 10000.0
SWA_HEADS = 16
SWA_KV_HEADS = 2
SWA_DIM = 64
SWA_WIDTH = SWA_HEADS * SWA_DIM

LANES = 128
ROW_TILE = 512
ATT_TILE = 256
ATT_BATCH = 2
VMEM_LIMIT = 48 * 1024 * 1024
LOG2E = math.log2(math.e)
SWA_BATCH = 4
SWA_STACK = 2
SKEW = 1
DEAD_CARRY = -256.0
SOFTPLUS_CLAMP = 64.0

BF16 = jnp.bfloat16
F32 = jnp.float32


def _dot(a, b):
    return jnp.dot(a, b, preferred_element_type=F32)


def _dot_nt(a, b):
    return lax.dot_general(a, b, (((1,), (1,)), ((), ())), preferred_element_type=F32)


def _rms(x, g):
    ms = jnp.mean(x * x, axis=-1, keepdims=True)
    return x * lax.rsqrt(ms + NORM_EPS) * g


def _silu(g):
    return g * (1.0 / (1.0 + jnp.exp(-g)))


def _params(sem):
    return pltpu.CompilerParams(dimension_semantics=sem, vmem_limit_bytes=VMEM_LIMIT)


def _iota(shape, dim):
    return lax.broadcasted_iota(jnp.int32, shape, dim)


def _split_pair(x):
    lane = _iota(x.shape, 1)
    zero = jnp.zeros_like(x)
    return jnp.where(lane < LANES // 2, x, zero), jnp.where(lane >= LANES // 2, x, zero)


def _merge_pair(lo, hi):
    lane = _iota(lo.shape, 1)
    return jnp.where(lane < LANES // 2, lo, hi)


def _proj0_kernel(x_ref, g_ref, w_ref, qg_ref, kvg_ref, wuq_ref, wukv_ref,
                  cq_ref, sq_ref, ck_ref, sk1_ref, sk2_ref,
                  qsb_ref, ksb_ref, vsb_ref, gate_ref, qm_ref, km_ref, vm_ref):
    rows = x_ref.shape[1]
    halves = [slice(0, rows // 2), slice(rows // 2, rows)] if rows >= 2 * BLOCK else [slice(0, rows)]
    qk_w = MLA_HEADS * LANES
    for r in halves:
        xn = _rms(x_ref[0, r, :], g_ref[...]).astype(BF16)
        qsb_ref[0, r, :] = (_dot(xn, w_ref[:, 0:512]) * LOG2E).astype(BF16)
        ksb_ref[0, r, :] = _dot(xn, w_ref[:, 512:1024]).astype(BF16)
        vsb_ref[0, r, :] = _dot(xn, w_ref[:, 1024:1536]).astype(BF16)
        gate_ref[0, r, :] = _dot(xn, w_ref[:, 1536:2560]).astype(BF16)
        lat = _dot(xn, w_ref[:, 2560:3072])
        cqn = _rms(lat[:, 0:256], qg_ref[...]).astype(BF16)
        ckvn = _rms(lat[:, 256:384], kvg_ref[...]).astype(BF16)
        kr = lat[:, 384:512]
        k_rope = (kr * ck_ref[r, :] + pltpu.roll(kr, LANES - MLA_ROPE // 2, 1) * sk1_ref[r, :]
                  + pltpu.roll(kr, MLA_ROPE // 2, 1) * sk2_ref[r, :])
        q_all = _dot(cqn, wuq_ref[...])
        kv_all = _dot(ckvn, wukv_ref[...])
        cq = cq_ref[r, :]
        sq = sq_ref[r, :]
        for h in range(MLA_HEADS):
            lo, hi = h * LANES, (h + 1) * LANES
            qm_ref[0, r, lo:hi] = (q_all[:, lo:hi] * cq + q_all[:, qk_w + lo:qk_w + hi] * sq).astype(BF16)
            km_ref[0, r, lo:hi] = (kv_all[:, lo:hi] + k_rope).astype(BF16)
        vm_ref[0, r, :] = kv_all[:, qk_w:].astype(BF16)


def _proj0(h, tile, g, w, qg, kvg, wuq, wukv, tables):
    B, L, _ = h.shape
    row = lambda width: pl.BlockSpec((1, tile, width), lambda b, t: (b, t, 0))
    full = lambda a: pl.BlockSpec(a.shape, lambda b, t: (0,) * a.ndim)
    tab = pl.BlockSpec((tile, LANES), lambda b, t: (t, 0))
    out = lambda width: jax.ShapeDtypeStruct((B, L, width), BF16)
    return pl.pallas_call(
        _proj0_kernel,
        grid=(B, L // tile),
        in_specs=[row(D_MODEL), full(g), full(w), full(qg), full(kvg), full(wuq), full(wukv),
                  tab, tab, tab, tab, tab],
        out_specs=[row(512), row(512), row(512), row(1024), row(1024), row(1024), row(512)],
        out_shape=[out(512), out(512), out(512), out(1024), out(1024), out(1024), out(512)],
        compiler_params=_params(("parallel", "parallel")),
        name="proj0",
    )(h, g, w, qg, kvg, wuq, wukv, *tables)


def _sb_chunk(qh_ref, kv, u, mask, acc_ref, c_ref, first=False):
    keys = kv[0][0].shape[0]
    items = [(b, h) for b in range(len(kv)) for h in range(SB_HEADS)]
    sl = lambda h: slice((h // 2) * LANES, (h // 2 + 1) * LANES)

    def scores(b, h):
        return _dot_nt(qh_ref[b, h], kv[b][0][:, sl(h)])

    def suffix(z):
        sp = jnp.maximum(jnp.log2(1.0 + jnp.exp2(jnp.minimum(z, SOFTPLUS_CLAMP))), z)
        if mask is not None:
            sp = jnp.where(mask, sp, 0.0)
        return _dot(sp.astype(BF16), u)

    def weights(b, h, z, sfx):
        total = jnp.broadcast_to(sfx[:, 0:1], c_ref.shape[2:])
        if first:
            a = jnp.exp2(z + sfx)
        else:
            c = c_ref[b, h]
            a = jnp.exp2(z + sfx + jnp.concatenate([c] * (keys // LANES), axis=1))
        if mask is not None:
            a = jnp.where(mask, a, 0.0)
        av = _dot(a.astype(BF16), kv[b][1][:, sl(h)])
        if first:
            acc_ref[b, h], c_ref[b, h] = av, total
        else:
            acc_ref[b, h] += av
            c_ref[b, h] = c + total

    zs, sfxs = {}, {}
    for step in range(len(items) + 2 * SKEW):
        if step < len(items):
            zs[step] = scores(*items[step])
        if 0 <= step - SKEW < len(items):
            sfxs[step - SKEW] = suffix(zs[step - SKEW])
        if 0 <= step - 2 * SKEW < len(items):
            weights(*items[step - 2 * SKEW], zs.pop(step - 2 * SKEW), sfxs.pop(step - 2 * SKEW))


def _sb_prologue(q_ref, qh_ref):
    for b in range(q_ref.shape[0]):
        for p in range(SB_HEADS // 2):
            qh_ref[b, 2 * p], qh_ref[b, 2 * p + 1] = _split_pair(q_ref[b, :, p * LANES:(p + 1) * LANES])


def _sb_epilogue(acc_ref, g_ref, o_ref):
    for b in range(o_ref.shape[0]):
        for p in range(SB_HEADS // 2):
            sl = slice(p * LANES, (p + 1) * LANES)
            o = _merge_pair(acc_ref[b, 2 * p], acc_ref[b, 2 * p + 1])
            o_ref[b, :, sl] = (o * _silu(g_ref[b, :, sl].astype(F32))).astype(BF16)


def _sb_kernel(q_ref, k_ref, v_ref, km_ref, vm_ref, g_ref, u_ref, um_ref, o_ref, qh_ref, acc_ref, c_ref):
    i = pl.program_id(1)
    T = ATT_TILE
    nb = q_ref.shape[0]
    _sb_prologue(q_ref, qh_ref)

    def real_chunk(j, mask, first=False):
        start = pl.multiple_of(j * T, T)
        kv = [(k_ref[b, pl.ds(start, T), :], v_ref[b, pl.ds(start, T), :]) for b in range(nb)]
        _sb_chunk(qh_ref, kv, u_ref[...], mask, acc_ref, c_ref, first)

    real_chunk(i, _iota((T, T), 1) < _iota((T, T), 0), first=True)

    def alive():
        return jnp.max(c_ref[...]) > DEAD_CARRY

    def earlier(state):
        idx, _ = state
        real_chunk(i - 1 - idx, None)
        return idx + 1, alive()

    _, still_alive = lax.while_loop(lambda state: (state[0] < i) & state[1], earlier, (jnp.int32(0), alive()))

    @pl.when(still_alive)
    def _():
        _sb_chunk(qh_ref, [(km_ref[0], vm_ref[0])] * nb, um_ref[...], _iota((T, BLOCK), 1) >= N_PAD,
                  acc_ref, c_ref)

    _sb_epilogue(acc_ref, g_ref, o_ref)


def _sb_attn(q, k, v, k_meta, v_meta, gate, u, u_meta):
    B, L, _ = q.shape
    T, nb = ATT_TILE, ATT_BATCH
    tile = pl.BlockSpec((nb, T, SB_WIDTH), lambda b, i: (b, i, 0))
    seq = pl.BlockSpec((nb, L, SB_WIDTH), lambda b, i: (b, 0, 0))
    meta = pl.BlockSpec((1, BLOCK, SB_WIDTH), lambda b, i: (0, 0, 0))
    const = lambda a: pl.BlockSpec(a.shape, lambda b, i: (0, 0))
    return pl.pallas_call(
        _sb_kernel,
        grid=(B // nb, L // T),
        in_specs=[tile, seq, seq, meta, meta, tile, const(u), const(u_meta)],
        out_specs=tile,
        out_shape=jax.ShapeDtypeStruct((B, L, SB_WIDTH), BF16),
        scratch_shapes=[pltpu.VMEM((nb, SB_HEADS, T, LANES), BF16), pltpu.VMEM((nb, SB_HEADS, T, LANES), F32),
                        pltpu.VMEM((nb, SB_HEADS, T, LANES), F32)],
        compiler_params=_params(("parallel", "arbitrary")),
        name="sb_attn",
    )(q, k, v, k_meta, v_meta, gate, u, u_meta)


def _sb_meta_kernel(q_ref, k_ref, v_ref, g_ref, u_ref, o_ref, qh_ref, acc_ref, c_ref):
    _sb_prologue(q_ref, qh_ref)
    row, col = _iota((BLOCK, BLOCK), 0), _iota((BLOCK, BLOCK), 1)
    _sb_chunk(qh_ref, [(k_ref[0], v_ref[0])], u_ref[...], (col < row) & (col >= N_PAD), acc_ref, c_ref, first=True)
    _sb_epilogue(acc_ref, g_ref, o_ref)


def _sb_meta(q, k, v, gate, u_meta):
    blk = pl.BlockSpec((1, BLOCK, SB_WIDTH), lambda i: (0, 0, 0))
    return pl.pallas_call(
        _sb_meta_kernel,
        grid=(1,),
        in_specs=[blk, blk, blk, blk, pl.BlockSpec(u_meta.shape, lambda i: (0, 0))],
        out_specs=blk,
        out_shape=jax.ShapeDtypeStruct((1, BLOCK, SB_WIDTH), BF16),
        scratch_shapes=[pltpu.VMEM((1, SB_HEADS, BLOCK, LANES), BF16), pltpu.VMEM((1, SB_HEADS, BLOCK, LANES), F32),
                        pltpu.VMEM((1, SB_HEADS, BLOCK, LANES), F32)],
        compiler_params=_params(("arbitrary",)),
        name="sb_meta",
    )(q, k, v, gate, u_meta)


def _mla_chunk(q_ref, kv, mask, acc_ref, m_ref, first=False):
    keys = kv[0][0].shape[0]
    items = [(b, h) for b in range(len(kv)) for h in range(MLA_HEADS)]
    ones = jnp.ones((keys, LANES), BF16)

    def scores(b, h):
        hs = slice(h * LANES, (h + 1) * LANES)
        s = _dot_nt(q_ref[b, :, hs], kv[b][0][:, hs])
        return s if mask is None else jnp.where(mask, s, NEG)

    def probs(b, h, s):
        m_new = jnp.max(s, axis=-1, keepdims=True)
        if first:
            m_new = jnp.broadcast_to(m_new, m_ref.shape[2:])
            alpha = None
        else:
            m_old = m_ref[b, h]
            m_new = jnp.maximum(m_old, m_new)
            alpha = jnp.exp2(m_old - m_new)
        m_ref[b, h] = m_new
        p = jnp.exp2(s - jnp.concatenate([m_new] * (keys // LANES), axis=1))
        return p.astype(BF16), alpha

    def update(b, h, p, alpha):
        vs = slice((h // 2) * LANES, (h // 2 + 1) * LANES)
        pv = _dot(p, jnp.concatenate([kv[b][1][:, vs], ones], axis=1))
        acc_ref[b, h] = pv if first else jnp.concatenate([alpha, alpha], axis=1) * acc_ref[b, h] + pv

    ss, ps = {}, {}
    for step in range(len(items) + 2 * SKEW):
        if step < len(items):
            ss[step] = scores(*items[step])
        if 0 <= step - SKEW < len(items):
            ps[step - SKEW] = probs(*items[step - SKEW], ss.pop(step - SKEW))
        if 0 <= step - 2 * SKEW < len(items):
            update(*items[step - 2 * SKEW], *ps.pop(step - 2 * SKEW))


def _mla_finish(acc_ref, g_ref, o_ref):
    for b in range(o_ref.shape[0]):
        for p in range(MLA_HEADS // 2):
            sl = slice(p * LANES, (p + 1) * LANES)
            lo = acc_ref[b, 2 * p, :, :LANES] / acc_ref[b, 2 * p, :, LANES:]
            hi = acc_ref[b, 2 * p + 1, :, :LANES] / acc_ref[b, 2 * p + 1, :, LANES:]
            gate = g_ref[b, :, MLA_WIDTH + p * LANES:MLA_WIDTH + (p + 1) * LANES].astype(F32)
            o_ref[b, :, sl] = (_merge_pair(lo, hi) * _silu(gate)).astype(BF16)


def _mla_kernel(q_ref, k_ref, v_ref, km_ref, vm_ref, g_ref, o_ref, acc_ref, m_ref):
    i = pl.program_id(1)
    T = ATT_TILE
    nb = q_ref.shape[0]

    def real_kv(j):
        start = pl.multiple_of(j * T, T)
        return [(k_ref[b, pl.ds(start, T), :], v_ref[b, pl.ds(start, T), :]) for b in range(nb)]

    diag = [(jnp.concatenate([k, km_ref[0]], axis=0), jnp.concatenate([v, vm_ref[0]], axis=0)) for k, v in real_kv(i)]
    mask = jnp.concatenate([_iota((T, T), 1) <= _iota((T, T), 0), _iota((T, BLOCK), 1) >= N_PAD], axis=1)
    _mla_chunk(q_ref, diag, mask, acc_ref, m_ref, first=True)

    def earlier(idx, carry):
        _mla_chunk(q_ref, real_kv(idx), None, acc_ref, m_ref)
        return carry

    lax.fori_loop(0, i, earlier, 0)
    _mla_finish(acc_ref, g_ref, o_ref)


def _mla_attn(q, k, v, k_meta, v_meta, gate):
    B, L, _ = q.shape
    T, nb = ATT_TILE, ATT_BATCH
    qk_w = MLA_HEADS * LANES
    return pl.pallas_call(
        _mla_kernel,
        grid=(B // nb, L // T),
        in_specs=[pl.BlockSpec((nb, T, qk_w), lambda b, i: (b, i, 0)),
                  pl.BlockSpec((nb, L, qk_w), lambda b, i: (b, 0, 0)),
                  pl.BlockSpec((nb, L, MLA_WIDTH), lambda b, i: (b, 0, 0)),
                  pl.BlockSpec((1, BLOCK, qk_w), lambda b, i: (0, 0, 0)),
                  pl.BlockSpec((1, BLOCK, MLA_WIDTH), lambda b, i: (0, 0, 0)),
                  pl.BlockSpec((nb, T, 2 * MLA_WIDTH), lambda b, i: (b, i, 0))],
        out_specs=pl.BlockSpec((nb, T, MLA_WIDTH), lambda b, i: (b, i, 0)),
        out_shape=jax.ShapeDtypeStruct((B, L, MLA_WIDTH), BF16),
        scratch_shapes=[pltpu.VMEM((nb, MLA_HEADS, T, 2 * LANES), F32), pltpu.VMEM((nb, MLA_HEADS, T, LANES), F32)],
        compiler_params=_params(("parallel", "arbitrary")),
        name="mla_attn",
    )(q, k, v, k_meta, v_meta, gate)


def _mla_meta_kernel(q_ref, k_ref, v_ref, g_ref, o_ref, acc_ref, m_ref):
    row, col = _iota((BLOCK, BLOCK), 0), _iota((BLOCK, BLOCK), 1)
    _mla_chunk(q_ref, [(k_ref[0], v_ref[0])], (col <= row) & (col >= N_PAD), acc_ref, m_ref, first=True)
    _mla_finish(acc_ref, g_ref, o_ref)


def _mla_meta(q, k, v, gate):
    qk_w = MLA_HEADS * LANES
    blk = lambda w: pl.BlockSpec((1, BLOCK, w), lambda i: (0, 0, 0))
    return pl.pallas_call(
        _mla_meta_kernel,
        grid=(1,),
        in_specs=[blk(qk_w), blk(qk_w), blk(MLA_WIDTH), blk(2 * MLA_WIDTH)],
        out_specs=blk(MLA_WIDTH),
        out_shape=jax.ShapeDtypeStruct((1, BLOCK, MLA_WIDTH), BF16),
        scratch_shapes=[pltpu.VMEM((1, MLA_HEADS, BLOCK, 2 * LANES), F32),
                        pltpu.VMEM((1, MLA_HEADS, BLOCK, LANES), F32)],
        compiler_params=_params(("arbitrary",)),
        name="mla_meta",
    )(q, k, v, gate)


def _mid_kernel(osb_ref, omla_ref, h_ref, wo_ref, g_ref, w_ref, h1_ref, q_ref, kv_ref, gate_ref):
    rows = h_ref.shape[1]
    halves = [slice(0, rows // 2), slice(rows // 2, rows)] if rows >= 2 * BLOCK else [slice(0, rows)]
    ys = [_dot(osb_ref[0, r, :], wo_ref[0:512, :]) + _dot(omla_ref[0, r, :], wo_ref[512:1024, :]) for r in halves]
    for r, y in zip(halves, ys):
        h1 = h_ref[0, r, :] + y
        h1_ref[0, r, :] = h1
        xn = _rms(h1, g_ref[...]).astype(BF16)
        q_ref[0, r, :] = (_dot(xn, w_ref[:, 0:1024]) * LOG2E).astype(BF16)
        kv_ref[0, r, :] = _dot(xn, w_ref[:, 1024:1280]).astype(BF16)
        gate_ref[0, r, :] = _dot(xn, w_ref[:, 1280:2304]).astype(BF16)


def _mid(osb, omla, h, tile, wo, g, w):
    B, L, _ = h.shape
    row = lambda width: pl.BlockSpec((1, tile, width), lambda b, t: (b, t, 0))
    full = lambda a: pl.BlockSpec(a.shape, lambda b, t: (0,) * a.ndim)
    return pl.pallas_call(
        _mid_kernel,
        grid=(B, L // tile),
        in_specs=[row(512), row(512), row(D_MODEL), full(wo), full(g), full(w)],
        out_specs=[row(D_MODEL), row(1024), row(256), row(1024)],
        out_shape=[jax.ShapeDtypeStruct((B, L, D_MODEL), F32),
                   jax.ShapeDtypeStruct((B, L, 1024), BF16),
                   jax.ShapeDtypeStruct((B, L, 256), BF16),
                   jax.ShapeDtypeStruct((B, L, 1024), BF16)],
        compiler_params=_params(("parallel", "parallel")),
        name="mid",
    )(osb, omla, h, wo, g, w)


def _swa_kernel(sink_ref, q_ref, kvp_ref, kvc_ref, kvm_ref, g_ref, bb_ref, mb_ref, h_ref, wo_ref, fg_ref,
                out_ref, o_scr):
    n = pl.program_id(1)
    first = jnp.where(n > 0, 0, 1)
    in_cur = _iota((BLOCK, BLOCK), 1) <= _iota((BLOCK, BLOCK), 0)
    blocks_before = (n + 1).astype(F32) * float(BLOCK)
    pairs = SWA_HEADS // 2
    items = [(b, g) for b in range(q_ref.shape[0]) for g in range(pairs // SWA_STACK)]
    k_sl, v_sl = slice(0, LANES), slice(LANES, 2 * LANES)
    heads_of = lambda g: [(p, hh) for p in range(g * SWA_STACK, (g + 1) * SWA_STACK) for hh in range(2)]

    def scores(b, g):
        q_stack = jnp.concatenate(
            [_split_pair(q_ref[b, :, p * LANES:(p + 1) * LANES])[hh] for p, hh in heads_of(g)], axis=0)
        k_band = jnp.concatenate([kvp_ref[b, :, k_sl], kvc_ref[b, :, k_sl]], axis=0)
        return _dot_nt(q_stack, k_band), _dot_nt(q_stack, kvm_ref[0, :, k_sl])

    def probs(g, z_stack, zm_stack):
        p_rows, invs = [], []
        for j, (p, hh) in enumerate(heads_of(g)):
            h = p + pairs * hh
            rows = slice(j * BLOCK, (j + 1) * BLOCK)
            z, z_meta = z_stack[rows], zm_stack[rows]
            slope = 2.0 ** (-8.0 * (h + 1.0) / SWA_HEADS) * LOG2E
            s_band = jnp.where(in_cur, z[:, BLOCK:], z[:, :BLOCK]) - bb_ref[first, h]
            s_meta = z_meta - mb_ref[h] - slope * blocks_before
            sink = sink_ref[h] * LOG2E
            m = jnp.maximum(jnp.max(jnp.maximum(s_band, s_meta), axis=-1, keepdims=True), sink)
            p_band, p_meta = jnp.exp2(s_band - m), jnp.exp2(s_meta - m)
            denom = jnp.sum(p_band + p_meta, axis=-1, keepdims=True) + jnp.exp2(sink - m)
            p_rows.append(jnp.concatenate(
                [jnp.where(in_cur, 0.0, p_band), jnp.where(in_cur, p_band, 0.0), p_meta], axis=1).astype(BF16))
            invs.append(1.0 / denom)
        return jnp.concatenate(p_rows, axis=0), invs

    def values(b, g, p_stack, invs):
        v_all = jnp.concatenate([kvp_ref[b, :, v_sl], kvc_ref[b, :, v_sl], kvm_ref[0, :, v_sl]], axis=0)
        o_stack = _dot(p_stack, v_all)
        for j, (p, hh) in enumerate(heads_of(g)):
            if hh == 1:
                sl = slice(p * LANES, (p + 1) * LANES)
                o = _merge_pair(o_stack[(j - 1) * BLOCK:j * BLOCK] * invs[j - 1],
                                o_stack[j * BLOCK:(j + 1) * BLOCK] * invs[j])
                o_scr[b, :, sl] = (o * _silu(g_ref[b, :, sl].astype(F32))).astype(BF16)

    zs, ps = {}, {}
    for step in range(len(items) + 2 * SKEW):
        if step < len(items):
            zs[step] = scores(*items[step])
        if 0 <= step - SKEW < len(items):
            ps[step - SKEW] = probs(items[step - SKEW][1], *zs.pop(step - SKEW))
        t = step - 2 * SKEW
        if 0 <= t < len(items):
            b, g = items[t]
            values(b, g, *ps.pop(t))
            if g == pairs // SWA_STACK - 1:
                out_ref[b] = _rms(h_ref[b] + _dot(o_scr[b], wo_ref[...]), fg_ref[...])


def _swa_bias_tables():
    row, col = _iota((BLOCK, BLOCK), 0), _iota((BLOCK, BLOCK), 1)
    slopes = (2.0 ** (-8.0 * (jnp.arange(SWA_HEADS, dtype=F32) + 1.0) / SWA_HEADS) * LOG2E)[:, None, None]
    band = slopes * jnp.where(col <= row, row - col, row - col + BLOCK).astype(F32)
    band_first = band + jnp.where(col <= row, 0.0, -NEG)
    meta = jnp.where(col >= N_PAD, 0.0, -NEG) + slopes * (row - col).astype(F32)
    return jnp.stack([band, band_first]), meta


def _swa_attn(sinks, q, kv, kv_meta, gate, h1, wo, final_g):
    B, L, _ = q.shape
    kvw = kv.shape[-1]
    nb = SWA_BATCH
    band_bias, meta_bias = _swa_bias_tables()
    row = lambda w: pl.BlockSpec((nb, BLOCK, w), lambda b, n, s: (b, n, 0))
    const = lambda a: pl.BlockSpec(a.shape, lambda b, n, s: (0,) * a.ndim)
    grid_spec = pltpu.PrefetchScalarGridSpec(
        num_scalar_prefetch=1,
        grid=(B // nb, L // BLOCK),
        in_specs=[row(SWA_WIDTH),
                  pl.BlockSpec((nb, BLOCK, kvw), lambda b, n, s: (b, jnp.maximum(n - 1, 0), 0)),
                  row(kvw), const(kv_meta), row(SWA_WIDTH), const(band_bias), const(meta_bias),
                  row(D_MODEL), const(wo), const(final_g)],
        out_specs=row(D_MODEL),
        scratch_shapes=[pltpu.VMEM((nb, BLOCK, SWA_WIDTH), BF16)],
    )
    return pl.pallas_call(
        _swa_kernel,
        grid_spec=grid_spec,
        out_shape=jax.ShapeDtypeStruct((B, L, D_MODEL), F32),
        compiler_params=_params(("parallel", "parallel")),
        name="swa_attn",
    )(sinks, q, kv, kv, kv_meta, gate, band_bias, meta_bias, h1, wo, final_g)


def _layer0_weights(w_in, w_uq, w_ukv):
    q, k, v, g_sb = (w_in[:, i * 512:(i + 1) * 512] for i in range(4))
    c_q, c_kv = w_in[:, 2048:2304], w_in[:, 2304:2432]
    k_r, g_mla = w_in[:, 2432:2464], w_in[:, 2464:2976]
    half = MLA_ROPE // 2
    r1, r2 = k_r[:, :half], k_r[:, half:]
    z = lambda n: jnp.zeros((D_MODEL, n), w_in.dtype)
    kr_blk = jnp.concatenate([z(MLA_NOPE), r1, r2, z(LANES - MLA_NOPE - MLA_ROPE)], axis=1)
    w0 = jnp.concatenate([q * (SB_DIM ** -0.5), k, v, g_sb, g_mla, c_q, c_kv, kr_blk], axis=1)

    uq = w_uq.reshape(MLA_Q_LORA, MLA_HEADS, MLA_NOPE + MLA_ROPE)
    nope, u1, u2 = uq[..., :MLA_NOPE], uq[..., MLA_NOPE:MLA_NOPE + half], uq[..., MLA_NOPE + half:]
    zq = lambda n: jnp.zeros((MLA_Q_LORA, MLA_HEADS, n), w_uq.dtype)
    uq_main = jnp.concatenate([nope, u1, u2, zq(LANES - MLA_NOPE - MLA_ROPE)], axis=-1)
    uq_rot = jnp.concatenate([zq(MLA_NOPE), -u2, u1, zq(LANES - MLA_NOPE - MLA_ROPE)], axis=-1)
    wuq = jnp.concatenate([uq_main.reshape(MLA_Q_LORA, -1), uq_rot.reshape(MLA_Q_LORA, -1)], axis=1)

    ukv = w_ukv.reshape(MLA_KV_LORA, MLA_HEADS, MLA_NOPE + MLA_V)
    k_nope = jnp.concatenate([ukv[..., :MLA_NOPE],
                              jnp.zeros((MLA_KV_LORA, MLA_HEADS, LANES - MLA_NOPE), w_ukv.dtype)], axis=-1)
    wukv = jnp.concatenate([k_nope.reshape(MLA_KV_LORA, -1),
                            ukv[..., MLA_NOPE:].reshape(MLA_KV_LORA, -1)], axis=1)
    return w0.astype(BF16), wuq.astype(BF16), wukv.astype(BF16)


def _pair_heads(w, axis):
    shape = w.shape
    w = w.reshape(shape[:axis] + (SWA_KV_HEADS, SWA_HEADS // SWA_KV_HEADS, SWA_DIM) + shape[axis + 1:])
    return jnp.swapaxes(w, axis, axis + 1).reshape(shape)


def _layer1_weights(w_in, w_out):
    q, kv, g = w_in[:, :1024], w_in[:, 1024:1280], w_in[:, 1280:2304]
    w1 = jnp.concatenate([_pair_heads(q * (SWA_DIM ** -0.5), 1), kv, _pair_heads(g, 1)], axis=1)
    return w1.astype(BF16), _pair_heads(w_out, 0).astype(BF16)


def _rope_tables():
    half = MLA_ROPE // 2
    pos = jnp.arange(N_META + SEQ).astype(F32)
    inv = ROPE_BASE ** (-jnp.arange(half, dtype=F32) / half)
    ang = pos[:, None] * inv[None, :]
    cos, sin = jnp.cos(ang), jnp.sin(ang)
    n = pos.shape[0]
    z = lambda w: jnp.zeros((n, w), F32)
    tail = LANES - MLA_NOPE - MLA_ROPE
    c = jnp.concatenate([jnp.ones((n, MLA_NOPE), F32), cos, cos, z(tail)], axis=1)
    s = jnp.concatenate([z(MLA_NOPE), sin, sin, z(tail)], axis=1)
    s1 = jnp.concatenate([z(MLA_NOPE), -sin, z(half), z(tail)], axis=1)
    s2 = jnp.concatenate([z(MLA_NOPE), z(half), sin, z(tail)], axis=1)
    scale = (MLA_NOPE + MLA_ROPE) ** -0.5 * LOG2E
    tabs = (c * scale, s * scale, c, s1, s2)
    pad = lambda t: jnp.concatenate([jnp.zeros((N_PAD, LANES), F32), t[:N_META]], axis=0)
    return tuple(pad(t) for t in tabs), tuple(t[N_META:] for t in tabs)


def _suffix_matrix(n):
    return jnp.where(_iota((n, n), 0) >= _iota((n, n), 1), -1.0, 0.0).astype(BF16)


def kernel(x, meta, norm_g, final_g, ev_w_in, ev_q_norm_g, ev_kv_norm_g, ev_w_uq, ev_w_ukv,
           ev_w_out, od_w_in, od_sinks, od_w_out):
    w0, wuq, wukv = _layer0_weights(ev_w_in[0], ev_w_uq[0], ev_w_ukv[0])
    w1, wo1 = _layer1_weights(od_w_in[0], od_w_out[0])
    wo0 = ev_w_out[0].astype(BF16)
    tabs_meta, tabs_real = _rope_tables()
    u, u_meta = _suffix_matrix(ATT_TILE), _suffix_matrix(BLOCK)
    l0 = (norm_g[0:1], w0, ev_q_norm_g[0:1], ev_kv_norm_g[0:1], wuq, wukv)

    hm = jnp.concatenate([jnp.zeros((N_PAD, D_MODEL), x.dtype), meta.astype(x.dtype)], axis=0)[None]
    qsb_m, ksb_m, vsb_m, gate_m, qm_m, km_m, vm_m = _proj0(hm, BLOCK, *l0, tabs_meta)
    osb_m = _sb_meta(qsb_m, ksb_m, vsb_m, gate_m, u_meta)
    omla_m = _mla_meta(qm_m, km_m, vm_m, gate_m)
    _, _, kv1_m, _ = _mid(osb_m, omla_m, hm, BLOCK, wo0, norm_g[1:2], w1)

    q_sb, k_sb, v_sb, gate0, q_mla, k_mla, v_mla = _proj0(x, ROW_TILE, *l0, tabs_real)
    o_sb = _sb_attn(q_sb, k_sb, v_sb, ksb_m, vsb_m, gate0, u, u_meta)
    o_mla = _mla_attn(q_mla, k_mla, v_mla, km_m, vm_m, gate0)
    h1, q1, kv1, gate1 = _mid(o_sb, o_mla, x, ROW_TILE, wo0, norm_g[1:2], w1)
    return _swa_attn(od_sinks[0], q1, kv1, kv1_m, gate1, h1, wo1, final_g[None, :])
```

```python
import math

import jax
import jax.numpy as jnp
from jax import lax
from jax.experimental import pallas as pl
from jax.experimental.pallas import tpu as pltpu

D_MODEL = 1024
SEQ = 2048
N_META = 16
BLOCK = 128
N_PAD = BLOCK - N_META
NORM_EPS = 1e-6
NEG = -1e30

SB_HEADS = 8
SB_DIM = 64
SB_WIDTH = SB_HEADS * SB_DIM
MLA_HEADS = 8
MLA_Q_LORA = 256
MLA_KV_LORA = 128
MLA_NOPE = 64
MLA_ROPE = 32
MLA_V = 64
MLA_WIDTH = MLA_HEADS * MLA_V
ROPE_BASE = 10000.0
SWA_HEADS = 16
SWA_KV_HEADS = 2
SWA_DIM = 64
SWA_WIDTH = SWA_HEADS * SWA_DIM

LANES = 128
ROW_TILE = 1024
ATT_TILE = 256
ATT_BATCH = 2
VMEM_LIMIT = 48 * 1024 * 1024
LOG2E = math.log2(math.e)
SWA_BATCH = 4
SWA_STACK = 2
SKEW = 1
DEAD_CARRY = -256.0
SOFTPLUS_CLAMP = 64.0

BF16 = jnp.bfloat16
F32 = jnp.float32


def _dot(a, b):
    return jnp.dot(a, b, preferred_element_type=F32)


def _dot_nt(a, b):
    return lax.dot_general(a, b, (((1,), (1,)), ((), ())), preferred_element_type=F32)


def _rms(x, g):
    ms = jnp.mean(x * x, axis=-1, keepdims=True)
    return x * lax.rsqrt(ms + NORM_EPS) * g


def _silu(g):
    return g * (1.0 / (1.0 + jnp.exp(-g)))


def _params(sem):
    return pltpu.CompilerParams(dimension_semantics=sem, vmem_limit_bytes=VMEM_LIMIT)


def _iota(shape, dim):
    return lax.broadcasted_iota(jnp.int32, shape, dim)


def _split_pair(x):
    lane = _iota(x.shape, 1)
    zero = jnp.zeros_like(x)
    return jnp.where(lane < LANES // 2, x, zero), jnp.where(lane >= LANES // 2, x, zero)


def _merge_pair(lo, hi):
    lane = _iota(lo.shape, 1)
    return jnp.where(lane < LANES // 2, lo, hi)


def _proj0_kernel(x_ref, g_ref, w_ref, qg_ref, kvg_ref, wuq_ref, wukv_ref,
                  cq_ref, sq_ref, ck_ref, sk1_ref, sk2_ref,
                  qsb_ref, ksb_ref, vsb_ref, gate_ref, qm_ref, km_ref, vm_ref):
    rows = x_ref.shape[1]
    halves = [slice(0, rows // 2), slice(rows // 2, rows)] if rows >= 2 * BLOCK else [slice(0, rows)]
    qk_w = MLA_HEADS * LANES
    for r in halves:
        xn = _rms(x_ref[0, r, :], g_ref[...]).astype(BF16)
        qsb_ref[0, r, :] = (_dot(xn, w_ref[:, 0:512]) * LOG2E).astype(BF16)
        ksb_ref[0, r, :] = _dot(xn, w_ref[:, 512:1024]).astype(BF16)
        vsb_ref[0, r, :] = _dot(xn, w_ref[:, 1024:1536]).astype(BF16)
        gate_ref[0, r, :] = _dot(xn, w_ref[:, 1536:2560]).astype(BF16)
        lat = _dot(xn, w_ref[:, 2560:3072])
        cqn = _rms(lat[:, 0:256], qg_ref[...]).astype(BF16)
        ckvn = _rms(lat[:, 256:384], kvg_ref[...]).astype(BF16)
        kr = lat[:, 384:512]
        k_rope = (kr * ck_ref[r, :] + pltpu.roll(kr, LANES - MLA_ROPE // 2, 1) * sk1_ref[r, :]
                  + pltpu.roll(kr, MLA_ROPE // 2, 1) * sk2_ref[r, :])
        q_all = _dot(cqn, wuq_ref[...])
        kv_all = _dot(ckvn, wukv_ref[...])
        cq = cq_ref[r, :]
        sq = sq_ref[r, :]
        for h in range(MLA_HEADS):
            lo, hi = h * LANES, (h + 1) * LANES
            qm_ref[0, r, lo:hi] = (q_all[:, lo:hi] * cq + q_all[:, qk_w + lo:qk_w + hi] * sq).astype(BF16)
            km_ref[0, r, lo:hi] = (kv_all[:, lo:hi] + k_rope).astype(BF16)
        vm_ref[0, r, :] = kv_all[:, qk_w:].astype(BF16)


def _proj0(h, tile, g, w, qg, kvg, wuq, wukv, tables):
    B, L, _ = h.shape
    row = lambda width: pl.BlockSpec((1, tile, width), lambda b, t: (b, t, 0))
    full = lambda a: pl.BlockSpec(a.shape, lambda b, t: (0,) * a.ndim)
    tab = pl.BlockSpec((tile, LANES), lambda b, t: (t, 0))
    out = lambda width: jax.ShapeDtypeStruct((B, L, width), BF16)
    return pl.pallas_call(
        _proj0_kernel,
        grid=(B, L // tile),
        in_specs=[row(D_MODEL), full(g), full(w), full(qg), full(kvg), full(wuq), full(wukv),
                  tab, tab, tab, tab, tab],
        out_specs=[row(512), row(512), row(512), row(1024), row(1024), row(1024), row(512)],
        out_shape=[out(512), out(512), out(512), out(1024), out(1024), out(1024), out(512)],
        compiler_params=_params(("parallel", "parallel")),
        name="proj0",
    )(h, g, w, qg, kvg, wuq, wukv, *tables)


def _sb_chunk(qh_ref, kv, u, mask, acc_ref, c_ref, first=False):
    keys = kv[0][0].shape[0]
    items = [(b, h) for b in range(len(kv)) for h in range(SB_HEADS)]
    sl = lambda h: slice((h // 2) * LANES, (h // 2 + 1) * LANES)

    def scores(b, h):
        return _dot_nt(qh_ref[b, h], kv[b][0][:, sl(h)])

    def suffix(z):
        sp = jnp.maximum(jnp.log2(1.0 + jnp.exp2(jnp.minimum(z, SOFTPLUS_CLAMP))), z)
        if mask is not None:
            sp = jnp.where(mask, sp, 0.0)
        return _dot(sp.astype(BF16), u)

    def weights(b, h, z, sfx):
        total = jnp.broadcast_to(sfx[:, 0:1], c_ref.shape[2:])
        if first:
            a = jnp.exp2(z + sfx)
        else:
            c = c_ref[b, h]
            a = jnp.exp2(z + sfx + jnp.concatenate([c] * (keys // LANES), axis=1))
        if mask is not None:
            a = jnp.where(mask, a, 0.0)
        av = _dot(a.astype(BF16), kv[b][1][:, sl(h)])
        if first:
            acc_ref[b, h], c_ref[b, h] = av, total
        else:
            acc_ref[b, h] += av
            c_ref[b, h] = c + total

    zs, sfxs = {}, {}
    for step in range(len(items) + 2 * SKEW):
        if step < len(items):
            zs[step] = scores(*items[step])
        if 0 <= step - SKEW < len(items):
            sfxs[step - SKEW] = suffix(zs[step - SKEW])
        if 0 <= step - 2 * SKEW < len(items):
            weights(*items[step - 2 * SKEW], zs.pop(step - 2 * SKEW), sfxs.pop(step - 2 * SKEW))


def _sb_prologue(q_ref, qh_ref):
    for b in range(q_ref.shape[0]):
        for p in range(SB_HEADS // 2):
            qh_ref[b, 2 * p], qh_ref[b, 2 * p + 1] = _split_pair(q_ref[b, :, p * LANES:(p + 1) * LANES])


def _sb_epilogue(acc_ref, g_ref, o_ref):
    for b in range(o_ref.shape[0]):
        for p in range(SB_HEADS // 2):
            sl = slice(p * LANES, (p + 1) * LANES)
            o = _merge_pair(acc_ref[b, 2 * p], acc_ref[b, 2 * p + 1])
            o_ref[b, :, sl] = (o * _silu(g_ref[b, :, sl].astype(F32))).astype(BF16)


def _sb_kernel(q_ref, k_ref, v_ref, km_ref, vm_ref, g_ref, u_ref, um_ref, o_ref, qh_ref, acc_ref, c_ref):
    i = pl.program_id(1)
    T = ATT_TILE
    nb = q_ref.shape[0]
    _sb_prologue(q_ref, qh_ref)

    def real_chunk(j, mask, first=False):
        start = pl.multiple_of(j * T, T)
        kv = [(k_ref[b, pl.ds(start, T), :], v_ref[b, pl.ds(start, T), :]) for b in range(nb)]
        _sb_chunk(qh_ref, kv, u_ref[...], mask, acc_ref, c_ref, first)

    real_chunk(i, _iota((T, T), 1) < _iota((T, T), 0), first=True)

    def alive():
        return jnp.max(c_ref[...]) > DEAD_CARRY

    def earlier(state):
        idx, _ = state
        real_chunk(i - 1 - idx, None)
        return idx + 1, alive()

    _, still_alive = lax.while_loop(lambda state: (state[0] < i) & state[1], earlier, (jnp.int32(0), alive()))

    @pl.when(still_alive)
    def _():
        _sb_chunk(qh_ref, [(km_ref[0], vm_ref[0])] * nb, um_ref[...], _iota((T, BLOCK), 1) >= N_PAD,
                  acc_ref, c_ref)

    _sb_epilogue(acc_ref, g_ref, o_ref)


def _sb_attn(q, k, v, k_meta, v_meta, gate, u, u_meta):
    B, L, _ = q.shape
    T, nb = ATT_TILE, ATT_BATCH
    tile = pl.BlockSpec((nb, T, SB_WIDTH), lambda b, i: (b, i, 0))
    seq = pl.BlockSpec((nb, L, SB_WIDTH), lambda b, i: (b, 0, 0))
    meta = pl.BlockSpec((1, BLOCK, SB_WIDTH), lambda b, i: (0, 0, 0))
    const = lambda a: pl.BlockSpec(a.shape, lambda b, i: (0, 0))
    return pl.pallas_call(
        _sb_kernel,
        grid=(B // nb, L // T),
        in_specs=[tile, seq, seq, meta, meta, tile, const(u), const(u_meta)],
        out_specs=tile,
        out_shape=jax.ShapeDtypeStruct((B, L, SB_WIDTH), BF16),
        scratch_shapes=[pltpu.VMEM((nb, SB_HEADS, T, LANES), BF16), pltpu.VMEM((nb, SB_HEADS, T, LANES), F32),
                        pltpu.VMEM((nb, SB_HEADS, T, LANES), F32)],
        compiler_params=_params(("parallel", "arbitrary")),
        name="sb_attn",
    )(q, k, v, k_meta, v_meta, gate, u, u_meta)


def _sb_meta_kernel(q_ref, k_ref, v_ref, g_ref, u_ref, o_ref, qh_ref, acc_ref, c_ref):
    _sb_prologue(q_ref, qh_ref)
    row, col = _iota((BLOCK, BLOCK), 0), _iota((BLOCK, BLOCK), 1)
    _sb_chunk(qh_ref, [(k_ref[0], v_ref[0])], u_ref[...], (col < row) & (col >= N_PAD), acc_ref, c_ref, first=True)
    _sb_epilogue(acc_ref, g_ref, o_ref)


def _sb_meta(q, k, v, gate, u_meta):
    blk = pl.BlockSpec((1, BLOCK, SB_WIDTH), lambda i: (0, 0, 0))
    return pl.pallas_call(
        _sb_meta_kernel,
        grid=(1,),
        in_specs=[blk, blk, blk, blk, pl.BlockSpec(u_meta.shape, lambda i: (0, 0))],
        out_specs=blk,
        out_shape=jax.ShapeDtypeStruct((1, BLOCK, SB_WIDTH), BF16),
        scratch_shapes=[pltpu.VMEM((1, SB_HEADS, BLOCK, LANES), BF16), pltpu.VMEM((1, SB_HEADS, BLOCK, LANES), F32),
                        pltpu.VMEM((1, SB_HEADS, BLOCK, LANES), F32)],
        compiler_params=_params(("arbitrary",)),
        name="sb_meta",
    )(q, k, v, gate, u_meta)


def _mla_chunk(q_ref, kv, mask, acc_ref, m_ref, first=False):
    keys = kv[0][0].shape[0]
    items = [(b, h) for b in range(len(kv)) for h in range(MLA_HEADS)]
    ones = jnp.ones((keys, LANES), BF16)

    def scores(b, h):
        hs = slice(h * LANES, (h + 1) * LANES)
        s = _dot_nt(q_ref[b, :, hs], kv[b][0][:, hs])
        return s if mask is None else jnp.where(mask, s, NEG)

    def probs(b, h, s):
        m_new = jnp.max(s, axis=-1, keepdims=True)
        if first:
            m_new = jnp.broadcast_to(m_new, m_ref.shape[2:])
            alpha = None
        else:
            m_old = m_ref[b, h]
            m_new = jnp.maximum(m_old, m_new)
            alpha = jnp.exp2(m_old - m_new)
        m_ref[b, h] = m_new
        p = jnp.exp2(s - jnp.concatenate([m_new] * (keys // LANES), axis=1))
        return p.astype(BF16), alpha

    def update(b, h, p, alpha):
        vs = slice((h // 2) * LANES, (h // 2 + 1) * LANES)
        pv = _dot(p, jnp.concatenate([kv[b][1][:, vs], ones], axis=1))
        acc_ref[b, h] = pv if first else jnp.concatenate([alpha, alpha], axis=1) * acc_ref[b, h] + pv

    ss, ps = {}, {}
    for step in range(len(items) + 2 * SKEW):
        if step < len(items):
            ss[step] = scores(*items[step])
        if 0 <= step - SKEW < len(items):
            ps[step - SKEW] = probs(*items[step - SKEW], ss.pop(step - SKEW))
        if 0 <= step - 2 * SKEW < len(items):
            update(*items[step - 2 * SKEW], *ps.pop(step - 2 * SKEW))


def _mla_finish(acc_ref, g_ref, o_ref):
    for b in range(o_ref.shape[0]):
        for p in range(MLA_HEADS // 2):
            sl = slice(p * LANES, (p + 1) * LANES)
            lo = acc_ref[b, 2 * p, :, :LANES] / acc_ref[b, 2 * p, :, LANES:]
            hi = acc_ref[b, 2 * p + 1, :, :LANES] / acc_ref[b, 2 * p + 1, :, LANES:]
            gate = g_ref[b, :, MLA_WIDTH + p * LANES:MLA_WIDTH + (p + 1) * LANES].astype(F32)
            o_ref[b, :, sl] = (_merge_pair(lo, hi) * _silu(gate)).astype(BF16)


def _mla_kernel(q_ref, k_ref, v_ref, km_ref, vm_ref, g_ref, o_ref, acc_ref, m_ref):
    i = pl.program_id(1)
    T = ATT_TILE
    nb = q_ref.shape[0]

    def real_kv(j):
        start = pl.multiple_of(j * T, T)
        return [(k_ref[b, pl.ds(start, T), :], v_ref[b, pl.ds(start, T), :]) for b in range(nb)]

    def first_chunk(extra):
        parts = [real_kv(i)] + ([real_kv(0)] if extra else []) + [[(km_ref[0], vm_ref[0])] * nb]
        kv = [tuple(jnp.concatenate([part[b][j] for part in parts], axis=0) for j in range(2)) for b in range(nb)]
        masks = ([_iota((T, T), 1) <= _iota((T, T), 0)] + ([_iota((T, T), 1) >= 0] if extra else [])
                 + [_iota((T, BLOCK), 1) >= N_PAD])
        _mla_chunk(q_ref, kv, jnp.concatenate(masks, axis=1), acc_ref, m_ref, first=True)

    odd = i % 2

    @pl.when(odd == 1)
    def _():
        first_chunk(True)

    @pl.when(odd == 0)
    def _():
        first_chunk(False)

    def earlier(idx, carry):
        start = pl.multiple_of((odd + 2 * idx) * T, T)
        kv = [(k_ref[b, pl.ds(start, 2 * T), :], v_ref[b, pl.ds(start, 2 * T), :]) for b in range(nb)]
        _mla_chunk(q_ref, kv, None, acc_ref, m_ref)
        return carry

    lax.fori_loop(0, i // 2, earlier, 0)
    _mla_finish(acc_ref, g_ref, o_ref)


def _mla_attn(q, k, v, k_meta, v_meta, gate):
    B, L, _ = q.shape
    T, nb = ATT_TILE, ATT_BATCH
    qk_w = MLA_HEADS * LANES
    return pl.pallas_call(
        _mla_kernel,
        grid=(B // nb, L // T),
        in_specs=[pl.BlockSpec((nb, T, qk_w), lambda b, i: (b, i, 0)),
                  pl.BlockSpec((nb, L, qk_w), lambda b, i: (b, 0, 0)),
                  pl.BlockSpec((nb, L, MLA_WIDTH), lambda b, i: (b, 0, 0)),
                  pl.BlockSpec((1, BLOCK, qk_w), lambda b, i: (0, 0, 0)),
                  pl.BlockSpec((1, BLOCK, MLA_WIDTH), lambda b, i: (0, 0, 0)),
                  pl.BlockSpec((nb, T, 2 * MLA_WIDTH), lambda b, i: (b, i, 0))],
        out_specs=pl.BlockSpec((nb, T, MLA_WIDTH), lambda b, i: (b, i, 0)),
        out_shape=jax.ShapeDtypeStruct((B, L, MLA_WIDTH), BF16),
        scratch_shapes=[pltpu.VMEM((nb, MLA_HEADS, T, 2 * LANES), F32), pltpu.VMEM((nb, MLA_HEADS, T, LANES), F32)],
        compiler_params=_params(("parallel", "arbitrary")),
        name="mla_attn",
    )(q, k, v, k_meta, v_meta, gate)


def _mla_meta_kernel(q_ref, k_ref, v_ref, g_ref, o_ref, acc_ref, m_ref):
    row, col = _iota((BLOCK, BLOCK), 0), _iota((BLOCK, BLOCK), 1)
    _mla_chunk(q_ref, [(k_ref[0], v_ref[0])], (col <= row) & (col >= N_PAD), acc_ref, m_ref, first=True)
    _mla_finish(acc_ref, g_ref, o_ref)


def _mla_meta(q, k, v, gate):
    qk_w = MLA_HEADS * LANES
    blk = lambda w: pl.BlockSpec((1, BLOCK, w), lambda i: (0, 0, 0))
    return pl.pallas_call(
        _mla_meta_kernel,
        grid=(1,),
        in_specs=[blk(qk_w), blk(qk_w), blk(MLA_WIDTH), blk(2 * MLA_WIDTH)],
        out_specs=blk(MLA_WIDTH),
        out_shape=jax.ShapeDtypeStruct((1, BLOCK, MLA_WIDTH), BF16),
        scratch_shapes=[pltpu.VMEM((1, MLA_HEADS, BLOCK, 2 * LANES), F32),
                        pltpu.VMEM((1, MLA_HEADS, BLOCK, LANES), F32)],
        compiler_params=_params(("arbitrary",)),
        name="mla_meta",
    )(q, k, v, gate)


def _mid_kernel(osb_ref, omla_ref, h_ref, wo_ref, g_ref, w_ref, h1_ref, q_ref, kv_ref, gate_ref):
    rows = h_ref.shape[1]
    halves = [slice(0, rows // 2), slice(rows // 2, rows)] if rows >= 2 * BLOCK else [slice(0, rows)]
    ys = [_dot(osb_ref[0, r, :], wo_ref[0:512, :]) + _dot(omla_ref[0, r, :], wo_ref[512:1024, :]) for r in halves]
    for r, y in zip(halves, ys):
        h1 = h_ref[0, r, :] + y
        h1_ref[0, r, :] = h1
        xn = _rms(h1, g_ref[...]).astype(BF16)
        q_ref[0, r, :] = (_dot(xn, w_ref[:, 0:1024]) * LOG2E).astype(BF16)
        kv_ref[0, r, :] = _dot(xn, w_ref[:, 1024:1280]).astype(BF16)
        gate_ref[0, r, :] = _dot(xn, w_ref[:, 1280:2304]).astype(BF16)


def _mid(osb, omla, h, tile, wo, g, w):
    B, L, _ = h.shape
    row = lambda width: pl.BlockSpec((1, tile, width), lambda b, t: (b, t, 0))
    full = lambda a: pl.BlockSpec(a.shape, lambda b, t: (0,) * a.ndim)
    return pl.pallas_call(
        _mid_kernel,
        grid=(B, L // tile),
        in_specs=[row(512), row(512), row(D_MODEL), full(wo), full(g), full(w)],
        out_specs=[row(D_MODEL), row(1024), row(256), row(1024)],
        out_shape=[jax.ShapeDtypeStruct((B, L, D_MODEL), F32),
                   jax.ShapeDtypeStruct((B, L, 1024), BF16),
                   jax.ShapeDtypeStruct((B, L, 256), BF16),
                   jax.ShapeDtypeStruct((B, L, 1024), BF16)],
        compiler_params=_params(("parallel", "parallel")),
        name="mid",
    )(osb, omla, h, wo, g, w)


def _swa_kernel(sink_ref, q_ref, kvp_ref, kvc_ref, kvm_ref, g_ref, bb_ref, mb_ref, h_ref, wo_ref, fg_ref,
                out_ref, o_scr):
    n = pl.program_id(1)
    first = jnp.where(n > 0, 0, 1)
    in_cur = _iota((BLOCK, BLOCK), 1) <= _iota((BLOCK, BLOCK), 0)
    blocks_before = (n + 1).astype(F32) * float(BLOCK)
    pairs = SWA_HEADS // 2
    items = [(b, g) for b in range(q_ref.shape[0]) for g in range(pairs // SWA_STACK)]
    k_sl, v_sl = slice(0, LANES), slice(LANES, 2 * LANES)
    heads_of = lambda g: [(p, hh) for p in range(g * SWA_STACK, (g + 1) * SWA_STACK) for hh in range(2)]

    def scores(b, g):
        q_stack = jnp.concatenate(
            [_split_pair(q_ref[b, :, p * LANES:(p + 1) * LANES])[hh] for p, hh in heads_of(g)], axis=0)
        k_band = jnp.concatenate([kvp_ref[b, :, k_sl], kvc_ref[b, :, k_sl]], axis=0)
        return _dot_nt(q_stack, k_band), _dot_nt(q_stack, kvm_ref[0, :, k_sl])

    def probs(g, z_stack, zm_stack):
        p_rows, invs = [], []
        for j, (p, hh) in enumerate(heads_of(g)):
            h = p + pairs * hh
            rows = slice(j * BLOCK, (j + 1) * BLOCK)
            z, z_meta = z_stack[rows], zm_stack[rows]
            slope = 2.0 ** (-8.0 * (h + 1.0) / SWA_HEADS) * LOG2E
            s_band = jnp.where(in_cur, z[:, BLOCK:], z[:, :BLOCK]) - bb_ref[first, h]
            s_meta = z_meta - mb_ref[h] - slope * blocks_before
            sink = sink_ref[h] * LOG2E
            m = jnp.maximum(jnp.max(jnp.maximum(s_band, s_meta), axis=-1, keepdims=True), sink)
            p_band, p_meta = jnp.exp2(s_band - m), jnp.exp2(s_meta - m)
            denom = jnp.sum(p_band + p_meta, axis=-1, keepdims=True) + jnp.exp2(sink - m)
            p_rows.append(jnp.concatenate(
                [jnp.where(in_cur, 0.0, p_band), jnp.where(in_cur, p_band, 0.0), p_meta], axis=1).astype(BF16))
            invs.append(1.0 / denom)
        return jnp.concatenate(p_rows, axis=0), invs

    def values(b, g, p_stack, invs):
        v_all = jnp.concatenate([kvp_ref[b, :, v_sl], kvc_ref[b, :, v_sl], kvm_ref[0, :, v_sl]], axis=0)
        o_stack = _dot(p_stack, v_all)
        for j, (p, hh) in enumerate(heads_of(g)):
            if hh == 1:
                sl = slice(p * LANES, (p + 1) * LANES)
                o = _merge_pair(o_stack[(j - 1) * BLOCK:j * BLOCK] * invs[j - 1],
                                o_stack[j * BLOCK:(j + 1) * BLOCK] * invs[j])
                o_scr[b, :, sl] = (o * _silu(g_ref[b, :, sl].astype(F32))).astype(BF16)

    zs, ps = {}, {}
    for step in range(len(items) + 2 * SKEW):
        if step < len(items):
            zs[step] = scores(*items[step])
        if 0 <= step - SKEW < len(items):
            ps[step - SKEW] = probs(items[step - SKEW][1], *zs.pop(step - SKEW))
        t = step - 2 * SKEW
        if 0 <= t < len(items):
            b, g = items[t]
            values(b, g, *ps.pop(t))
            if g == pairs // SWA_STACK - 1:
                out_ref[b] = _rms(h_ref[b] + _dot(o_scr[b], wo_ref[...]), fg_ref[...])


def _swa_bias_tables():
    row, col = _iota((BLOCK, BLOCK), 0), _iota((BLOCK, BLOCK), 1)
    slopes = (2.0 ** (-8.0 * (jnp.arange(SWA_HEADS, dtype=F32) + 1.0) / SWA_HEADS) * LOG2E)[:, None, None]
    band = slopes * jnp.where(col <= row, row - col, row - col + BLOCK).astype(F32)
    band_first = band + jnp.where(col <= row, 0.0, -NEG)
    meta = jnp.where(col >= N_PAD, 0.0, -NEG) + slopes * (row - col).astype(F32)
    return jnp.stack([band, band_first]), meta


def _swa_attn(sinks, q, kv, kv_meta, gate, h1, wo, final_g):
    B, L, _ = q.shape
    kvw = kv.shape[-1]
    nb = SWA_BATCH
    band_bias, meta_bias = _swa_bias_tables()
    row = lambda w: pl.BlockSpec((nb, BLOCK, w), lambda b, n, s: (b, n, 0))
    const = lambda a: pl.BlockSpec(a.shape, lambda b, n, s: (0,) * a.ndim)
    grid_spec = pltpu.PrefetchScalarGridSpec(
        num_scalar_prefetch=1,
        grid=(B // nb, L // BLOCK),
        in_specs=[row(SWA_WIDTH),
                  pl.BlockSpec((nb, BLOCK, kvw), lambda b, n, s: (b, jnp.maximum(n - 1, 0), 0)),
                  row(kvw), const(kv_meta), row(SWA_WIDTH), const(band_bias), const(meta_bias),
                  row(D_MODEL), const(wo), const(final_g)],
        out_specs=row(D_MODEL),
        scratch_shapes=[pltpu.VMEM((nb, BLOCK, SWA_WIDTH), BF16)],
    )
    return pl.pallas_call(
        _swa_kernel,
        grid_spec=grid_spec,
        out_shape=jax.ShapeDtypeStruct((B, L, D_MODEL), F32),
        compiler_params=_params(("parallel", "parallel")),
        name="swa_attn",
    )(sinks, q, kv, kv, kv_meta, gate, band_bias, meta_bias, h1, wo, final_g)


def _layer0_weights(w_in, w_uq, w_ukv):
    q, k, v, g_sb = (w_in[:, i * 512:(i + 1) * 512] for i in range(4))
    c_q, c_kv = w_in[:, 2048:2304], w_in[:, 2304:2432]
    k_r, g_mla = w_in[:, 2432:2464], w_in[:, 2464:2976]
    half = MLA_ROPE // 2
    r1, r2 = k_r[:, :half], k_r[:, half:]
    z = lambda n: jnp.zeros((D_MODEL, n), w_in.dtype)
    kr_blk = jnp.concatenate([z(MLA_NOPE), r1, r2, z(LANES - MLA_NOPE - MLA_ROPE)], axis=1)
    w0 = jnp.concatenate([q * (SB_DIM ** -0.5), k, v, g_sb, g_mla, c_q, c_kv, kr_blk], axis=1)

    uq = w_uq.reshape(MLA_Q_LORA, MLA_HEADS, MLA_NOPE + MLA_ROPE)
    nope, u1, u2 = uq[..., :MLA_NOPE], uq[..., MLA_NOPE:MLA_NOPE + half], uq[..., MLA_NOPE + half:]
    zq = lambda n: jnp.zeros((MLA_Q_LORA, MLA_HEADS, n), w_uq.dtype)
    uq_main = jnp.concatenate([nope, u1, u2, zq(LANES - MLA_NOPE - MLA_ROPE)], axis=-1)
    uq_rot = jnp.concatenate([zq(MLA_NOPE), -u2, u1, zq(LANES - MLA_NOPE - MLA_ROPE)], axis=-1)
    wuq = jnp.concatenate([uq_main.reshape(MLA_Q_LORA, -1), uq_rot.reshape(MLA_Q_LORA, -1)], axis=1)

    ukv = w_ukv.reshape(MLA_KV_LORA, MLA_HEADS, MLA_NOPE + MLA_V)
    k_nope = jnp.concatenate([ukv[..., :MLA_NOPE],
                              jnp.zeros((MLA_KV_LORA, MLA_HEADS, LANES - MLA_NOPE), w_ukv.dtype)], axis=-1)
    wukv = jnp.concatenate([k_nope.reshape(MLA_KV_LORA, -1),
                            ukv[..., MLA_NOPE:].reshape(MLA_KV_LORA, -1)], axis=1)
    return w0.astype(BF16), wuq.astype(BF16), wukv.astype(BF16)


def _pair_heads(w, axis):
    shape = w.shape
    w = w.reshape(shape[:axis] + (SWA_KV_HEADS, SWA_HEADS // SWA_KV_HEADS, SWA_DIM) + shape[axis + 1:])
    return jnp.swapaxes(w, axis, axis + 1).reshape(shape)


def _layer1_weights(w_in, w_out):
    q, kv, g = w_in[:, :1024], w_in[:, 1024:1280], w_in[:, 1280:2304]
    w1 = jnp.concatenate([_pair_heads(q * (SWA_DIM ** -0.5), 1), kv, _pair_heads(g, 1)], axis=1)
    return w1.astype(BF16), _pair_heads(w_out, 0).astype(BF16)


def _rope_tables():
    half = MLA_ROPE // 2
    pos = jnp.arange(N_META + SEQ).astype(F32)
    inv = ROPE_BASE ** (-jnp.arange(half, dtype=F32) / half)
    ang = pos[:, None] * inv[None, :]
    cos, sin = jnp.cos(ang), jnp.sin(ang)
    n = pos.shape[0]
    z = lambda w: jnp.zeros((n, w), F32)
    tail = LANES - MLA_NOPE - MLA_ROPE
    c = jnp.concatenate([jnp.ones((n, MLA_NOPE), F32), cos, cos, z(tail)], axis=1)
    s = jnp.concatenate([z(MLA_NOPE), sin, sin, z(tail)], axis=1)
    s1 = jnp.concatenate([z(MLA_NOPE), -sin, z(half), z(tail)], axis=1)
    s2 = jnp.concatenate([z(MLA_NOPE), z(half), sin, z(tail)], axis=1)
    scale = (MLA_NOPE + MLA_ROPE) ** -0.5 * LOG2E
    tabs = (c * scale, s * scale, c, s1, s2)
    pad = lambda t: jnp.concatenate([jnp.zeros((N_PAD, LANES), F32), t[:N_META]], axis=0)
    return tuple(pad(t) for t in tabs), tuple(t[N_META:] for t in tabs)


def _suffix_matrix(n):
    return jnp.where(_iota((n, n), 0) >= _iota((n, n), 1), -1.0, 0.0).astype(BF16)


def kernel(x, meta, norm_g, final_g, ev_w_in, ev_q_norm_g, ev_kv_norm_g, ev_w_uq, ev_w_ukv,
           ev_w_out, od_w_in, od_sinks, od_w_out):
    w0, wuq, wukv = _layer0_weights(ev_w_in[0], ev_w_uq[0], ev_w_ukv[0])
    w1, wo1 = _layer1_weights(od_w_in[0], od_w_out[0])
    wo0 = ev_w_out[0].astype(BF16)
    tabs_meta, tabs_real = _rope_tables()
    u, u_meta = _suffix_matrix(ATT_TILE), _suffix_matrix(BLOCK)
    l0 = (norm_g[0:1], w0, ev_q_norm_g[0:1], ev_kv_norm_g[0:1], wuq, wukv)

    hm = jnp.concatenate([jnp.zeros((N_PAD, D_MODEL), x.dtype), meta.astype(x.dtype)], axis=0)[None]
    qsb_m, ksb_m, vsb_m, gate_m, qm_m, km_m, vm_m = _proj0(hm, BLOCK, *l0, tabs_meta)
    osb_m = _sb_meta(qsb_m, ksb_m, vsb_m, gate_m, u_meta)
    omla_m = _mla_meta(qm_m, km_m, vm_m, gate_m)
    _, _, kv1_m, _ = _mid(osb_m, omla_m, hm, BLOCK, wo0, norm_g[1:2], w1)

    q_sb, k_sb, v_sb, gate0, q_mla, k_mla, v_mla = _proj0(x, ROW_TILE, *l0, tabs_real)
    o_sb = _sb_attn(q_sb, k_sb, v_sb, ksb_m, vsb_m, gate0, u, u_meta)
    o_mla = _mla_attn(q_mla, k_mla, v_mla, km_m, vm_m, gate0)
    h1, q1, kv1, gate1 = _mid(o_sb, o_mla, x, ROW_TILE, wo0, norm_g[1:2], w1)
    return _swa_attn(od_sinks[0], q1, kv1, kv1_m, gate1, h1, wo1, final_g[None, :])
```

```python
import math

import numpy as np
import jax
import jax.numpy as jnp
from jax import lax
from jax.experimental import pallas as pl
from jax.experimental.pallas import tpu as pltpu

D_MODEL = 1024
SEQ = 2048
N_META = 16
BLOCK = 128
N_PAD = BLOCK - N_META
NORM_EPS = 1e-6
NEG = -1e30

SB_HEADS = 8
SB_DIM = 64
SB_WIDTH = SB_HEADS * SB_DIM
MLA_HEADS = 8
MLA_Q_LORA = 256
MLA_KV_LORA = 128
MLA_NOPE = 64
MLA_ROPE = 32
MLA_V = 64
MLA_WIDTH = MLA_HEADS * MLA_V
ROPE_BASE = 10000.0
SWA_HEADS = 16
SWA_KV_HEADS = 2
SWA_DIM = 64
SWA_WIDTH = SWA_HEADS * SWA_DIM

LANES = 128
ROW_TILE = 1024
ATT_TILE = 256
ATT_BATCH = 2
VMEM_LIMIT = 48 * 1024 * 1024
LOG2E = math.log2(math.e)
SWA_BATCH = 4
SWA_STACK = 2
SKEW = 1
DEAD_CARRY = -256.0
SOFTPLUS_CLAMP = 64.0

BF16 = jnp.bfloat16
F32 = jnp.float32


def _dot(a, b):
    return jnp.dot(a, b, preferred_element_type=F32)


def _dot_nt(a, b):
    return lax.dot_general(a, b, (((1,), (1,)), ((), ())), preferred_element_type=F32)


def _rms(x, g):
    ms = jnp.mean(x * x, axis=-1, keepdims=True)
    return x * lax.rsqrt(ms + NORM_EPS) * g


def _silu(g):
    return g * (1.0 / (1.0 + jnp.exp(-g)))


def _params(sem):
    return pltpu.CompilerParams(dimension_semantics=sem, vmem_limit_bytes=VMEM_LIMIT)


def _iota(shape, dim):
    return lax.broadcasted_iota(jnp.int32, shape, dim)


def _split_pair(x):
    lane = _iota(x.shape, 1)
    zero = jnp.zeros_like(x)
    return jnp.where(lane < LANES // 2, x, zero), jnp.where(lane >= LANES // 2, x, zero)


def _merge_pair(lo, hi):
    lane = _iota(lo.shape, 1)
    return jnp.where(lane < LANES // 2, lo, hi)


def _proj0_kernel(x_ref, g_ref, w_ref, qg_ref, kvg_ref, wuq_ref, wukv_ref,
                  cq_ref, sq_ref, ck_ref, sk1_ref, sk2_ref,
                  qsb_ref, ksb_ref, vsb_ref, gate_ref, qm_ref, km_ref, vm_ref):
    rows = x_ref.shape[1]
    halves = [slice(0, rows // 2), slice(rows // 2, rows)] if rows >= 2 * BLOCK else [slice(0, rows)]
    qk_w = MLA_HEADS * LANES
    for r in halves:
        xn = _rms(x_ref[0, r, :], g_ref[...]).astype(BF16)
        qsb_ref[0, r, :] = (_dot(xn, w_ref[:, 0:512]) * LOG2E).astype(BF16)
        ksb_ref[0, r, :] = _dot(xn, w_ref[:, 512:1024]).astype(BF16)
        vsb_ref[0, r, :] = _dot(xn, w_ref[:, 1024:1536]).astype(BF16)
        gate_ref[0, r, :] = _dot(xn, w_ref[:, 1536:2560]).astype(BF16)
        lat = _dot(xn, w_ref[:, 2560:3072])
        cqn = _rms(lat[:, 0:256], qg_ref[...]).astype(BF16)
        ckvn = _rms(lat[:, 256:384], kvg_ref[...]).astype(BF16)
        kr = lat[:, 384:512]
        k_rope = (kr * ck_ref[r, :] + pltpu.roll(kr, LANES - MLA_ROPE // 2, 1) * sk1_ref[r, :]
                  + pltpu.roll(kr, MLA_ROPE // 2, 1) * sk2_ref[r, :])
        q_all = _dot(cqn, wuq_ref[...])
        kv_all = _dot(ckvn, wukv_ref[...])
        cq = cq_ref[r, :]
        sq = sq_ref[r, :]
        for h in range(MLA_HEADS):
            lo, hi = h * LANES, (h + 1) * LANES
            qm_ref[0, r, lo:hi] = (q_all[:, lo:hi] * cq + q_all[:, qk_w + lo:qk_w + hi] * sq).astype(BF16)
            km_ref[0, r, lo:hi] = (kv_all[:, lo:hi] + k_rope).astype(BF16)
        vm_ref[0, r, :] = kv_all[:, qk_w:].astype(BF16)


def _proj0(h, tile, g, w, qg, kvg, wuq, wukv, tables):
    B, L, _ = h.shape
    row = lambda width: pl.BlockSpec((1, tile, width), lambda b, t: (b, t, 0))
    full = lambda a: pl.BlockSpec(a.shape, lambda b, t: (0,) * a.ndim)
    tab = pl.BlockSpec((tile, LANES), lambda b, t: (t, 0))
    out = lambda width: jax.ShapeDtypeStruct((B, L, width), BF16)
    return pl.pallas_call(
        _proj0_kernel,
        grid=(B, L // tile),
        in_specs=[row(D_MODEL), full(g), full(w), full(qg), full(kvg), full(wuq), full(wukv),
                  tab, tab, tab, tab, tab],
        out_specs=[row(512), row(512), row(512), row(1024), row(1024), row(1024), row(512)],
        out_shape=[out(512), out(512), out(512), out(1024), out(1024), out(1024), out(512)],
        compiler_params=_params(("parallel", "parallel")),
        name="proj0",
    )(h, g, w, qg, kvg, wuq, wukv, *tables)


def _sb_chunk(qh_ref, kv, u, mask, acc_ref, c_ref, first=False):
    keys = kv[0][0].shape[0]
    items = [(b, h) for b in range(len(kv)) for h in range(SB_HEADS)]
    sl = lambda h: slice((h // 2) * LANES, (h // 2 + 1) * LANES)

    def scores(b, h):
        return _dot_nt(qh_ref[b, h], kv[b][0][:, sl(h)])

    def suffix(z):
        sp = jnp.maximum(jnp.log2(1.0 + jnp.exp2(jnp.minimum(z, SOFTPLUS_CLAMP))), z)
        if mask is not None:
            sp = jnp.where(mask, sp, 0.0)
        return _dot(sp.astype(BF16), u)

    def weights(b, h, z, sfx):
        total = jnp.broadcast_to(sfx[:, 0:1], c_ref.shape[2:])
        if first:
            a = jnp.exp2(z + sfx)
        else:
            c = c_ref[b, h]
            a = jnp.exp2(z + sfx + jnp.concatenate([c] * (keys // LANES), axis=1))
        if mask is not None:
            a = jnp.where(mask, a, 0.0)
        av = _dot(a.astype(BF16), kv[b][1][:, sl(h)])
        if first:
            acc_ref[b, h], c_ref[b, h] = av, total
        else:
            acc_ref[b, h] += av
            c_ref[b, h] = c + total

    zs, sfxs = {}, {}
    for step in range(len(items) + 2 * SKEW):
        if step < len(items):
            zs[step] = scores(*items[step])
        if 0 <= step - SKEW < len(items):
            sfxs[step - SKEW] = suffix(zs[step - SKEW])
        if 0 <= step - 2 * SKEW < len(items):
            weights(*items[step - 2 * SKEW], zs.pop(step - 2 * SKEW), sfxs.pop(step - 2 * SKEW))


def _sb_prologue(q_ref, qh_ref):
    for b in range(q_ref.shape[0]):
        for p in range(SB_HEADS // 2):
            qh_ref[b, 2 * p], qh_ref[b, 2 * p + 1] = _split_pair(q_ref[b, :, p * LANES:(p + 1) * LANES])


def _sb_epilogue(acc_ref, g_ref, o_ref):
    for b in range(o_ref.shape[0]):
        for p in range(SB_HEADS // 2):
            sl = slice(p * LANES, (p + 1) * LANES)
            o = _merge_pair(acc_ref[b, 2 * p], acc_ref[b, 2 * p + 1])
            o_ref[b, :, sl] = (o * _silu(g_ref[b, :, sl].astype(F32))).astype(BF16)


def _sb_kernel(q_ref, k_ref, v_ref, km_ref, vm_ref, g_ref, u_ref, um_ref, o_ref, qh_ref, acc_ref, c_ref):
    i = pl.program_id(1)
    T = ATT_TILE
    nb = q_ref.shape[0]
    _sb_prologue(q_ref, qh_ref)

    def real_chunk(j, mask, first=False):
        start = pl.multiple_of(j * T, T)
        kv = [(k_ref[b, pl.ds(start, T), :], v_ref[b, pl.ds(start, T), :]) for b in range(nb)]
        _sb_chunk(qh_ref, kv, u_ref[...], mask, acc_ref, c_ref, first)

    real_chunk(i, _iota((T, T), 1) < _iota((T, T), 0), first=True)

    def alive():
        return jnp.max(c_ref[...]) > DEAD_CARRY

    def earlier(state):
        idx, _ = state
        real_chunk(i - 1 - idx, None)
        return idx + 1, alive()

    _, still_alive = lax.while_loop(lambda state: (state[0] < i) & state[1], earlier, (jnp.int32(0), alive()))

    @pl.when(still_alive)
    def _():
        _sb_chunk(qh_ref, [(km_ref[0], vm_ref[0])] * nb, um_ref[...], _iota((T, BLOCK), 1) >= N_PAD,
                  acc_ref, c_ref)

    _sb_epilogue(acc_ref, g_ref, o_ref)


def _sb_attn(q, k, v, k_meta, v_meta, gate, u, u_meta):
    B, L, _ = q.shape
    T, nb = ATT_TILE, ATT_BATCH
    tile = pl.BlockSpec((nb, T, SB_WIDTH), lambda b, i: (b, i, 0))
    seq = pl.BlockSpec((nb, L, SB_WIDTH), lambda b, i: (b, 0, 0))
    meta = pl.BlockSpec((1, BLOCK, SB_WIDTH), lambda b, i: (0, 0, 0))
    const = lambda a: pl.BlockSpec(a.shape, lambda b, i: (0, 0))
    return pl.pallas_call(
        _sb_kernel,
        grid=(B // nb, L // T),
        in_specs=[tile, seq, seq, meta, meta, tile, const(u), const(u_meta)],
        out_specs=tile,
        out_shape=jax.ShapeDtypeStruct((B, L, SB_WIDTH), BF16),
        scratch_shapes=[pltpu.VMEM((nb, SB_HEADS, T, LANES), BF16), pltpu.VMEM((nb, SB_HEADS, T, LANES), F32),
                        pltpu.VMEM((nb, SB_HEADS, T, LANES), F32)],
        compiler_params=_params(("parallel", "arbitrary")),
        name="sb_attn",
    )(q, k, v, k_meta, v_meta, gate, u, u_meta)


def _sb_meta_kernel(q_ref, k_ref, v_ref, g_ref, u_ref, o_ref, qh_ref, acc_ref, c_ref):
    _sb_prologue(q_ref, qh_ref)
    row, col = _iota((BLOCK, BLOCK), 0), _iota((BLOCK, BLOCK), 1)
    _sb_chunk(qh_ref, [(k_ref[0], v_ref[0])], u_ref[...], (col < row) & (col >= N_PAD), acc_ref, c_ref, first=True)
    _sb_epilogue(acc_ref, g_ref, o_ref)


def _sb_meta(q, k, v, gate, u_meta):
    blk = pl.BlockSpec((1, BLOCK, SB_WIDTH), lambda i: (0, 0, 0))
    return pl.pallas_call(
        _sb_meta_kernel,
        grid=(1,),
        in_specs=[blk, blk, blk, blk, pl.BlockSpec(u_meta.shape, lambda i: (0, 0))],
        out_specs=blk,
        out_shape=jax.ShapeDtypeStruct((1, BLOCK, SB_WIDTH), BF16),
        scratch_shapes=[pltpu.VMEM((1, SB_HEADS, BLOCK, LANES), BF16), pltpu.VMEM((1, SB_HEADS, BLOCK, LANES), F32),
                        pltpu.VMEM((1, SB_HEADS, BLOCK, LANES), F32)],
        compiler_params=_params(("arbitrary",)),
        name="sb_meta",
    )(q, k, v, gate, u_meta)


def _mla_chunk(q_ref, kv, mask, acc_ref, m_ref, first=False):
    keys = kv[0][0].shape[0]
    items = [(b, h) for b in range(len(kv)) for h in range(MLA_HEADS)]
    ones = jnp.ones((keys, LANES), BF16)

    def scores(b, h):
        hs = slice(h * LANES, (h + 1) * LANES)
        s = _dot_nt(q_ref[b, :, hs], kv[b][0][:, hs])
        return s if mask is None else jnp.where(mask, s, NEG)

    def probs(b, h, s):
        m_new = jnp.max(s, axis=-1, keepdims=True)
        if first:
            m_new = jnp.broadcast_to(m_new, m_ref.shape[2:])
            alpha = None
        else:
            m_old = m_ref[b, h]
            m_new = jnp.maximum(m_old, m_new)
            alpha = jnp.exp2(m_old - m_new)
        m_ref[b, h] = m_new
        p = jnp.exp2(s - jnp.concatenate([m_new] * (keys // LANES), axis=1))
        return p.astype(BF16), alpha

    def update(b, h, p, alpha):
        vs = slice((h // 2) * LANES, (h // 2 + 1) * LANES)
        pv = _dot(p, jnp.concatenate([kv[b][1][:, vs], ones], axis=1))
        acc_ref[b, h] = pv if first else jnp.concatenate([alpha, alpha], axis=1) * acc_ref[b, h] + pv

    ss, ps = {}, {}
    for step in range(len(items) + 2 * SKEW):
        if step < len(items):
            ss[step] = scores(*items[step])
        if 0 <= step - SKEW < len(items):
            ps[step - SKEW] = probs(*items[step - SKEW], ss.pop(step - SKEW))
        if 0 <= step - 2 * SKEW < len(items):
            update(*items[step - 2 * SKEW], *ps.pop(step - 2 * SKEW))


def _mla_finish(acc_ref, g_ref, o_ref):
    for b in range(o_ref.shape[0]):
        for p in range(MLA_HEADS // 2):
            sl = slice(p * LANES, (p + 1) * LANES)
            lo = acc_ref[b, 2 * p, :, :LANES] / acc_ref[b, 2 * p, :, LANES:]
            hi = acc_ref[b, 2 * p + 1, :, :LANES] / acc_ref[b, 2 * p + 1, :, LANES:]
            gate = g_ref[b, :, MLA_WIDTH + p * LANES:MLA_WIDTH + (p + 1) * LANES].astype(F32)
            o_ref[b, :, sl] = (_merge_pair(lo, hi) * _silu(gate)).astype(BF16)


def _mla_kernel(q_ref, k_ref, v_ref, km_ref, vm_ref, g_ref, o_ref, acc_ref, m_ref):
    i = pl.program_id(1)
    T = ATT_TILE
    nb = q_ref.shape[0]

    def real_kv(j):
        start = pl.multiple_of(j * T, T)
        return [(k_ref[b, pl.ds(start, T), :], v_ref[b, pl.ds(start, T), :]) for b in range(nb)]

    def first_chunk(extra):
        parts = [real_kv(i)] + ([real_kv(0)] if extra else []) + [[(km_ref[0], vm_ref[0])] * nb]
        kv = [tuple(jnp.concatenate([part[b][j] for part in parts], axis=0) for j in range(2)) for b in range(nb)]
        masks = ([_iota((T, T), 1) <= _iota((T, T), 0)] + ([_iota((T, T), 1) >= 0] if extra else [])
                 + [_iota((T, BLOCK), 1) >= N_PAD])
        _mla_chunk(q_ref, kv, jnp.concatenate(masks, axis=1), acc_ref, m_ref, first=True)

    odd = i % 2

    @pl.when(odd == 1)
    def _():
        first_chunk(True)

    @pl.when(odd == 0)
    def _():
        first_chunk(False)

    def earlier(idx, carry):
        start = pl.multiple_of((odd + 2 * idx) * T, T)
        kv = [(k_ref[b, pl.ds(start, 2 * T), :], v_ref[b, pl.ds(start, 2 * T), :]) for b in range(nb)]
        _mla_chunk(q_ref, kv, None, acc_ref, m_ref)
        return carry

    lax.fori_loop(0, i // 2, earlier, 0)
    _mla_finish(acc_ref, g_ref, o_ref)


def _mla_attn(q, k, v, k_meta, v_meta, gate):
    B, L, _ = q.shape
    T, nb = ATT_TILE, ATT_BATCH
    qk_w = MLA_HEADS * LANES
    return pl.pallas_call(
        _mla_kernel,
        grid=(B // nb, L // T),
        in_specs=[pl.BlockSpec((nb, T, qk_w), lambda b, i: (b, i, 0)),
                  pl.BlockSpec((nb, L, qk_w), lambda b, i: (b, 0, 0)),
                  pl.BlockSpec((nb, L, MLA_WIDTH), lambda b, i: (b, 0, 0)),
                  pl.BlockSpec((1, BLOCK, qk_w), lambda b, i: (0, 0, 0)),
                  pl.BlockSpec((1, BLOCK, MLA_WIDTH), lambda b, i: (0, 0, 0)),
                  pl.BlockSpec((nb, T, 2 * MLA_WIDTH), lambda b, i: (b, i, 0))],
        out_specs=pl.BlockSpec((nb, T, MLA_WIDTH), lambda b, i: (b, i, 0)),
        out_shape=jax.ShapeDtypeStruct((B, L, MLA_WIDTH), BF16),
        scratch_shapes=[pltpu.VMEM((nb, MLA_HEADS, T, 2 * LANES), F32), pltpu.VMEM((nb, MLA_HEADS, T, LANES), F32)],
        compiler_params=_params(("parallel", "arbitrary")),
        name="mla_attn",
    )(q, k, v, k_meta, v_meta, gate)


def _mla_meta_kernel(q_ref, k_ref, v_ref, g_ref, o_ref, acc_ref, m_ref):
    row, col = _iota((BLOCK, BLOCK), 0), _iota((BLOCK, BLOCK), 1)
    _mla_chunk(q_ref, [(k_ref[0], v_ref[0])], (col <= row) & (col >= N_PAD), acc_ref, m_ref, first=True)
    _mla_finish(acc_ref, g_ref, o_ref)


def _mla_meta(q, k, v, gate):
    qk_w = MLA_HEADS * LANES
    blk = lambda w: pl.BlockSpec((1, BLOCK, w), lambda i: (0, 0, 0))
    return pl.pallas_call(
        _mla_meta_kernel,
        grid=(1,),
        in_specs=[blk(qk_w), blk(qk_w), blk(MLA_WIDTH), blk(2 * MLA_WIDTH)],
        out_specs=blk(MLA_WIDTH),
        out_shape=jax.ShapeDtypeStruct((1, BLOCK, MLA_WIDTH), BF16),
        scratch_shapes=[pltpu.VMEM((1, MLA_HEADS, BLOCK, 2 * LANES), F32),
                        pltpu.VMEM((1, MLA_HEADS, BLOCK, LANES), F32)],
        compiler_params=_params(("arbitrary",)),
        name="mla_meta",
    )(q, k, v, gate)


def _mid_kernel(osb_ref, omla_ref, h_ref, wo_ref, g_ref, w_ref, h1_ref, q_ref, kv_ref, gate_ref):
    rows = h_ref.shape[1]
    halves = [slice(0, rows // 2), slice(rows // 2, rows)] if rows >= 2 * BLOCK else [slice(0, rows)]
    ys = [_dot(osb_ref[0, r, :], wo_ref[0:512, :]) + _dot(omla_ref[0, r, :], wo_ref[512:1024, :]) for r in halves]
    for r, y in zip(halves, ys):
        h1 = h_ref[0, r, :] + y
        h1_ref[0, r, :] = h1
        xn = _rms(h1, g_ref[...]).astype(BF16)
        q_ref[0, r, :] = (_dot(xn, w_ref[:, 0:1024]) * LOG2E).astype(BF16)
        kv_ref[0, r, :] = _dot(xn, w_ref[:, 1024:1280]).astype(BF16)
        gate_ref[0, r, :] = _dot(xn, w_ref[:, 1280:2304]).astype(BF16)


def _mid(osb, omla, h, tile, wo, g, w):
    B, L, _ = h.shape
    row = lambda width: pl.BlockSpec((1, tile, width), lambda b, t: (b, t, 0))
    full = lambda a: pl.BlockSpec(a.shape, lambda b, t: (0,) * a.ndim)
    return pl.pallas_call(
        _mid_kernel,
        grid=(B, L // tile),
        in_specs=[row(512), row(512), row(D_MODEL), full(wo), full(g), full(w)],
        out_specs=[row(D_MODEL), row(1024), row(256), row(1024)],
        out_shape=[jax.ShapeDtypeStruct((B, L, D_MODEL), F32),
                   jax.ShapeDtypeStruct((B, L, 1024), BF16),
                   jax.ShapeDtypeStruct((B, L, 256), BF16),
                   jax.ShapeDtypeStruct((B, L, 1024), BF16)],
        compiler_params=_params(("parallel", "parallel")),
        name="mid",
    )(osb, omla, h, wo, g, w)


def _swa_kernel(sink_ref, q_ref, kvp_ref, kvc_ref, kvm_ref, g_ref, bb_ref, mb_ref, h_ref, wo_ref, fg_ref,
                out_ref, o_scr):
    n = pl.program_id(1)
    first = jnp.where(n > 0, 0, 1)
    in_cur = _iota((BLOCK, BLOCK), 1) <= _iota((BLOCK, BLOCK), 0)
    blocks_before = (n + 1).astype(F32) * float(BLOCK)
    pairs = SWA_HEADS // 2
    items = [(b, g) for b in range(q_ref.shape[0]) for g in range(pairs // SWA_STACK)]
    k_sl, v_sl = slice(0, LANES), slice(LANES, 2 * LANES)
    heads_of = lambda g: [(p, hh) for p in range(g * SWA_STACK, (g + 1) * SWA_STACK) for hh in range(2)]

    def scores(b, g):
        q_stack = jnp.concatenate(
            [_split_pair(q_ref[b, :, p * LANES:(p + 1) * LANES])[hh] for p, hh in heads_of(g)], axis=0)
        k_band = jnp.concatenate([kvp_ref[b, :, k_sl], kvc_ref[b, :, k_sl]], axis=0)
        return _dot_nt(q_stack, k_band), _dot_nt(q_stack, kvm_ref[0, :, k_sl])

    def probs(g, z_stack, zm_stack):
        p_rows, invs = [], []
        for j, (p, hh) in enumerate(heads_of(g)):
            h = p + pairs * hh
            rows = slice(j * BLOCK, (j + 1) * BLOCK)
            z, z_meta = z_stack[rows], zm_stack[rows]
            slope = 2.0 ** (-8.0 * (h + 1.0) / SWA_HEADS) * LOG2E
            s_band = jnp.where(in_cur, z[:, BLOCK:], z[:, :BLOCK]) - bb_ref[first, h]
            s_meta = z_meta - mb_ref[h] - slope * blocks_before
            sink = sink_ref[h] * LOG2E
            m = jnp.maximum(jnp.max(jnp.maximum(s_band, s_meta), axis=-1, keepdims=True), sink)
            p_band, p_meta = jnp.exp2(s_band - m), jnp.exp2(s_meta - m)
            denom = jnp.sum(p_band + p_meta, axis=-1, keepdims=True) + jnp.exp2(sink - m)
            p_rows.append(jnp.concatenate(
                [jnp.where(in_cur, 0.0, p_band), jnp.where(in_cur, p_band, 0.0), p_meta], axis=1).astype(BF16))
            invs.append(1.0 / denom)
        return jnp.concatenate(p_rows, axis=0), invs

    def values(b, g, p_stack, invs):
        v_all = jnp.concatenate([kvp_ref[b, :, v_sl], kvc_ref[b, :, v_sl], kvm_ref[0, :, v_sl]], axis=0)
        o_stack = _dot(p_stack, v_all)
        for j, (p, hh) in enumerate(heads_of(g)):
            if hh == 1:
                sl = slice(p * LANES, (p + 1) * LANES)
                o = _merge_pair(o_stack[(j - 1) * BLOCK:j * BLOCK] * invs[j - 1],
                                o_stack[j * BLOCK:(j + 1) * BLOCK] * invs[j])
                o_scr[b, :, sl] = (o * _silu(g_ref[b, :, sl].astype(F32))).astype(BF16)

    zs, ps = {}, {}
    for step in range(len(items) + 2 * SKEW):
        if step < len(items):
            zs[step] = scores(*items[step])
        if 0 <= step - SKEW < len(items):
            ps[step - SKEW] = probs(items[step - SKEW][1], *zs.pop(step - SKEW))
        t = step - 2 * SKEW
        if 0 <= t < len(items):
            b, g = items[t]
            values(b, g, *ps.pop(t))
            if g == pairs // SWA_STACK - 1:
                out_ref[b] = _rms(h_ref[b] + _dot(o_scr[b], wo_ref[...]), fg_ref[...])


def _swa_bias_tables():
    row, col = np.arange(BLOCK)[:, None], np.arange(BLOCK)[None, :]
    slopes = (2.0 ** (-8.0 * (np.arange(SWA_HEADS) + 1.0) / SWA_HEADS) * LOG2E)[:, None, None]
    band = slopes * np.where(col <= row, row - col, row - col + BLOCK)
    band_first = band + np.where(col <= row, 0.0, -NEG)
    meta = np.where(col >= N_PAD, 0.0, -NEG) + slopes * (row - col)
    return jnp.asarray(np.stack([band, band_first]), F32), jnp.asarray(meta, F32)


def _swa_attn(sinks, q, kv, kv_meta, gate, h1, wo, final_g):
    B, L, _ = q.shape
    kvw = kv.shape[-1]
    nb = SWA_BATCH
    band_bias, meta_bias = _swa_bias_tables()
    row = lambda w: pl.BlockSpec((nb, BLOCK, w), lambda b, n, s: (b, n, 0))
    const = lambda a: pl.BlockSpec(a.shape, lambda b, n, s: (0,) * a.ndim)
    grid_spec = pltpu.PrefetchScalarGridSpec(
        num_scalar_prefetch=1,
        grid=(B // nb, L // BLOCK),
        in_specs=[row(SWA_WIDTH),
                  pl.BlockSpec((nb, BLOCK, kvw), lambda b, n, s: (b, jnp.maximum(n - 1, 0), 0)),
                  row(kvw), const(kv_meta), row(SWA_WIDTH), const(band_bias), const(meta_bias),
                  row(D_MODEL), const(wo), const(final_g)],
        out_specs=row(D_MODEL),
        scratch_shapes=[pltpu.VMEM((nb, BLOCK, SWA_WIDTH), BF16)],
    )
    return pl.pallas_call(
        _swa_kernel,
        grid_spec=grid_spec,
        out_shape=jax.ShapeDtypeStruct((B, L, D_MODEL), F32),
        compiler_params=_params(("parallel", "parallel")),
        name="swa_attn",
    )(sinks, q, kv, kv, kv_meta, gate, band_bias, meta_bias, h1, wo, final_g)


def _layer0_weights(w_in, w_uq, w_ukv):
    w_in, w_uq, w_ukv = w_in.astype(BF16), w_uq.astype(BF16), w_ukv.astype(BF16)
    q, k, v, g_sb = (w_in[:, i * 512:(i + 1) * 512] for i in range(4))
    c_q, c_kv = w_in[:, 2048:2304], w_in[:, 2304:2432]
    k_r, g_mla = w_in[:, 2432:2464], w_in[:, 2464:2976]
    half = MLA_ROPE // 2
    r1, r2 = k_r[:, :half], k_r[:, half:]
    z = lambda n: jnp.zeros((D_MODEL, n), w_in.dtype)
    kr_blk = jnp.concatenate([z(MLA_NOPE), r1, r2, z(LANES - MLA_NOPE - MLA_ROPE)], axis=1)
    w0 = jnp.concatenate([q * (SB_DIM ** -0.5), k, v, g_sb, g_mla, c_q, c_kv, kr_blk], axis=1)

    uq = w_uq.reshape(MLA_Q_LORA, MLA_HEADS, MLA_NOPE + MLA_ROPE)
    nope, u1, u2 = uq[..., :MLA_NOPE], uq[..., MLA_NOPE:MLA_NOPE + half], uq[..., MLA_NOPE + half:]
    zq = lambda n: jnp.zeros((MLA_Q_LORA, MLA_HEADS, n), w_uq.dtype)
    uq_main = jnp.concatenate([nope, u1, u2, zq(LANES - MLA_NOPE - MLA_ROPE)], axis=-1)
    uq_rot = jnp.concatenate([zq(MLA_NOPE), -u2, u1, zq(LANES - MLA_NOPE - MLA_ROPE)], axis=-1)
    wuq = jnp.concatenate([uq_main.reshape(MLA_Q_LORA, -1), uq_rot.reshape(MLA_Q_LORA, -1)], axis=1)

    ukv = w_ukv.reshape(MLA_KV_LORA, MLA_HEADS, MLA_NOPE + MLA_V)
    k_nope = jnp.concatenate([ukv[..., :MLA_NOPE],
                              jnp.zeros((MLA_KV_LORA, MLA_HEADS, LANES - MLA_NOPE), w_ukv.dtype)], axis=-1)
    wukv = jnp.concatenate([k_nope.reshape(MLA_KV_LORA, -1),
                            ukv[..., MLA_NOPE:].reshape(MLA_KV_LORA, -1)], axis=1)
    return w0, wuq, wukv


def _pair_heads(w, axis):
    shape = w.shape
    w = w.reshape(shape[:axis] + (SWA_KV_HEADS, SWA_HEADS // SWA_KV_HEADS, SWA_DIM) + shape[axis + 1:])
    return jnp.swapaxes(w, axis, axis + 1).reshape(shape)


def _layer1_weights(w_in, w_out):
    w_in, w_out = w_in.astype(BF16), w_out.astype(BF16)
    q, kv, g = w_in[:, :1024], w_in[:, 1024:1280], w_in[:, 1280:2304]
    w1 = jnp.concatenate([_pair_heads(q * (SWA_DIM ** -0.5), 1), kv, _pair_heads(g, 1)], axis=1)
    return w1, _pair_heads(w_out, 0)


def _rope_tables():
    half = MLA_ROPE // 2
    pos = np.arange(N_META + SEQ, dtype=np.float64)
    inv = ROPE_BASE ** (-np.arange(half, dtype=np.float64) / half)
    ang = pos[:, None] * inv[None, :]
    cos, sin = np.cos(ang), np.sin(ang)
    n = pos.shape[0]
    z = lambda w: np.zeros((n, w))
    tail = LANES - MLA_NOPE - MLA_ROPE
    c = np.concatenate([np.ones((n, MLA_NOPE)), cos, cos, z(tail)], axis=1)
    s = np.concatenate([z(MLA_NOPE), sin, sin, z(tail)], axis=1)
    s1 = np.concatenate([z(MLA_NOPE), -sin, z(half), z(tail)], axis=1)
    s2 = np.concatenate([z(MLA_NOPE), z(half), sin, z(tail)], axis=1)
    scale = (MLA_NOPE + MLA_ROPE) ** -0.5 * LOG2E
    tabs = (c * scale, s * scale, c, s1, s2)
    pad = lambda t: np.concatenate([np.zeros((N_PAD, LANES)), t[:N_META]], axis=0)
    return (tuple(jnp.asarray(pad(t), F32) for t in tabs), tuple(jnp.asarray(t[N_META:], F32) for t in tabs))


def _suffix_matrix(n):
    return jnp.asarray(np.where(np.arange(n)[:, None] >= np.arange(n)[None, :], -1.0, 0.0), BF16)


def kernel(x, meta, norm_g, final_g, ev_w_in, ev_q_norm_g, ev_kv_norm_g, ev_w_uq, ev_w_ukv,
           ev_w_out, od_w_in, od_sinks, od_w_out):
    w0, wuq, wukv = _layer0_weights(ev_w_in[0], ev_w_uq[0], ev_w_ukv[0])
    w1, wo1 = _layer1_weights(od_w_in[0], od_w_out[0])
    wo0 = ev_w_out[0].astype(BF16)
    tabs_meta, tabs_real = _rope_tables()
    u, u_meta = _suffix_matrix(ATT_TILE), _suffix_matrix(BLOCK)
    l0 = (norm_g[0:1], w0, ev_q_norm_g[0:1], ev_kv_norm_g[0:1], wuq, wukv)

    hm = jnp.concatenate([jnp.zeros((N_PAD, D_MODEL), x.dtype), meta.astype(x.dtype)], axis=0)[None]
    qsb_m, ksb_m, vsb_m, gate_m, qm_m, km_m, vm_m = _proj0(hm, BLOCK, *l0, tabs_meta)
    osb_m = _sb_meta(qsb_m, ksb_m, vsb_m, gate_m, u_meta)
    omla_m = _mla_meta(qm_m, km_m, vm_m, gate_m)
    _, _, kv1_m, _ = _mid(osb_m, omla_m, hm, BLOCK, wo0, norm_g[1:2], w1)

    q_sb, k_sb, v_sb, gate0, q_mla, k_mla, v_mla = _proj0(x, ROW_TILE, *l0, tabs_real)
    o_sb = _sb_attn(q_sb, k_sb, v_sb, ksb_m, vsb_m, gate0, u, u_meta)
    o_mla = _mla_attn(q_mla, k_mla, v_mla, km_m, vm_m, gate0)
    h1, q1, kv1, gate1 = _mid(o_sb, o_mla, x, ROW_TILE, wo0, norm_g[1:2], w1)
    return _swa_attn(od_sinks[0], q1, kv1, kv1_m, gate1, h1, wo1, final_g[None, :])
```

```python
import math

import numpy as np
import jax
import jax.numpy as jnp
from jax import lax
from jax.experimental import pallas as pl
from jax.experimental.pallas import tpu as pltpu

D_MODEL = 1024
SEQ = 2048
N_META = 16
BLOCK = 128
N_PAD = BLOCK - N_META
NORM_EPS = 1e-6
NEG = -1e30

SB_HEADS = 8
SB_DIM = 64
SB_WIDTH = SB_HEADS * SB_DIM
MLA_HEADS = 8
MLA_Q_LORA = 256
MLA_KV_LORA = 128
MLA_NOPE = 64
MLA_ROPE = 32
MLA_V = 64
MLA_WIDTH = MLA_HEADS * MLA_V
ROPE_BASE = 10000.0
SWA_HEADS = 16
SWA_KV_HEADS = 2
SWA_DIM = 64
SWA_WIDTH = SWA_HEADS * SWA_DIM

LANES = 128
ROW_TILE = 1024
ATT_TILE = 256
ATT_BATCH = 2
VMEM_LIMIT = 48 * 1024 * 1024
LOG2E = math.log2(math.e)
SWA_BATCH = 4
SWA_STACK = 2
MLA_SKEW = 2
SKEW = 1
DEAD_CARRY = -256.0
SOFTPLUS_CLAMP = 64.0

BF16 = jnp.bfloat16
F32 = jnp.float32


def _dot(a, b):
    return jnp.dot(a, b, preferred_element_type=F32)


def _dot_nt(a, b):
    return lax.dot_general(a, b, (((1,), (1,)), ((), ())), preferred_element_type=F32)


def _rms(x, g):
    ms = jnp.mean(x * x, axis=-1, keepdims=True)
    return x * lax.rsqrt(ms + NORM_EPS) * g


def _silu(g):
    return g * (1.0 / (1.0 + jnp.exp(-g)))


def _params(sem):
    return pltpu.CompilerParams(dimension_semantics=sem, vmem_limit_bytes=VMEM_LIMIT)


def _iota(shape, dim):
    return lax.broadcasted_iota(jnp.int32, shape, dim)


def _split_pair(x):
    lane = _iota(x.shape, 1)
    zero = jnp.zeros_like(x)
    return jnp.where(lane < LANES // 2, x, zero), jnp.where(lane >= LANES // 2, x, zero)


def _merge_pair(lo, hi):
    lane = _iota(lo.shape, 1)
    return jnp.where(lane < LANES // 2, lo, hi)


def _proj0_kernel(x_ref, g_ref, w_ref, qg_ref, kvg_ref, wuq_ref, wukv_ref,
                  cq_ref, sq_ref, ck_ref, sk1_ref, sk2_ref,
                  qsb_ref, ksb_ref, vsb_ref, gate_ref, qm_ref, km_ref, vm_ref):
    rows = x_ref.shape[1]
    halves = [slice(0, rows // 2), slice(rows // 2, rows)] if rows >= 2 * BLOCK else [slice(0, rows)]
    qk_w = MLA_HEADS * LANES
    for r in halves:
        xn = _rms(x_ref[0, r, :], g_ref[...]).astype(BF16)
        qsb_ref[0, r, :] = (_dot(xn, w_ref[:, 0:512]) * LOG2E).astype(BF16)
        ksb_ref[0, r, :] = _dot(xn, w_ref[:, 512:1024]).astype(BF16)
        vsb_ref[0, r, :] = _dot(xn, w_ref[:, 1024:1536]).astype(BF16)
        gate_ref[0, r, :] = _dot(xn, w_ref[:, 1536:2560]).astype(BF16)
        lat = _dot(xn, w_ref[:, 2560:3072])
        cqn = _rms(lat[:, 0:256], qg_ref[...]).astype(BF16)
        ckvn = _rms(lat[:, 256:384], kvg_ref[...]).astype(BF16)
        kr = lat[:, 384:512]
        k_rope = (kr * ck_ref[r, :] + pltpu.roll(kr, LANES - MLA_ROPE // 2, 1) * sk1_ref[r, :]
                  + pltpu.roll(kr, MLA_ROPE // 2, 1) * sk2_ref[r, :])
        q_all = _dot(cqn, wuq_ref[...])
        kv_all = _dot(ckvn, wukv_ref[...])
        cq = cq_ref[r, :]
        sq = sq_ref[r, :]
        for h in range(MLA_HEADS):
            lo, hi = h * LANES, (h + 1) * LANES
            qm_ref[0, r, lo:hi] = (q_all[:, lo:hi] * cq + q_all[:, qk_w + lo:qk_w + hi] * sq).astype(BF16)
            km_ref[0, r, lo:hi] = (kv_all[:, lo:hi] + k_rope).astype(BF16)
        vm_ref[0, r, :] = kv_all[:, qk_w:].astype(BF16)


def _proj0(h, tile, g, w, qg, kvg, wuq, wukv, tables):
    B, L, _ = h.shape
    row = lambda width: pl.BlockSpec((1, tile, width), lambda b, t: (b, t, 0))
    full = lambda a: pl.BlockSpec(a.shape, lambda b, t: (0,) * a.ndim)
    tab = pl.BlockSpec((tile, LANES), lambda b, t: (t, 0))
    out = lambda width: jax.ShapeDtypeStruct((B, L, width), BF16)
    return pl.pallas_call(
        _proj0_kernel,
        grid=(B, L // tile),
        in_specs=[row(D_MODEL), full(g), full(w), full(qg), full(kvg), full(wuq), full(wukv),
                  tab, tab, tab, tab, tab],
        out_specs=[row(512), row(512), row(512), row(1024), row(1024), row(1024), row(512)],
        out_shape=[out(512), out(512), out(512), out(1024), out(1024), out(1024), out(512)],
        compiler_params=_params(("parallel", "parallel")),
        name="proj0",
    )(h, g, w, qg, kvg, wuq, wukv, *tables)


def _sb_chunk(qh_ref, kv, u, mask, acc_ref, c_ref, first=False):
    keys = kv[0][0].shape[0]
    items = [(b, h) for b in range(len(kv)) for h in range(SB_HEADS)]
    sl = lambda h: slice((h // 2) * LANES, (h // 2 + 1) * LANES)

    def scores(b, h):
        return _dot_nt(qh_ref[b, h], kv[b][0][:, sl(h)])

    def suffix(z):
        sp = jnp.maximum(jnp.log2(1.0 + jnp.exp2(jnp.minimum(z, SOFTPLUS_CLAMP))), z)
        if mask is not None:
            sp = jnp.where(mask, sp, 0.0)
        return _dot(sp.astype(BF16), u)

    def weights(b, h, z, sfx):
        total = jnp.broadcast_to(sfx[:, 0:1], c_ref.shape[2:])
        if first:
            a = jnp.exp2(z + sfx)
        else:
            c = c_ref[b, h]
            a = jnp.exp2(z + sfx + jnp.concatenate([c] * (keys // LANES), axis=1))
        if mask is not None:
            a = jnp.where(mask, a, 0.0)
        av = _dot(a.astype(BF16), kv[b][1][:, sl(h)])
        if first:
            acc_ref[b, h], c_ref[b, h] = av, total
        else:
            acc_ref[b, h] += av
            c_ref[b, h] = c + total

    zs, sfxs = {}, {}
    for step in range(len(items) + 2 * SKEW):
        if step < len(items):
            zs[step] = scores(*items[step])
        if 0 <= step - SKEW < len(items):
            sfxs[step - SKEW] = suffix(zs[step - SKEW])
        if 0 <= step - 2 * SKEW < len(items):
            weights(*items[step - 2 * SKEW], zs.pop(step - 2 * SKEW), sfxs.pop(step - 2 * SKEW))


def _sb_prologue(q_ref, qh_ref):
    for b in range(q_ref.shape[0]):
        for p in range(SB_HEADS // 2):
            qh_ref[b, 2 * p], qh_ref[b, 2 * p + 1] = _split_pair(q_ref[b, :, p * LANES:(p + 1) * LANES])


def _sb_epilogue(acc_ref, g_ref, o_ref):
    for b in range(o_ref.shape[0]):
        for p in range(SB_HEADS // 2):
            sl = slice(p * LANES, (p + 1) * LANES)
            o = _merge_pair(acc_ref[b, 2 * p], acc_ref[b, 2 * p + 1])
            o_ref[b, :, sl] = (o * _silu(g_ref[b, :, sl].astype(F32))).astype(BF16)


def _sb_kernel(q_ref, k_ref, v_ref, km_ref, vm_ref, g_ref, u_ref, um_ref, o_ref, qh_ref, acc_ref, c_ref):
    i = pl.program_id(1)
    T = ATT_TILE
    nb = q_ref.shape[0]
    _sb_prologue(q_ref, qh_ref)

    def real_chunk(j, mask, first=False):
        start = pl.multiple_of(j * T, T)
        kv = [(k_ref[b, pl.ds(start, T), :], v_ref[b, pl.ds(start, T), :]) for b in range(nb)]
        _sb_chunk(qh_ref, kv, u_ref[...], mask, acc_ref, c_ref, first)

    real_chunk(i, _iota((T, T), 1) < _iota((T, T), 0), first=True)

    def alive():
        return jnp.max(c_ref[...]) > DEAD_CARRY

    def earlier(state):
        idx, _ = state
        real_chunk(i - 1 - idx, None)
        return idx + 1, alive()

    _, still_alive = lax.while_loop(lambda state: (state[0] < i) & state[1], earlier, (jnp.int32(0), alive()))

    @pl.when(still_alive)
    def _():
        _sb_chunk(qh_ref, [(km_ref[0], vm_ref[0])] * nb, um_ref[...], _iota((T, BLOCK), 1) >= N_PAD,
                  acc_ref, c_ref)

    _sb_epilogue(acc_ref, g_ref, o_ref)


def _sb_attn(q, k, v, k_meta, v_meta, gate, u, u_meta):
    B, L, _ = q.shape
    T, nb = ATT_TILE, ATT_BATCH
    tile = pl.BlockSpec((nb, T, SB_WIDTH), lambda b, i: (b, i, 0))
    seq = pl.BlockSpec((nb, L, SB_WIDTH), lambda b, i: (b, 0, 0))
    meta = pl.BlockSpec((1, BLOCK, SB_WIDTH), lambda b, i: (0, 0, 0))
    const = lambda a: pl.BlockSpec(a.shape, lambda b, i: (0, 0))
    return pl.pallas_call(
        _sb_kernel,
        grid=(B // nb, L // T),
        in_specs=[tile, seq, seq, meta, meta, tile, const(u), const(u_meta)],
        out_specs=tile,
        out_shape=jax.ShapeDtypeStruct((B, L, SB_WIDTH), BF16),
        scratch_shapes=[pltpu.VMEM((nb, SB_HEADS, T, LANES), BF16), pltpu.VMEM((nb, SB_HEADS, T, LANES), F32),
                        pltpu.VMEM((nb, SB_HEADS, T, LANES), F32)],
        compiler_params=_params(("parallel", "arbitrary")),
        name="sb_attn",
    )(q, k, v, k_meta, v_meta, gate, u, u_meta)


def _sb_meta_kernel(q_ref, k_ref, v_ref, g_ref, u_ref, o_ref, qh_ref, acc_ref, c_ref):
    _sb_prologue(q_ref, qh_ref)
    row, col = _iota((BLOCK, BLOCK), 0), _iota((BLOCK, BLOCK), 1)
    _sb_chunk(qh_ref, [(k_ref[0], v_ref[0])], u_ref[...], (col < row) & (col >= N_PAD), acc_ref, c_ref, first=True)
    _sb_epilogue(acc_ref, g_ref, o_ref)


def _sb_meta(q, k, v, gate, u_meta):
    blk = pl.BlockSpec((1, BLOCK, SB_WIDTH), lambda i: (0, 0, 0))
    return pl.pallas_call(
        _sb_meta_kernel,
        grid=(1,),
        in_specs=[blk, blk, blk, blk, pl.BlockSpec(u_meta.shape, lambda i: (0, 0))],
        out_specs=blk,
        out_shape=jax.ShapeDtypeStruct((1, BLOCK, SB_WIDTH), BF16),
        scratch_shapes=[pltpu.VMEM((1, SB_HEADS, BLOCK, LANES), BF16), pltpu.VMEM((1, SB_HEADS, BLOCK, LANES), F32),
                        pltpu.VMEM((1, SB_HEADS, BLOCK, LANES), F32)],
        compiler_params=_params(("arbitrary",)),
        name="sb_meta",
    )(q, k, v, gate, u_meta)


def _mla_chunk(q_ref, kv, mask, acc_ref, m_ref, l_ref, first=False):
    items = [(b, h) for b in range(len(kv)) for h in range(MLA_HEADS)]
    rep = lambda row: jnp.broadcast_to(row, m_ref.shape[2:])

    def scores(b, h):
        hs = slice(h * LANES, (h + 1) * LANES)
        s = _dot_nt(kv[b][0][:, hs], q_ref[b, :, hs])
        return s if mask is None else jnp.where(mask, s, NEG)

    def probs(b, h, s):
        m_new = jnp.max(s, axis=0, keepdims=True)
        if first:
            alpha = None
        else:
            m_old = m_ref[b, h, 0:1, :]
            m_new = jnp.maximum(m_old, m_new)
            alpha = jnp.exp2(m_old - m_new)
        m_ref[b, h] = rep(m_new)
        p = jnp.exp2(s - m_new)
        return p.astype(BF16), jnp.sum(p, axis=0, keepdims=True), alpha

    def update(b, h, p, l_new, alpha):
        vs = slice((h // 2) * LANES, (h // 2 + 1) * LANES)
        pv = lax.dot_general(kv[b][1][:, vs], p, (((0,), (0,)), ((), ())), preferred_element_type=F32)
        if first:
            acc_ref[b, h], l_ref[b, h] = pv, rep(l_new)
        else:
            acc_ref[b, h] = alpha * acc_ref[b, h] + pv
            l_ref[b, h] = rep(alpha * l_ref[b, h, 0:1, :] + l_new)

    ss, ps = {}, {}
    for step in range(len(items) + 2 * MLA_SKEW):
        if step < len(items):
            ss[step] = scores(*items[step])
        if 0 <= step - MLA_SKEW < len(items):
            ps[step - MLA_SKEW] = probs(*items[step - MLA_SKEW], ss.pop(step - MLA_SKEW))
        if 0 <= step - 2 * MLA_SKEW < len(items):
            update(*items[step - 2 * MLA_SKEW], *ps.pop(step - 2 * MLA_SKEW))


def _mla_finish(acc_ref, l_ref, g_ref, o_ref):
    half = LANES // 2
    for b in range(o_ref.shape[0]):
        for p in range(MLA_HEADS // 2):
            sl = slice(p * LANES, (p + 1) * LANES)
            lo = acc_ref[b, 2 * p, :half, :] * (1.0 / l_ref[b, 2 * p, 0:1, :])
            hi = acc_ref[b, 2 * p + 1, half:, :] * (1.0 / l_ref[b, 2 * p + 1, 0:1, :])
            o = jnp.concatenate([lo, hi], axis=0).T
            gate = g_ref[b, :, MLA_WIDTH + p * LANES:MLA_WIDTH + (p + 1) * LANES].astype(F32)
            o_ref[b, :, sl] = (o * _silu(gate)).astype(BF16)


def _mla_kernel(q_ref, k_ref, v_ref, km_ref, vm_ref, g_ref, o_ref, acc_ref, m_ref, l_ref):
    i = pl.program_id(1)
    T = ATT_TILE
    nb = q_ref.shape[0]

    def real_kv(j):
        start = pl.multiple_of(j * T, T)
        return [(k_ref[b, pl.ds(start, T), :], v_ref[b, pl.ds(start, T), :]) for b in range(nb)]

    def first_chunk(extra):
        parts = [real_kv(i)] + ([real_kv(0)] if extra else []) + [[(km_ref[0], vm_ref[0])] * nb]
        kv = [tuple(jnp.concatenate([part[b][j] for part in parts], axis=0) for j in range(2)) for b in range(nb)]
        masks = ([_iota((T, T), 0) <= _iota((T, T), 1)] + ([_iota((T, T), 0) >= 0] if extra else [])
                 + [_iota((BLOCK, T), 0) >= N_PAD])
        _mla_chunk(q_ref, kv, jnp.concatenate(masks, axis=0), acc_ref, m_ref, l_ref, first=True)

    odd = i % 2

    @pl.when(odd == 1)
    def _():
        first_chunk(True)

    @pl.when(odd == 0)
    def _():
        first_chunk(False)

    def earlier(idx, carry):
        start = pl.multiple_of((odd + 2 * idx) * T, T)
        kv = [(k_ref[b, pl.ds(start, 2 * T), :], v_ref[b, pl.ds(start, 2 * T), :]) for b in range(nb)]
        _mla_chunk(q_ref, kv, None, acc_ref, m_ref, l_ref)
        return carry

    lax.fori_loop(0, i // 2, earlier, 0)
    _mla_finish(acc_ref, l_ref, g_ref, o_ref)


def _mla_attn(q, k, v, k_meta, v_meta, gate):
    B, L, _ = q.shape
    T, nb = ATT_TILE, ATT_BATCH
    qk_w = MLA_HEADS * LANES
    return pl.pallas_call(
        _mla_kernel,
        grid=(B // nb, L // T),
        in_specs=[pl.BlockSpec((nb, T, qk_w), lambda b, i: (b, i, 0)),
                  pl.BlockSpec((nb, L, qk_w), lambda b, i: (b, 0, 0)),
                  pl.BlockSpec((nb, L, MLA_WIDTH), lambda b, i: (b, 0, 0)),
                  pl.BlockSpec((1, BLOCK, qk_w), lambda b, i: (0, 0, 0)),
                  pl.BlockSpec((1, BLOCK, MLA_WIDTH), lambda b, i: (0, 0, 0)),
                  pl.BlockSpec((nb, T, 2 * MLA_WIDTH), lambda b, i: (b, i, 0))],
        out_specs=pl.BlockSpec((nb, T, MLA_WIDTH), lambda b, i: (b, i, 0)),
        out_shape=jax.ShapeDtypeStruct((B, L, MLA_WIDTH), BF16),
        scratch_shapes=[pltpu.VMEM((nb, MLA_HEADS, LANES, T), F32), pltpu.VMEM((nb, MLA_HEADS, 8, T), F32),
                        pltpu.VMEM((nb, MLA_HEADS, 8, T), F32)],
        compiler_params=_params(("parallel", "arbitrary")),
        name="mla_attn",
    )(q, k, v, k_meta, v_meta, gate)


def _mla_meta_kernel(q_ref, k_ref, v_ref, g_ref, o_ref, acc_ref, m_ref, l_ref):
    key, query = _iota((BLOCK, BLOCK), 0), _iota((BLOCK, BLOCK), 1)
    _mla_chunk(q_ref, [(k_ref[0], v_ref[0])], (key <= query) & (key >= N_PAD), acc_ref, m_ref, l_ref, first=True)
    _mla_finish(acc_ref, l_ref, g_ref, o_ref)


def _mla_meta(q, k, v, gate):
    qk_w = MLA_HEADS * LANES
    blk = lambda w: pl.BlockSpec((1, BLOCK, w), lambda i: (0, 0, 0))
    return pl.pallas_call(
        _mla_meta_kernel,
        grid=(1,),
        in_specs=[blk(qk_w), blk(qk_w), blk(MLA_WIDTH), blk(2 * MLA_WIDTH)],
        out_specs=blk(MLA_WIDTH),
        out_shape=jax.ShapeDtypeStruct((1, BLOCK, MLA_WIDTH), BF16),
        scratch_shapes=[pltpu.VMEM((1, MLA_HEADS, LANES, BLOCK), F32), pltpu.VMEM((1, MLA_HEADS, 8, BLOCK), F32),
                        pltpu.VMEM((1, MLA_HEADS, 8, BLOCK), F32)],
        compiler_params=_params(("arbitrary",)),
        name="mla_meta",
    )(q, k, v, gate)


def _mid_kernel(osb_ref, omla_ref, h_ref, wo_ref, g_ref, w_ref, h1_ref, q_ref, kv_ref, gate_ref):
    rows = h_ref.shape[1]
    halves = [slice(0, rows // 2), slice(rows // 2, rows)] if rows >= 2 * BLOCK else [slice(0, rows)]
    ys = [_dot(osb_ref[0, r, :], wo_ref[0:512, :]) + _dot(omla_ref[0, r, :], wo_ref[512:1024, :]) for r in halves]
    for r, y in zip(halves, ys):
        h1 = h_ref[0, r, :] + y
        h1_ref[0, r, :] = h1
        xn = _rms(h1, g_ref[...]).astype(BF16)
        q_ref[0, r, :] = (_dot(xn, w_ref[:, 0:1024]) * LOG2E).astype(BF16)
        kv_ref[0, r, :] = _dot(xn, w_ref[:, 1024:1280]).astype(BF16)
        gate_ref[0, r, :] = _dot(xn, w_ref[:, 1280:2304]).astype(BF16)


def _mid(osb, omla, h, tile, wo, g, w):
    B, L, _ = h.shape
    row = lambda width: pl.BlockSpec((1, tile, width), lambda b, t: (b, t, 0))
    full = lambda a: pl.BlockSpec(a.shape, lambda b, t: (0,) * a.ndim)
    return pl.pallas_call(
        _mid_kernel,
        grid=(B, L // tile),
        in_specs=[row(512), row(512), row(D_MODEL), full(wo), full(g), full(w)],
        out_specs=[row(D_MODEL), row(1024), row(256), row(1024)],
        out_shape=[jax.ShapeDtypeStruct((B, L, D_MODEL), F32),
                   jax.ShapeDtypeStruct((B, L, 1024), BF16),
                   jax.ShapeDtypeStruct((B, L, 256), BF16),
                   jax.ShapeDtypeStruct((B, L, 1024), BF16)],
        compiler_params=_params(("parallel", "parallel")),
        name="mid",
    )(osb, omla, h, wo, g, w)


def _swa_kernel(sink_ref, q_ref, kvp_ref, kvc_ref, kvm_ref, g_ref, bb_ref, mb_ref, h_ref, wo_ref, fg_ref,
                out_ref, o_scr):
    n = pl.program_id(1)
    first = jnp.where(n > 0, 0, 1)
    in_cur = _iota((BLOCK, BLOCK), 1) <= _iota((BLOCK, BLOCK), 0)
    blocks_before = (n + 1).astype(F32) * float(BLOCK)
    pairs = SWA_HEADS // 2
    items = [(b, g) for b in range(q_ref.shape[0]) for g in range(pairs // SWA_STACK)]
    k_sl, v_sl = slice(0, LANES), slice(LANES, 2 * LANES)
    heads_of = lambda g: [(p, hh) for p in range(g * SWA_STACK, (g + 1) * SWA_STACK) for hh in range(2)]

    def scores(b, g):
        q_stack = jnp.concatenate(
            [_split_pair(q_ref[b, :, p * LANES:(p + 1) * LANES])[hh] for p, hh in heads_of(g)], axis=0)
        k_band = jnp.concatenate([kvp_ref[b, :, k_sl], kvc_ref[b, :, k_sl]], axis=0)
        return _dot_nt(q_stack, k_band), _dot_nt(q_stack, kvm_ref[0, :, k_sl])

    def probs(g, z_stack, zm_stack):
        p_rows, invs = [], []
        for j, (p, hh) in enumerate(heads_of(g)):
            h = p + pairs * hh
            rows = slice(j * BLOCK, (j + 1) * BLOCK)
            z, z_meta = z_stack[rows], zm_stack[rows]
            slope = 2.0 ** (-8.0 * (h + 1.0) / SWA_HEADS) * LOG2E
            s_band = jnp.where(in_cur, z[:, BLOCK:], z[:, :BLOCK]) - bb_ref[first, h]
            s_meta = z_meta - mb_ref[h] - slope * blocks_before
            sink = sink_ref[h] * LOG2E
            m = jnp.maximum(jnp.max(jnp.maximum(s_band, s_meta), axis=-1, keepdims=True), sink)
            p_band, p_meta = jnp.exp2(s_band - m), jnp.exp2(s_meta - m)
            denom = jnp.sum(p_band + p_meta, axis=-1, keepdims=True) + jnp.exp2(sink - m)
            p_rows.append(jnp.concatenate(
                [jnp.where(in_cur, 0.0, p_band), jnp.where(in_cur, p_band, 0.0), p_meta], axis=1).astype(BF16))
            invs.append(1.0 / denom)
        return jnp.concatenate(p_rows, axis=0), invs

    def values(b, g, p_stack, invs):
        v_all = jnp.concatenate([kvp_ref[b, :, v_sl], kvc_ref[b, :, v_sl], kvm_ref[0, :, v_sl]], axis=0)
        o_stack = _dot(p_stack, v_all)
        for j, (p, hh) in enumerate(heads_of(g)):
            if hh == 1:
                sl = slice(p * LANES, (p + 1) * LANES)
                o = _merge_pair(o_stack[(j - 1) * BLOCK:j * BLOCK] * invs[j - 1],
                                o_stack[j * BLOCK:(j + 1) * BLOCK] * invs[j])
                o_scr[b, :, sl] = (o * _silu(g_ref[b, :, sl].astype(F32))).astype(BF16)

    zs, ps = {}, {}
    for step in range(len(items) + 2 * SKEW):
        if step < len(items):
            zs[step] = scores(*items[step])
        if 0 <= step - SKEW < len(items):
            ps[step - SKEW] = probs(items[step - SKEW][1], *zs.pop(step - SKEW))
        t = step - 2 * SKEW
        if 0 <= t < len(items):
            b, g = items[t]
            values(b, g, *ps.pop(t))
            if g == pairs // SWA_STACK - 1:
                out_ref[b] = _rms(h_ref[b] + _dot(o_scr[b], wo_ref[...]), fg_ref[...])


def _swa_bias_tables():
    row, col = np.arange(BLOCK)[:, None], np.arange(BLOCK)[None, :]
    slopes = (2.0 ** (-8.0 * (np.arange(SWA_HEADS) + 1.0) / SWA_HEADS) * LOG2E)[:, None, None]
    band = slopes * np.where(col <= row, row - col, row - col + BLOCK)
    band_first = band + np.where(col <= row, 0.0, -NEG)
    meta = np.where(col >= N_PAD, 0.0, -NEG) + slopes * (row - col)
    return jnp.asarray(np.stack([band, band_first]), F32), jnp.asarray(meta, F32)


def _swa_attn(sinks, q, kv, kv_meta, gate, h1, wo, final_g):
    B, L, _ = q.shape
    kvw = kv.shape[-1]
    nb = SWA_BATCH
    band_bias, meta_bias = _swa_bias_tables()
    row = lambda w: pl.BlockSpec((nb, BLOCK, w), lambda b, n, s: (b, n, 0))
    const = lambda a: pl.BlockSpec(a.shape, lambda b, n, s: (0,) * a.ndim)
    grid_spec = pltpu.PrefetchScalarGridSpec(
        num_scalar_prefetch=1,
        grid=(B // nb, L // BLOCK),
        in_specs=[row(SWA_WIDTH),
                  pl.BlockSpec((nb, BLOCK, kvw), lambda b, n, s: (b, jnp.maximum(n - 1, 0), 0)),
                  row(kvw), const(kv_meta), row(SWA_WIDTH), const(band_bias), const(meta_bias),
                  row(D_MODEL), const(wo), const(final_g)],
        out_specs=row(D_MODEL),
        scratch_shapes=[pltpu.VMEM((nb, BLOCK, SWA_WIDTH), BF16)],
    )
    return pl.pallas_call(
        _swa_kernel,
        grid_spec=grid_spec,
        out_shape=jax.ShapeDtypeStruct((B, L, D_MODEL), F32),
        compiler_params=_params(("parallel", "parallel")),
        name="swa_attn",
    )(sinks, q, kv, kv, kv_meta, gate, band_bias, meta_bias, h1, wo, final_g)


def _layer0_weights(w_in, w_uq, w_ukv):
    w_in, w_uq, w_ukv = w_in.astype(BF16), w_uq.astype(BF16), w_ukv.astype(BF16)
    q, k, v, g_sb = (w_in[:, i * 512:(i + 1) * 512] for i in range(4))
    c_q, c_kv = w_in[:, 2048:2304], w_in[:, 2304:2432]
    k_r, g_mla = w_in[:, 2432:2464], w_in[:, 2464:2976]
    half = MLA_ROPE // 2
    r1, r2 = k_r[:, :half], k_r[:, half:]
    z = lambda n: jnp.zeros((D_MODEL, n), w_in.dtype)
    kr_blk = jnp.concatenate([z(MLA_NOPE), r1, r2, z(LANES - MLA_NOPE - MLA_ROPE)], axis=1)
    w0 = jnp.concatenate([q * (SB_DIM ** -0.5), k, v, g_sb, g_mla, c_q, c_kv, kr_blk], axis=1)

    uq = w_uq.reshape(MLA_Q_LORA, MLA_HEADS, MLA_NOPE + MLA_ROPE)
    nope, u1, u2 = uq[..., :MLA_NOPE], uq[..., MLA_NOPE:MLA_NOPE + half], uq[..., MLA_NOPE + half:]
    zq = lambda n: jnp.zeros((MLA_Q_LORA, MLA_HEADS, n), w_uq.dtype)
    uq_main = jnp.concatenate([nope, u1, u2, zq(LANES - MLA_NOPE - MLA_ROPE)], axis=-1)
    uq_rot = jnp.concatenate([zq(MLA_NOPE), -u2, u1, zq(LANES - MLA_NOPE - MLA_ROPE)], axis=-1)
    wuq = jnp.concatenate([uq_main.reshape(MLA_Q_LORA, -1), uq_rot.reshape(MLA_Q_LORA, -1)], axis=1)

    ukv = w_ukv.reshape(MLA_KV_LORA, MLA_HEADS, MLA_NOPE + MLA_V)
    k_nope = jnp.concatenate([ukv[..., :MLA_NOPE],
                              jnp.zeros((MLA_KV_LORA, MLA_HEADS, LANES - MLA_NOPE), w_ukv.dtype)], axis=-1)
    wukv = jnp.concatenate([k_nope.reshape(MLA_KV_LORA, -1),
                            ukv[..., MLA_NOPE:].reshape(MLA_KV_LORA, -1)], axis=1)
    return w0, wuq, wukv


def _pair_heads(w, axis):
    shape = w.shape
    w = w.reshape(shape[:axis] + (SWA_KV_HEADS, SWA_HEADS // SWA_KV_HEADS, SWA_DIM) + shape[axis + 1:])
    return jnp.swapaxes(w, axis, axis + 1).reshape(shape)


def _layer1_weights(w_in, w_out):
    w_in, w_out = w_in.astype(BF16), w_out.astype(BF16)
    q, kv, g = w_in[:, :1024], w_in[:, 1024:1280], w_in[:, 1280:2304]
    w1 = jnp.concatenate([_pair_heads(q * (SWA_DIM ** -0.5), 1), kv, _pair_heads(g, 1)], axis=1)
    return w1, _pair_heads(w_out, 0)


def _rope_tables():
    half = MLA_ROPE // 2
    pos = np.arange(N_META + SEQ, dtype=np.float64)
    inv = ROPE_BASE ** (-np.arange(half, dtype=np.float64) / half)
    ang = pos[:, None] * inv[None, :]
    cos, sin = np.cos(ang), np.sin(ang)
    n = pos.shape[0]
    z = lambda w: np.zeros((n, w))
    tail = LANES - MLA_NOPE - MLA_ROPE
    c = np.concatenate([np.ones((n, MLA_NOPE)), cos, cos, z(tail)], axis=1)
    s = np.concatenate([z(MLA_NOPE), sin, sin, z(tail)], axis=1)
    s1 = np.concatenate([z(MLA_NOPE), -sin, z(half), z(tail)], axis=1)
    s2 = np.concatenate([z(MLA_NOPE), z(half), sin, z(tail)], axis=1)
    scale = (MLA_NOPE + MLA_ROPE) ** -0.5 * LOG2E
    tabs = (c * scale, s * scale, c, s1, s2)
    pad = lambda t: np.concatenate([np.zeros((N_PAD, LANES)), t[:N_META]], axis=0)
    return (tuple(jnp.asarray(pad(t), F32) for t in tabs), tuple(jnp.asarray(t[N_META:], F32) for t in tabs))


def _suffix_matrix(n):
    return jnp.asarray(np.where(np.arange(n)[:, None] >= np.arange(n)[None, :], -1.0, 0.0), BF16)


def kernel(x, meta, norm_g, final_g, ev_w_in, ev_q_norm_g, ev_kv_norm_g, ev_w_uq, ev_w_ukv,
           ev_w_out, od_w_in, od_sinks, od_w_out):
    w0, wuq, wukv = _layer0_weights(ev_w_in[0], ev_w_uq[0], ev_w_ukv[0])
    w1, wo1 = _layer1_weights(od_w_in[0], od_w_out[0])
    wo0 = ev_w_out[0].astype(BF16)
    tabs_meta, tabs_real = _rope_tables()
    u, u_meta = _suffix_matrix(ATT_TILE), _suffix_matrix(BLOCK)
    l0 = (norm_g[0:1], w0, ev_q_norm_g[0:1], ev_kv_norm_g[0:1], wuq, wukv)

    hm = jnp.concatenate([jnp.zeros((N_PAD, D_MODEL), x.dtype), meta.astype(x.dtype)], axis=0)[None]
    qsb_m, ksb_m, vsb_m, gate_m, qm_m, km_m, vm_m = _proj0(hm, BLOCK, *l0, tabs_meta)
    osb_m = _sb_meta(qsb_m, ksb_m, vsb_m, gate_m, u_meta)
    omla_m = _mla_meta(qm_m, km_m, vm_m, gate_m)
    _, _, kv1_m, _ = _mid(osb_m, omla_m, hm, BLOCK, wo0, norm_g[1:2], w1)

    q_sb, k_sb, v_sb, gate0, q_mla, k_mla, v_mla = _proj0(x, ROW_TILE, *l0, tabs_real)
    o_sb = _sb_attn(q_sb, k_sb, v_sb, ksb_m, vsb_m, gate0, u, u_meta)
    o_mla = _mla_attn(q_mla, k_mla, v_mla, km_m, vm_m, gate0)
    h1, q1, kv1, gate1 = _mid(o_sb, o_mla, x, ROW_TILE, wo0, norm_g[1:2], w1)
    return _swa_attn(od_sinks[0], q1, kv1, kv1_m, gate1, h1, wo1, final_g[None, :])
```

```python
import math

import numpy as np
import jax
import jax.numpy as jnp
from jax import lax
from jax.experimental import pallas as pl
from jax.experimental.pallas import tpu as pltpu

D_MODEL = 1024
SEQ = 2048
N_META = 16
BLOCK = 128
N_PAD = BLOCK - N_META
NORM_EPS = 1e-6
NEG = -1e30

SB_HEADS = 8
SB_DIM = 64
SB_WIDTH = SB_HEADS * SB_DIM
MLA_HEADS = 8
MLA_Q_LORA = 256
MLA_KV_LORA = 128
MLA_NOPE = 64
MLA_ROPE = 32
MLA_V = 64
MLA_WIDTH = MLA_HEADS * MLA_V
ROPE_BASE = 10000.0
SWA_HEADS = 16
SWA_KV_HEADS = 2
SWA_DIM = 64
SWA_WIDTH = SWA_HEADS * SWA_DIM

LANES = 128
ROW_TILE = 1024
ATT_TILE = 256
ATT_BATCH = 2
VMEM_LIMIT = 48 * 1024 * 1024
LOG2E = math.log2(math.e)
SWA_BATCH = 4
MLA_SKEW = 2
SWA_SKEW = 2
SKEW = 1
DEAD_CARRY = -256.0
SOFTPLUS_CLAMP = 64.0

BF16 = jnp.bfloat16
F32 = jnp.float32


def _dot(a, b):
    return jnp.dot(a, b, preferred_element_type=F32)


def _dot_nt(a, b):
    return lax.dot_general(a, b, (((1,), (1,)), ((), ())), preferred_element_type=F32)


def _rms(x, g):
    ms = jnp.mean(x * x, axis=-1, keepdims=True)
    return x * lax.rsqrt(ms + NORM_EPS) * g


def _silu(g):
    return g * (1.0 / (1.0 + jnp.exp(-g)))


def _params(sem):
    return pltpu.CompilerParams(dimension_semantics=sem, vmem_limit_bytes=VMEM_LIMIT)


def _iota(shape, dim):
    return lax.broadcasted_iota(jnp.int32, shape, dim)


def _split_pair(x):
    lane = _iota(x.shape, 1)
    zero = jnp.zeros_like(x)
    return jnp.where(lane < LANES // 2, x, zero), jnp.where(lane >= LANES // 2, x, zero)


def _merge_pair(lo, hi):
    lane = _iota(lo.shape, 1)
    return jnp.where(lane < LANES // 2, lo, hi)


def _proj0_kernel(x_ref, g_ref, w_ref, qg_ref, kvg_ref, wuq_ref, wukv_ref,
                  cq_ref, sq_ref, ck_ref, sk1_ref, sk2_ref,
                  qsb_ref, ksb_ref, vsb_ref, gate_ref, qm_ref, km_ref, vm_ref):
    rows = x_ref.shape[1]
    halves = [slice(0, rows // 2), slice(rows // 2, rows)] if rows >= 2 * BLOCK else [slice(0, rows)]
    qk_w = MLA_HEADS * LANES
    for r in halves:
        xn = _rms(x_ref[0, r, :], g_ref[...]).astype(BF16)
        qsb_ref[0, r, :] = (_dot(xn, w_ref[:, 0:512]) * LOG2E).astype(BF16)
        ksb_ref[0, r, :] = _dot(xn, w_ref[:, 512:1024]).astype(BF16)
        vsb_ref[0, r, :] = _dot(xn, w_ref[:, 1024:1536]).astype(BF16)
        gate_ref[0, r, :] = _dot(xn, w_ref[:, 1536:2560]).astype(BF16)
        lat = _dot(xn, w_ref[:, 2560:3072])
        cqn = _rms(lat[:, 0:256], qg_ref[...]).astype(BF16)
        ckvn = _rms(lat[:, 256:384], kvg_ref[...]).astype(BF16)
        kr = lat[:, 384:512]
        k_rope = (kr * ck_ref[r, :] + pltpu.roll(kr, LANES - MLA_ROPE // 2, 1) * sk1_ref[r, :]
                  + pltpu.roll(kr, MLA_ROPE // 2, 1) * sk2_ref[r, :])
        q_all = _dot(cqn, wuq_ref[...])
        kv_all = _dot(ckvn, wukv_ref[...])
        cq = cq_ref[r, :]
        sq = sq_ref[r, :]
        for h in range(MLA_HEADS):
            lo, hi = h * LANES, (h + 1) * LANES
            qm_ref[0, r, lo:hi] = (q_all[:, lo:hi] * cq + q_all[:, qk_w + lo:qk_w + hi] * sq).astype(BF16)
            km_ref[0, r, lo:hi] = (kv_all[:, lo:hi] + k_rope).astype(BF16)
        vm_ref[0, r, :] = kv_all[:, qk_w:].astype(BF16)


def _proj0(h, tile, g, w, qg, kvg, wuq, wukv, tables):
    B, L, _ = h.shape
    row = lambda width: pl.BlockSpec((1, tile, width), lambda b, t: (b, t, 0))
    full = lambda a: pl.BlockSpec(a.shape, lambda b, t: (0,) * a.ndim)
    tab = pl.BlockSpec((tile, LANES), lambda b, t: (t, 0))
    out = lambda width: jax.ShapeDtypeStruct((B, L, width), BF16)
    return pl.pallas_call(
        _proj0_kernel,
        grid=(B, L // tile),
        in_specs=[row(D_MODEL), full(g), full(w), full(qg), full(kvg), full(wuq), full(wukv),
                  tab, tab, tab, tab, tab],
        out_specs=[row(512), row(512), row(512), row(1024), row(1024), row(1024), row(512)],
        out_shape=[out(512), out(512), out(512), out(1024), out(1024), out(1024), out(512)],
        compiler_params=_params(("parallel", "parallel")),
        name="proj0",
    )(h, g, w, qg, kvg, wuq, wukv, *tables)


def _sb_chunk(qh_ref, kv, u, mask, acc_ref, c_ref, first=False):
    keys = kv[0][0].shape[0]
    items = [(b, h) for b in range(len(kv)) for h in range(SB_HEADS)]
    sl = lambda h: slice((h // 2) * LANES, (h // 2 + 1) * LANES)

    def scores(b, h):
        return _dot_nt(qh_ref[b, h], kv[b][0][:, sl(h)])

    def suffix(z):
        sp = jnp.maximum(jnp.log2(1.0 + jnp.exp2(jnp.minimum(z, SOFTPLUS_CLAMP))), z)
        if mask is not None:
            sp = jnp.where(mask, sp, 0.0)
        return _dot(sp.astype(BF16), u)

    def weights(b, h, z, sfx):
        total = jnp.broadcast_to(sfx[:, 0:1], c_ref.shape[2:])
        if first:
            a = jnp.exp2(z + sfx)
        else:
            c = c_ref[b, h]
            a = jnp.exp2(z + sfx + jnp.concatenate([c] * (keys // LANES), axis=1))
        if mask is not None:
            a = jnp.where(mask, a, 0.0)
        av = _dot(a.astype(BF16), kv[b][1][:, sl(h)])
        if first:
            acc_ref[b, h], c_ref[b, h] = av, total
        else:
            acc_ref[b, h] += av
            c_ref[b, h] = c + total

    zs, sfxs = {}, {}
    for step in range(len(items) + 2 * SKEW):
        if step < len(items):
            zs[step] = scores(*items[step])
        if 0 <= step - SKEW < len(items):
            sfxs[step - SKEW] = suffix(zs[step - SKEW])
        if 0 <= step - 2 * SKEW < len(items):
            weights(*items[step - 2 * SKEW], zs.pop(step - 2 * SKEW), sfxs.pop(step - 2 * SKEW))


def _sb_prologue(q_ref, qh_ref):
    for b in range(q_ref.shape[0]):
        for p in range(SB_HEADS // 2):
            qh_ref[b, 2 * p], qh_ref[b, 2 * p + 1] = _split_pair(q_ref[b, :, p * LANES:(p + 1) * LANES])


def _sb_epilogue(acc_ref, g_ref, o_ref):
    for b in range(o_ref.shape[0]):
        for p in range(SB_HEADS // 2):
            sl = slice(p * LANES, (p + 1) * LANES)
            o = _merge_pair(acc_ref[b, 2 * p], acc_ref[b, 2 * p + 1])
            o_ref[b, :, sl] = (o * _silu(g_ref[b, :, sl].astype(F32))).astype(BF16)


def _sb_kernel(q_ref, k_ref, v_ref, km_ref, vm_ref, g_ref, u_ref, um_ref, o_ref, qh_ref, acc_ref, c_ref):
    i = pl.program_id(1)
    T = ATT_TILE
    nb = q_ref.shape[0]
    _sb_prologue(q_ref, qh_ref)

    def real_chunk(j, mask, first=False):
        start = pl.multiple_of(j * T, T)
        kv = [(k_ref[b, pl.ds(start, T), :], v_ref[b, pl.ds(start, T), :]) for b in range(nb)]
        _sb_chunk(qh_ref, kv, u_ref[...], mask, acc_ref, c_ref, first)

    real_chunk(i, _iota((T, T), 1) < _iota((T, T), 0), first=True)

    def alive():
        return jnp.max(c_ref[...]) > DEAD_CARRY

    def earlier(state):
        idx, _ = state
        real_chunk(i - 1 - idx, None)
        return idx + 1, alive()

    _, still_alive = lax.while_loop(lambda state: (state[0] < i) & state[1], earlier, (jnp.int32(0), alive()))

    @pl.when(still_alive)
    def _():
        _sb_chunk(qh_ref, [(km_ref[0], vm_ref[0])] * nb, um_ref[...], _iota((T, BLOCK), 1) >= N_PAD,
                  acc_ref, c_ref)

    _sb_epilogue(acc_ref, g_ref, o_ref)


def _sb_attn(q, k, v, k_meta, v_meta, gate, u, u_meta):
    B, L, _ = q.shape
    T, nb = ATT_TILE, ATT_BATCH
    tile = pl.BlockSpec((nb, T, SB_WIDTH), lambda b, i: (b, i, 0))
    seq = pl.BlockSpec((nb, L, SB_WIDTH), lambda b, i: (b, 0, 0))
    meta = pl.BlockSpec((1, BLOCK, SB_WIDTH), lambda b, i: (0, 0, 0))
    const = lambda a: pl.BlockSpec(a.shape, lambda b, i: (0, 0))
    return pl.pallas_call(
        _sb_kernel,
        grid=(B // nb, L // T),
        in_specs=[tile, seq, seq, meta, meta, tile, const(u), const(u_meta)],
        out_specs=tile,
        out_shape=jax.ShapeDtypeStruct((B, L, SB_WIDTH), BF16),
        scratch_shapes=[pltpu.VMEM((nb, SB_HEADS, T, LANES), BF16), pltpu.VMEM((nb, SB_HEADS, T, LANES), F32),
                        pltpu.VMEM((nb, SB_HEADS, T, LANES), F32)],
        compiler_params=_params(("parallel", "arbitrary")),
        name="sb_attn",
    )(q, k, v, k_meta, v_meta, gate, u, u_meta)


def _sb_meta_kernel(q_ref, k_ref, v_ref, g_ref, u_ref, o_ref, qh_ref, acc_ref, c_ref):
    _sb_prologue(q_ref, qh_ref)
    row, col = _iota((BLOCK, BLOCK), 0), _iota((BLOCK, BLOCK), 1)
    _sb_chunk(qh_ref, [(k_ref[0], v_ref[0])], u_ref[...], (col < row) & (col >= N_PAD), acc_ref, c_ref, first=True)
    _sb_epilogue(acc_ref, g_ref, o_ref)


def _sb_meta(q, k, v, gate, u_meta):
    blk = pl.BlockSpec((1, BLOCK, SB_WIDTH), lambda i: (0, 0, 0))
    return pl.pallas_call(
        _sb_meta_kernel,
        grid=(1,),
        in_specs=[blk, blk, blk, blk, pl.BlockSpec(u_meta.shape, lambda i: (0, 0))],
        out_specs=blk,
        out_shape=jax.ShapeDtypeStruct((1, BLOCK, SB_WIDTH), BF16),
        scratch_shapes=[pltpu.VMEM((1, SB_HEADS, BLOCK, LANES), BF16), pltpu.VMEM((1, SB_HEADS, BLOCK, LANES), F32),
                        pltpu.VMEM((1, SB_HEADS, BLOCK, LANES), F32)],
        compiler_params=_params(("arbitrary",)),
        name="sb_meta",
    )(q, k, v, gate, u_meta)


def _mla_chunk(q_ref, kv, mask, acc_ref, m_ref, l_ref, first=False):
    items = [(b, h) for b in range(len(kv)) for h in range(MLA_HEADS)]
    rep = lambda row: jnp.broadcast_to(row, m_ref.shape[2:])

    def scores(b, h):
        hs = slice(h * LANES, (h + 1) * LANES)
        s = _dot_nt(kv[b][0][:, hs], q_ref[b, :, hs])
        return s if mask is None else jnp.where(mask, s, NEG)

    def probs(b, h, s):
        m_new = jnp.max(s, axis=0, keepdims=True)
        if first:
            alpha = None
        else:
            m_old = m_ref[b, h, 0:1, :]
            m_new = jnp.maximum(m_old, m_new)
            alpha = jnp.exp2(m_old - m_new)
        m_ref[b, h] = rep(m_new)
        p = jnp.exp2(s - m_new)
        return p.astype(BF16), jnp.sum(p, axis=0, keepdims=True), alpha

    def update(b, h, p, l_new, alpha):
        vs = slice((h // 2) * LANES, (h // 2 + 1) * LANES)
        pv = lax.dot_general(kv[b][1][:, vs], p, (((0,), (0,)), ((), ())), preferred_element_type=F32)
        if first:
            acc_ref[b, h], l_ref[b, h] = pv, rep(l_new)
        else:
            acc_ref[b, h] = alpha * acc_ref[b, h] + pv
            l_ref[b, h] = rep(alpha * l_ref[b, h, 0:1, :] + l_new)

    ss, ps = {}, {}
    for step in range(len(items) + 2 * MLA_SKEW):
        if step < len(items):
            ss[step] = scores(*items[step])
        if 0 <= step - MLA_SKEW < len(items):
            ps[step - MLA_SKEW] = probs(*items[step - MLA_SKEW], ss.pop(step - MLA_SKEW))
        if 0 <= step - 2 * MLA_SKEW < len(items):
            update(*items[step - 2 * MLA_SKEW], *ps.pop(step - 2 * MLA_SKEW))


def _mla_finish(acc_ref, l_ref, g_ref, o_ref):
    half = LANES // 2
    for b in range(o_ref.shape[0]):
        for p in range(MLA_HEADS // 2):
            sl = slice(p * LANES, (p + 1) * LANES)
            lo = acc_ref[b, 2 * p, :half, :] * (1.0 / l_ref[b, 2 * p, 0:1, :])
            hi = acc_ref[b, 2 * p + 1, half:, :] * (1.0 / l_ref[b, 2 * p + 1, 0:1, :])
            o = jnp.concatenate([lo, hi], axis=0).T
            gate = g_ref[b, :, MLA_WIDTH + p * LANES:MLA_WIDTH + (p + 1) * LANES].astype(F32)
            o_ref[b, :, sl] = (o * _silu(gate)).astype(BF16)


def _mla_kernel(q_ref, k_ref, v_ref, km_ref, vm_ref, g_ref, o_ref, acc_ref, m_ref, l_ref):
    i = pl.program_id(1)
    T = ATT_TILE
    nb = q_ref.shape[0]

    def real_kv(j):
        start = pl.multiple_of(j * T, T)
        return [(k_ref[b, pl.ds(start, T), :], v_ref[b, pl.ds(start, T), :]) for b in range(nb)]

    def first_chunk(extra):
        parts = [real_kv(i)] + ([real_kv(0)] if extra else []) + [[(km_ref[0], vm_ref[0])] * nb]
        kv = [tuple(jnp.concatenate([part[b][j] for part in parts], axis=0) for j in range(2)) for b in range(nb)]
        masks = ([_iota((T, T), 0) <= _iota((T, T), 1)] + ([_iota((T, T), 0) >= 0] if extra else [])
                 + [_iota((BLOCK, T), 0) >= N_PAD])
        _mla_chunk(q_ref, kv, jnp.concatenate(masks, axis=0), acc_ref, m_ref, l_ref, first=True)

    odd = i % 2

    @pl.when(odd == 1)
    def _():
        first_chunk(True)

    @pl.when(odd == 0)
    def _():
        first_chunk(False)

    def earlier(idx, carry):
        start = pl.multiple_of((odd + 2 * idx) * T, T)
        kv = [(k_ref[b, pl.ds(start, 2 * T), :], v_ref[b, pl.ds(start, 2 * T), :]) for b in range(nb)]
        _mla_chunk(q_ref, kv, None, acc_ref, m_ref, l_ref)
        return carry

    lax.fori_loop(0, i // 2, earlier, 0)
    _mla_finish(acc_ref, l_ref, g_ref, o_ref)


def _mla_attn(q, k, v, k_meta, v_meta, gate):
    B, L, _ = q.shape
    T, nb = ATT_TILE, ATT_BATCH
    qk_w = MLA_HEADS * LANES
    return pl.pallas_call(
        _mla_kernel,
        grid=(B // nb, L // T),
        in_specs=[pl.BlockSpec((nb, T, qk_w), lambda b, i: (b, i, 0)),
                  pl.BlockSpec((nb, L, qk_w), lambda b, i: (b, 0, 0)),
                  pl.BlockSpec((nb, L, MLA_WIDTH), lambda b, i: (b, 0, 0)),
                  pl.BlockSpec((1, BLOCK, qk_w), lambda b, i: (0, 0, 0)),
                  pl.BlockSpec((1, BLOCK, MLA_WIDTH), lambda b, i: (0, 0, 0)),
                  pl.BlockSpec((nb, T, 2 * MLA_WIDTH), lambda b, i: (b, i, 0))],
        out_specs=pl.BlockSpec((nb, T, MLA_WIDTH), lambda b, i: (b, i, 0)),
        out_shape=jax.ShapeDtypeStruct((B, L, MLA_WIDTH), BF16),
        scratch_shapes=[pltpu.VMEM((nb, MLA_HEADS, LANES, T), F32), pltpu.VMEM((nb, MLA_HEADS, 8, T), F32),
                        pltpu.VMEM((nb, MLA_HEADS, 8, T), F32)],
        compiler_params=_params(("parallel", "arbitrary")),
        name="mla_attn",
    )(q, k, v, k_meta, v_meta, gate)


def _mla_meta_kernel(q_ref, k_ref, v_ref, g_ref, o_ref, acc_ref, m_ref, l_ref):
    key, query = _iota((BLOCK, BLOCK), 0), _iota((BLOCK, BLOCK), 1)
    _mla_chunk(q_ref, [(k_ref[0], v_ref[0])], (key <= query) & (key >= N_PAD), acc_ref, m_ref, l_ref, first=True)
    _mla_finish(acc_ref, l_ref, g_ref, o_ref)


def _mla_meta(q, k, v, gate):
    qk_w = MLA_HEADS * LANES
    blk = lambda w: pl.BlockSpec((1, BLOCK, w), lambda i: (0, 0, 0))
    return pl.pallas_call(
        _mla_meta_kernel,
        grid=(1,),
        in_specs=[blk(qk_w), blk(qk_w), blk(MLA_WIDTH), blk(2 * MLA_WIDTH)],
        out_specs=blk(MLA_WIDTH),
        out_shape=jax.ShapeDtypeStruct((1, BLOCK, MLA_WIDTH), BF16),
        scratch_shapes=[pltpu.VMEM((1, MLA_HEADS, LANES, BLOCK), F32), pltpu.VMEM((1, MLA_HEADS, 8, BLOCK), F32),
                        pltpu.VMEM((1, MLA_HEADS, 8, BLOCK), F32)],
        compiler_params=_params(("arbitrary",)),
        name="mla_meta",
    )(q, k, v, gate)


def _mid_kernel(osb_ref, omla_ref, h_ref, wo_ref, g_ref, w_ref, h1_ref, q_ref, kv_ref, gate_ref):
    rows = h_ref.shape[1]
    halves = [slice(0, rows // 2), slice(rows // 2, rows)] if rows >= 2 * BLOCK else [slice(0, rows)]
    ys = [_dot(osb_ref[0, r, :], wo_ref[0:512, :]) + _dot(omla_ref[0, r, :], wo_ref[512:1024, :]) for r in halves]
    for r, y in zip(halves, ys):
        h1 = h_ref[0, r, :] + y
        h1_ref[0, r, :] = h1
        xn = _rms(h1, g_ref[...]).astype(BF16)
        q_ref[0, r, :] = (_dot(xn, w_ref[:, 0:1024]) * LOG2E).astype(BF16)
        kv_ref[0, r, :] = _dot(xn, w_ref[:, 1024:1280]).astype(BF16)
        gate_ref[0, r, :] = _dot(xn, w_ref[:, 1280:2304]).astype(BF16)


def _mid(osb, omla, h, tile, wo, g, w):
    B, L, _ = h.shape
    row = lambda width: pl.BlockSpec((1, tile, width), lambda b, t: (b, t, 0))
    full = lambda a: pl.BlockSpec(a.shape, lambda b, t: (0,) * a.ndim)
    return pl.pallas_call(
        _mid_kernel,
        grid=(B, L // tile),
        in_specs=[row(512), row(512), row(D_MODEL), full(wo), full(g), full(w)],
        out_specs=[row(D_MODEL), row(1024), row(256), row(1024)],
        out_shape=[jax.ShapeDtypeStruct((B, L, D_MODEL), F32),
                   jax.ShapeDtypeStruct((B, L, 1024), BF16),
                   jax.ShapeDtypeStruct((B, L, 256), BF16),
                   jax.ShapeDtypeStruct((B, L, 1024), BF16)],
        compiler_params=_params(("parallel", "parallel")),
        name="mid",
    )(osb, omla, h, wo, g, w)


def _swa_kernel(sink_ref, q_ref, kvp_ref, kvc_ref, kvm_ref, g_ref, bb_ref, mb_ref, h_ref, wo_ref, fg_ref,
                out_ref, o_scr):
    n = pl.program_id(1)
    first = jnp.where(n > 0, 0, 1)
    in_cur = _iota((BLOCK, BLOCK), 0) <= _iota((BLOCK, BLOCK), 1)
    blocks_before = (n + 1).astype(F32) * float(BLOCK)
    pairs = SWA_HEADS // 2
    half = LANES // 2
    items = [(b, p) for b in range(q_ref.shape[0]) for p in range(pairs)]
    k_sl, v_sl = slice(0, LANES), slice(LANES, 2 * LANES)
    keys_of = lambda b, sl: jnp.concatenate([kvp_ref[b, :, sl], kvc_ref[b, :, sl], kvm_ref[0, :, sl]], axis=0)

    def scores(b, p):
        q_stack = jnp.concatenate(_split_pair(q_ref[b, :, p * LANES:(p + 1) * LANES]), axis=0)
        return _dot_nt(keys_of(b, k_sl), q_stack)

    def probs(p, z_pair):
        p_cols, invs = [], []
        for hh in range(2):
            h = p + pairs * hh
            z = z_pair[:, hh * BLOCK:(hh + 1) * BLOCK]
            slope = 2.0 ** (-8.0 * (h + 1.0) / SWA_HEADS) * LOG2E
            s_band = jnp.where(in_cur, z[BLOCK:2 * BLOCK], z[:BLOCK]) - bb_ref[first, h]
            s_meta = z[2 * BLOCK:] - mb_ref[h] - slope * blocks_before
            sink = sink_ref[h] * LOG2E
            m = jnp.maximum(jnp.max(jnp.maximum(s_band, s_meta), axis=0, keepdims=True), sink)
            p_band, p_meta = jnp.exp2(s_band - m), jnp.exp2(s_meta - m)
            denom = jnp.sum(p_band + p_meta, axis=0, keepdims=True) + jnp.exp2(sink - m)
            p_cols.append(jnp.concatenate(
                [jnp.where(in_cur, 0.0, p_band), jnp.where(in_cur, p_band, 0.0), p_meta], axis=0).astype(BF16))
            invs.append(1.0 / denom)
        return jnp.concatenate(p_cols, axis=1), jnp.concatenate(invs, axis=1)

    def values(b, p, p_pair, inv):
        o_t = lax.dot_general(keys_of(b, v_sl), p_pair, (((0,), (0,)), ((), ())), preferred_element_type=F32) * inv
        o = jnp.concatenate([o_t[:half, :BLOCK], o_t[half:, BLOCK:]], axis=0).T
        sl = slice(p * LANES, (p + 1) * LANES)
        o_scr[b, :, sl] = (o * _silu(g_ref[b, :, sl].astype(F32))).astype(BF16)

    zs, ps = {}, {}
    for step in range(len(items) + 2 * SWA_SKEW):
        if step < len(items):
            zs[step] = scores(*items[step])
        if 0 <= step - SWA_SKEW < len(items):
            ps[step - SWA_SKEW] = probs(items[step - SWA_SKEW][1], zs.pop(step - SWA_SKEW))
        t = step - 2 * SWA_SKEW
        if 0 <= t < len(items):
            b, p = items[t]
            values(b, p, *ps.pop(t))
            if p == pairs - 1:
                out_ref[b] = _rms(h_ref[b] + _dot(o_scr[b], wo_ref[...]), fg_ref[...])


def _swa_bias_tables():
    col, row = np.arange(BLOCK)[:, None], np.arange(BLOCK)[None, :]
    slopes = (2.0 ** (-8.0 * (np.arange(SWA_HEADS) + 1.0) / SWA_HEADS) * LOG2E)[:, None, None]
    band = slopes * np.where(col <= row, row - col, row - col + BLOCK)
    band_first = band + np.where(col <= row, 0.0, -NEG)
    meta = np.where(col >= N_PAD, 0.0, -NEG) + slopes * (row - col)
    return jnp.asarray(np.stack([band, band_first]), F32), jnp.asarray(meta, F32)


def _swa_attn(sinks, q, kv, kv_meta, gate, h1, wo, final_g):
    B, L, _ = q.shape
    kvw = kv.shape[-1]
    nb = SWA_BATCH
    band_bias, meta_bias = _swa_bias_tables()
    row = lambda w: pl.BlockSpec((nb, BLOCK, w), lambda b, n, s: (b, n, 0))
    const = lambda a: pl.BlockSpec(a.shape, lambda b, n, s: (0,) * a.ndim)
    grid_spec = pltpu.PrefetchScalarGridSpec(
        num_scalar_prefetch=1,
        grid=(B // nb, L // BLOCK),
        in_specs=[row(SWA_WIDTH),
                  pl.BlockSpec((nb, BLOCK, kvw), lambda b, n, s: (b, jnp.maximum(n - 1, 0), 0)),
                  row(kvw), const(kv_meta), row(SWA_WIDTH), const(band_bias), const(meta_bias),
                  row(D_MODEL), const(wo), const(final_g)],
        out_specs=row(D_MODEL),
        scratch_shapes=[pltpu.VMEM((nb, BLOCK, SWA_WIDTH), BF16)],
    )
    return pl.pallas_call(
        _swa_kernel,
        grid_spec=grid_spec,
        out_shape=jax.ShapeDtypeStruct((B, L, D_MODEL), F32),
        compiler_params=_params(("parallel", "parallel")),
        name="swa_attn",
    )(sinks, q, kv, kv, kv_meta, gate, band_bias, meta_bias, h1, wo, final_g)


def _layer0_weights(w_in, w_uq, w_ukv):
    w_in, w_uq, w_ukv = w_in.astype(BF16), w_uq.astype(BF16), w_ukv.astype(BF16)
    q, k, v, g_sb = (w_in[:, i * 512:(i + 1) * 512] for i in range(4))
    c_q, c_kv = w_in[:, 2048:2304], w_in[:, 2304:2432]
    k_r, g_mla = w_in[:, 2432:2464], w_in[:, 2464:2976]
    half = MLA_ROPE // 2
    r1, r2 = k_r[:, :half], k_r[:, half:]
    z = lambda n: jnp.zeros((D_MODEL, n), w_in.dtype)
    kr_blk = jnp.concatenate([z(MLA_NOPE), r1, r2, z(LANES - MLA_NOPE - MLA_ROPE)], axis=1)
    w0 = jnp.concatenate([q * (SB_DIM ** -0.5), k, v, g_sb, g_mla, c_q, c_kv, kr_blk], axis=1)

    uq = w_uq.reshape(MLA_Q_LORA, MLA_HEADS, MLA_NOPE + MLA_ROPE)
    nope, u1, u2 = uq[..., :MLA_NOPE], uq[..., MLA_NOPE:MLA_NOPE + half], uq[..., MLA_NOPE + half:]
    zq = lambda n: jnp.zeros((MLA_Q_LORA, MLA_HEADS, n), w_uq.dtype)
    uq_main = jnp.concatenate([nope, u1, u2, zq(LANES - MLA_NOPE - MLA_ROPE)], axis=-1)
    uq_rot = jnp.concatenate([zq(MLA_NOPE), -u2, u1, zq(LANES - MLA_NOPE - MLA_ROPE)], axis=-1)
    wuq = jnp.concatenate([uq_main.reshape(MLA_Q_LORA, -1), uq_rot.reshape(MLA_Q_LORA, -1)], axis=1)

    ukv = w_ukv.reshape(MLA_KV_LORA, MLA_HEADS, MLA_NOPE + MLA_V)
    k_nope = jnp.concatenate([ukv[..., :MLA_NOPE],
                              jnp.zeros((MLA_KV_LORA, MLA_HEADS, LANES - MLA_NOPE), w_ukv.dtype)], axis=-1)
    wukv = jnp.concatenate([k_nope.reshape(MLA_KV_LORA, -1),
                            ukv[..., MLA_NOPE:].reshape(MLA_KV_LORA, -1)], axis=1)
    return w0, wuq, wukv


def _pair_heads(w, axis):
    shape = w.shape
    w = w.reshape(shape[:axis] + (SWA_KV_HEADS, SWA_HEADS // SWA_KV_HEADS, SWA_DIM) + shape[axis + 1:])
    return jnp.swapaxes(w, axis, axis + 1).reshape(shape)


def _layer1_weights(w_in, w_out):
    w_in, w_out = w_in.astype(BF16), w_out.astype(BF16)
    q, kv, g = w_in[:, :1024], w_in[:, 1024:1280], w_in[:, 1280:2304]
    w1 = jnp.concatenate([_pair_heads(q * (SWA_DIM ** -0.5), 1), kv, _pair_heads(g, 1)], axis=1)
    return w1, _pair_heads(w_out, 0)


def _rope_tables():
    half = MLA_ROPE // 2
    pos = np.arange(N_META + SEQ, dtype=np.float64)
    inv = ROPE_BASE ** (-np.arange(half, dtype=np.float64) / half)
    ang = pos[:, None] * inv[None, :]
    cos, sin = np.cos(ang), np.sin(ang)
    n = pos.shape[0]
    z = lambda w: np.zeros((n, w))
    tail = LANES - MLA_NOPE - MLA_ROPE
    c = np.concatenate([np.ones((n, MLA_NOPE)), cos, cos, z(tail)], axis=1)
    s = np.concatenate([z(MLA_NOPE), sin, sin, z(tail)], axis=1)
    s1 = np.concatenate([z(MLA_NOPE), -sin, z(half), z(tail)], axis=1)
    s2 = np.concatenate([z(MLA_NOPE), z(half), sin, z(tail)], axis=1)
    scale = (MLA_NOPE + MLA_ROPE) ** -0.5 * LOG2E
    tabs = (c * scale, s * scale, c, s1, s2)
    pad = lambda t: np.concatenate([np.zeros((N_PAD, LANES)), t[:N_META]], axis=0)
    return (tuple(jnp.asarray(pad(t), F32) for t in tabs), tuple(jnp.asarray(t[N_META:], F32) for t in tabs))


def _suffix_matrix(n):
    return jnp.asarray(np.where(np.arange(n)[:, None] >= np.arange(n)[None, :], -1.0, 0.0), BF16)


def kernel(x, meta, norm_g, final_g, ev_w_in, ev_q_norm_g, ev_kv_norm_g, ev_w_uq, ev_w_ukv,
           ev_w_out, od_w_in, od_sinks, od_w_out):
    w0, wuq, wukv = _layer0_weights(ev_w_in[0], ev_w_uq[0], ev_w_ukv[0])
    w1, wo1 = _layer1_weights(od_w_in[0], od_w_out[0])
    wo0 = ev_w_out[0].astype(BF16)
    tabs_meta, tabs_real = _rope_tables()
    u, u_meta = _suffix_matrix(ATT_TILE), _suffix_matrix(BLOCK)
    l0 = (norm_g[0:1], w0, ev_q_norm_g[0:1], ev_kv_norm_g[0:1], wuq, wukv)

    hm = jnp.concatenate([jnp.zeros((N_PAD, D_MODEL), x.dtype), meta.astype(x.dtype)], axis=0)[None]
    qsb_m, ksb_m, vsb_m, gate_m, qm_m, km_m, vm_m = _proj0(hm, BLOCK, *l0, tabs_meta)
    osb_m = _sb_meta(qsb_m, ksb_m, vsb_m, gate_m, u_meta)
    omla_m = _mla_meta(qm_m, km_m, vm_m, gate_m)
    _, _, kv1_m, _ = _mid(osb_m, omla_m, hm, BLOCK, wo0, norm_g[1:2], w1)

    q_sb, k_sb, v_sb, gate0, q_mla, k_mla, v_mla = _proj0(x, ROW_TILE, *l0, tabs_real)
    o_sb = _sb_attn(q_sb, k_sb, v_sb, ksb_m, vsb_m, gate0, u, u_meta)
    o_mla = _mla_attn(q_mla, k_mla, v_mla, km_m, vm_m, gate0)
    h1, q1, kv1, gate1 = _mid(o_sb, o_mla, x, ROW_TILE, wo0, norm_g[1:2], w1)
    return _swa_attn(od_sinks[0], q1, kv1, kv1_m, gate1, h1, wo1, final_g[None, :])
```

```python
import math

import numpy as np
import jax
import jax.numpy as jnp
from jax import lax
from jax.experimental import pallas as pl
from jax.experimental.pallas import tpu as pltpu

D_MODEL = 1024
SEQ = 2048
N_META = 16
BLOCK = 128
N_PAD = BLOCK - N_META
NORM_EPS = 1e-6
NEG = -1e30

SB_HEADS = 8
SB_DIM = 64
SB_WIDTH = SB_HEADS * SB_DIM
MLA_HEADS = 8
MLA_Q_LORA = 256
MLA_KV_LORA = 128
MLA_NOPE = 64
MLA_ROPE = 32
MLA_V = 64
MLA_WIDTH = MLA_HEADS * MLA_V
ROPE_BASE = 10000.0
SWA_HEADS = 16
SWA_KV_HEADS = 2
SWA_DIM = 64
SWA_WIDTH = SWA_HEADS * SWA_DIM

LANES = 128
ROW_TILE = 1024
ATT_TILE = 256
ATT_BATCH = 2
VMEM_LIMIT = 48 * 1024 * 1024
LOG2E = math.log2(math.e)
SWA_BATCH = 4
MLA_SKEW = 2
SWA_SKEW = 2
SKEW = 1
SOFTPLUS_CLAMP = 64.0

BF16 = jnp.bfloat16
F32 = jnp.float32


def _dot(a, b):
    return jnp.dot(a, b, preferred_element_type=F32)


def _dot_nt(a, b):
    return lax.dot_general(a, b, (((1,), (1,)), ((), ())), preferred_element_type=F32)


def _rms(x, g):
    ms = jnp.mean(x * x, axis=-1, keepdims=True)
    return x * lax.rsqrt(ms + NORM_EPS) * g


def _silu(g):
    return g * (1.0 / (1.0 + jnp.exp(-g)))


def _params(sem):
    return pltpu.CompilerParams(dimension_semantics=sem, vmem_limit_bytes=VMEM_LIMIT)


def _iota(shape, dim):
    return lax.broadcasted_iota(jnp.int32, shape, dim)


def _split_pair(x):
    lane = _iota(x.shape, 1)
    zero = jnp.zeros_like(x)
    return jnp.where(lane < LANES // 2, x, zero), jnp.where(lane >= LANES // 2, x, zero)


def _merge_pair(lo, hi):
    lane = _iota(lo.shape, 1)
    return jnp.where(lane < LANES // 2, lo, hi)


def _proj0_kernel(x_ref, g_ref, w_ref, qg_ref, kvg_ref, wuq_ref, wukv_ref,
                  cq_ref, sq_ref, ck_ref, sk1_ref, sk2_ref,
                  qsb_ref, ksb_ref, vsb_ref, gate_ref, qm_ref, km_ref, vm_ref):
    rows = x_ref.shape[1]
    halves = [slice(0, rows // 2), slice(rows // 2, rows)] if rows >= 2 * BLOCK else [slice(0, rows)]
    qk_w = MLA_HEADS * LANES
    for r in halves:
        xn = _rms(x_ref[0, r, :], g_ref[...]).astype(BF16)
        qsb_ref[0, r, :] = (_dot(xn, w_ref[:, 0:512]) * LOG2E).astype(BF16)
        ksb_ref[0, r, :] = _dot(xn, w_ref[:, 512:1024]).astype(BF16)
        vsb_ref[0, r, :] = _dot(xn, w_ref[:, 1024:1536]).astype(BF16)
        gate_ref[0, r, :] = _dot(xn, w_ref[:, 1536:2560]).astype(BF16)
        lat = _dot(xn, w_ref[:, 2560:3072])
        cqn = _rms(lat[:, 0:256], qg_ref[...]).astype(BF16)
        ckvn = _rms(lat[:, 256:384], kvg_ref[...]).astype(BF16)
        kr = lat[:, 384:512]
        k_rope = (kr * ck_ref[r, :] + pltpu.roll(kr, LANES - MLA_ROPE // 2, 1) * sk1_ref[r, :]
                  + pltpu.roll(kr, MLA_ROPE // 2, 1) * sk2_ref[r, :])
        q_all = _dot(cqn, wuq_ref[...])
        kv_all = _dot(ckvn, wukv_ref[...])
        cq = cq_ref[r, :]
        sq = sq_ref[r, :]
        for h in range(MLA_HEADS):
            lo, hi = h * LANES, (h + 1) * LANES
            qm_ref[0, r, lo:hi] = (q_all[:, lo:hi] * cq + q_all[:, qk_w + lo:qk_w + hi] * sq).astype(BF16)
            km_ref[0, r, lo:hi] = (kv_all[:, lo:hi] + k_rope).astype(BF16)
        vm_ref[0, r, :] = kv_all[:, qk_w:].astype(BF16)


def _proj0(h, tile, g, w, qg, kvg, wuq, wukv, tables):
    B, L, _ = h.shape
    row = lambda width: pl.BlockSpec((1, tile, width), lambda b, t: (b, t, 0))
    full = lambda a: pl.BlockSpec(a.shape, lambda b, t: (0,) * a.ndim)
    tab = pl.BlockSpec((tile, LANES), lambda b, t: (t, 0))
    out = lambda width: jax.ShapeDtypeStruct((B, L, width), BF16)
    return pl.pallas_call(
        _proj0_kernel,
        grid=(B, L // tile),
        in_specs=[row(D_MODEL), full(g), full(w), full(qg), full(kvg), full(wuq), full(wukv),
                  tab, tab, tab, tab, tab],
        out_specs=[row(512), row(512), row(512), row(1024), row(1024), row(1024), row(512)],
        out_shape=[out(512), out(512), out(512), out(1024), out(1024), out(1024), out(512)],
        compiler_params=_params(("parallel", "parallel")),
        name="proj0",
    )(h, g, w, qg, kvg, wuq, wukv, *tables)


def _sb_chunk(qh_ref, kv, u, mask, acc_ref, c_ref, first=False):
    keys = kv[0][0].shape[0]
    items = [(b, h) for b in range(len(kv)) for h in range(SB_HEADS)]
    sl = lambda h: slice((h // 2) * LANES, (h // 2 + 1) * LANES)

    def scores(b, h):
        return _dot_nt(qh_ref[b, h], kv[b][0][:, sl(h)])

    def suffix(z):
        sp = jnp.maximum(jnp.log2(1.0 + jnp.exp2(jnp.minimum(z, SOFTPLUS_CLAMP))), z)
        if mask is not None:
            sp = jnp.where(mask, sp, 0.0)
        return _dot(sp.astype(BF16), u)

    def weights(b, h, z, sfx):
        total = jnp.broadcast_to(sfx[:, 0:1], c_ref.shape[2:])
        if first:
            a = jnp.exp2(z + sfx)
        else:
            c = c_ref[b, h]
            a = jnp.exp2(z + sfx + jnp.concatenate([c] * (keys // LANES), axis=1))
        if mask is not None:
            a = jnp.where(mask, a, 0.0)
        av = _dot(a.astype(BF16), kv[b][1][:, sl(h)])
        if first:
            acc_ref[b, h], c_ref[b, h] = av, total
        else:
            acc_ref[b, h] += av
            c_ref[b, h] = c + total

    zs, sfxs = {}, {}
    for step in range(len(items) + 2 * SKEW):
        if step < len(items):
            zs[step] = scores(*items[step])
        if 0 <= step - SKEW < len(items):
            sfxs[step - SKEW] = suffix(zs[step - SKEW])
        if 0 <= step - 2 * SKEW < len(items):
            weights(*items[step - 2 * SKEW], zs.pop(step - 2 * SKEW), sfxs.pop(step - 2 * SKEW))


def _sb_prologue(q_ref, qh_ref):
    for b in range(q_ref.shape[0]):
        for p in range(SB_HEADS // 2):
            qh_ref[b, 2 * p], qh_ref[b, 2 * p + 1] = _split_pair(q_ref[b, :, p * LANES:(p + 1) * LANES])


def _sb_epilogue(acc_ref, g_ref, o_ref):
    for b in range(o_ref.shape[0]):
        for p in range(SB_HEADS // 2):
            sl = slice(p * LANES, (p + 1) * LANES)
            o = _merge_pair(acc_ref[b, 2 * p], acc_ref[b, 2 * p + 1])
            o_ref[b, :, sl] = (o * _silu(g_ref[b, :, sl].astype(F32))).astype(BF16)


def _sb_kernel(q_ref, k_ref, v_ref, km_ref, vm_ref, g_ref, u_ref, um_ref, o_ref, qh_ref, acc_ref, c_ref):
    i = pl.program_id(1)
    T = ATT_TILE
    nb = q_ref.shape[0]
    _sb_prologue(q_ref, qh_ref)

    def real_chunk(j, mask, first=False):
        start = pl.multiple_of(j * T, T)
        kv = [(k_ref[b, pl.ds(start, T), :], v_ref[b, pl.ds(start, T), :]) for b in range(nb)]
        _sb_chunk(qh_ref, kv, u_ref[...], mask, acc_ref, c_ref, first)

    real_chunk(i, _iota((T, T), 1) < _iota((T, T), 0), first=True)

    def earlier(idx, carry):
        real_chunk(i - 1 - idx, None)
        return carry

    lax.fori_loop(0, i, earlier, 0)
    _sb_chunk(qh_ref, [(km_ref[0], vm_ref[0])] * nb, um_ref[...], _iota((T, BLOCK), 1) >= N_PAD, acc_ref, c_ref)
    _sb_epilogue(acc_ref, g_ref, o_ref)


def _sb_attn(q, k, v, k_meta, v_meta, gate, u, u_meta):
    B, L, _ = q.shape
    T, nb = ATT_TILE, ATT_BATCH
    tile = pl.BlockSpec((nb, T, SB_WIDTH), lambda b, i: (b, i, 0))
    seq = pl.BlockSpec((nb, L, SB_WIDTH), lambda b, i: (b, 0, 0))
    meta = pl.BlockSpec((1, BLOCK, SB_WIDTH), lambda b, i: (0, 0, 0))
    const = lambda a: pl.BlockSpec(a.shape, lambda b, i: (0, 0))
    return pl.pallas_call(
        _sb_kernel,
        grid=(B // nb, L // T),
        in_specs=[tile, seq, seq, meta, meta, tile, const(u), const(u_meta)],
        out_specs=tile,
        out_shape=jax.ShapeDtypeStruct((B, L, SB_WIDTH), BF16),
        scratch_shapes=[pltpu.VMEM((nb, SB_HEADS, T, LANES), BF16), pltpu.VMEM((nb, SB_HEADS, T, LANES), F32),
                        pltpu.VMEM((nb, SB_HEADS, T, LANES), F32)],
        compiler_params=_params(("parallel", "arbitrary")),
        name="sb_attn",
    )(q, k, v, k_meta, v_meta, gate, u, u_meta)


def _sb_meta_kernel(q_ref, k_ref, v_ref, g_ref, u_ref, o_ref, qh_ref, acc_ref, c_ref):
    _sb_prologue(q_ref, qh_ref)
    row, col = _iota((BLOCK, BLOCK), 0), _iota((BLOCK, BLOCK), 1)
    _sb_chunk(qh_ref, [(k_ref[0], v_ref[0])], u_ref[...], (col < row) & (col >= N_PAD), acc_ref, c_ref, first=True)
    _sb_epilogue(acc_ref, g_ref, o_ref)


def _sb_meta(q, k, v, gate, u_meta):
    blk = pl.BlockSpec((1, BLOCK, SB_WIDTH), lambda i: (0, 0, 0))
    return pl.pallas_call(
        _sb_meta_kernel,
        grid=(1,),
        in_specs=[blk, blk, blk, blk, pl.BlockSpec(u_meta.shape, lambda i: (0, 0))],
        out_specs=blk,
        out_shape=jax.ShapeDtypeStruct((1, BLOCK, SB_WIDTH), BF16),
        scratch_shapes=[pltpu.VMEM((1, SB_HEADS, BLOCK, LANES), BF16), pltpu.VMEM((1, SB_HEADS, BLOCK, LANES), F32),
                        pltpu.VMEM((1, SB_HEADS, BLOCK, LANES), F32)],
        compiler_params=_params(("arbitrary",)),
        name="sb_meta",
    )(q, k, v, gate, u_meta)


def _mla_chunk(q_ref, kv, mask, acc_ref, m_ref, l_ref, first=False):
    items = [(b, h) for b in range(len(kv)) for h in range(MLA_HEADS)]
    rep = lambda row: jnp.broadcast_to(row, m_ref.shape[2:])

    def scores(b, h):
        hs = slice(h * LANES, (h + 1) * LANES)
        s = _dot_nt(kv[b][0][:, hs], q_ref[b, :, hs])
        return s if mask is None else jnp.where(mask, s, NEG)

    def probs(b, h, s):
        m_new = jnp.max(s, axis=0, keepdims=True)
        if first:
            alpha = None
        else:
            m_old = m_ref[b, h, 0:1, :]
            m_new = jnp.maximum(m_old, m_new)
            alpha = jnp.exp2(m_old - m_new)
        m_ref[b, h] = rep(m_new)
        p = jnp.exp2(s - m_new)
        return p.astype(BF16), jnp.sum(p, axis=0, keepdims=True), alpha

    def update(b, h, p, l_new, alpha):
        vs = slice((h // 2) * LANES, (h // 2 + 1) * LANES)
        pv = lax.dot_general(kv[b][1][:, vs], p, (((0,), (0,)), ((), ())), preferred_element_type=F32)
        if first:
            acc_ref[b, h], l_ref[b, h] = pv, rep(l_new)
        else:
            acc_ref[b, h] = alpha * acc_ref[b, h] + pv
            l_ref[b, h] = rep(alpha * l_ref[b, h, 0:1, :] + l_new)

    ss, ps = {}, {}
    for step in range(len(items) + 2 * MLA_SKEW):
        if step < len(items):
            ss[step] = scores(*items[step])
        if 0 <= step - MLA_SKEW < len(items):
            ps[step - MLA_SKEW] = probs(*items[step - MLA_SKEW], ss.pop(step - MLA_SKEW))
        if 0 <= step - 2 * MLA_SKEW < len(items):
            update(*items[step - 2 * MLA_SKEW], *ps.pop(step - 2 * MLA_SKEW))


def _mla_finish(acc_ref, l_ref, g_ref, o_ref):
    half = LANES // 2
    for b in range(o_ref.shape[0]):
        for p in range(MLA_HEADS // 2):
            sl = slice(p * LANES, (p + 1) * LANES)
            lo = acc_ref[b, 2 * p, :half, :] * (1.0 / l_ref[b, 2 * p, 0:1, :])
            hi = acc_ref[b, 2 * p + 1, half:, :] * (1.0 / l_ref[b, 2 * p + 1, 0:1, :])
            o = jnp.concatenate([lo, hi], axis=0).T
            gate = g_ref[b, :, MLA_WIDTH + p * LANES:MLA_WIDTH + (p + 1) * LANES].astype(F32)
            o_ref[b, :, sl] = (o * _silu(gate)).astype(BF16)


def _mla_kernel(q_ref, k_ref, v_ref, km_ref, vm_ref, g_ref, o_ref, acc_ref, m_ref, l_ref):
    i = pl.program_id(1)
    T = ATT_TILE
    nb = q_ref.shape[0]

    def real_kv(j):
        start = pl.multiple_of(j * T, T)
        return [(k_ref[b, pl.ds(start, T), :], v_ref[b, pl.ds(start, T), :]) for b in range(nb)]

    def first_chunk(extra):
        parts = [real_kv(i)] + ([real_kv(0)] if extra else []) + [[(km_ref[0], vm_ref[0])] * nb]
        kv = [tuple(jnp.concatenate([part[b][j] for part in parts], axis=0) for j in range(2)) for b in range(nb)]
        masks = ([_iota((T, T), 0) <= _iota((T, T), 1)] + ([_iota((T, T), 0) >= 0] if extra else [])
                 + [_iota((BLOCK, T), 0) >= N_PAD])
        _mla_chunk(q_ref, kv, jnp.concatenate(masks, axis=0), acc_ref, m_ref, l_ref, first=True)

    odd = i % 2

    @pl.when(odd == 1)
    def _():
        first_chunk(True)

    @pl.when(odd == 0)
    def _():
        first_chunk(False)

    def earlier(idx, carry):
        start = pl.multiple_of((odd + 2 * idx) * T, T)
        kv = [(k_ref[b, pl.ds(start, 2 * T), :], v_ref[b, pl.ds(start, 2 * T), :]) for b in range(nb)]
        _mla_chunk(q_ref, kv, None, acc_ref, m_ref, l_ref)
        return carry

    lax.fori_loop(0, i // 2, earlier, 0)
    _mla_finish(acc_ref, l_ref, g_ref, o_ref)


def _mla_attn(q, k, v, k_meta, v_meta, gate):
    B, L, _ = q.shape
    T, nb = ATT_TILE, ATT_BATCH
    qk_w = MLA_HEADS * LANES
    return pl.pallas_call(
        _mla_kernel,
        grid=(B // nb, L // T),
        in_specs=[pl.BlockSpec((nb, T, qk_w), lambda b, i: (b, i, 0)),
                  pl.BlockSpec((nb, L, qk_w), lambda b, i: (b, 0, 0)),
                  pl.BlockSpec((nb, L, MLA_WIDTH), lambda b, i: (b, 0, 0)),
                  pl.BlockSpec((1, BLOCK, qk_w), lambda b, i: (0, 0, 0)),
                  pl.BlockSpec((1, BLOCK, MLA_WIDTH), lambda b, i: (0, 0, 0)),
                  pl.BlockSpec((nb, T, 2 * MLA_WIDTH), lambda b, i: (b, i, 0))],
        out_specs=pl.BlockSpec((nb, T, MLA_WIDTH), lambda b, i: (b, i, 0)),
        out_shape=jax.ShapeDtypeStruct((B, L, MLA_WIDTH), BF16),
        scratch_shapes=[pltpu.VMEM((nb, MLA_HEADS, LANES, T), F32), pltpu.VMEM((nb, MLA_HEADS, 8, T), F32),
                        pltpu.VMEM((nb, MLA_HEADS, 8, T), F32)],
        compiler_params=_params(("parallel", "arbitrary")),
        name="mla_attn",
    )(q, k, v, k_meta, v_meta, gate)


def _mla_meta_kernel(q_ref, k_ref, v_ref, g_ref, o_ref, acc_ref, m_ref, l_ref):
    key, query = _iota((BLOCK, BLOCK), 0), _iota((BLOCK, BLOCK), 1)
    _mla_chunk(q_ref, [(k_ref[0], v_ref[0])], (key <= query) & (key >= N_PAD), acc_ref, m_ref, l_ref, first=True)
    _mla_finish(acc_ref, l_ref, g_ref, o_ref)


def _mla_meta(q, k, v, gate):
    qk_w = MLA_HEADS * LANES
    blk = lambda w: pl.BlockSpec((1, BLOCK, w), lambda i: (0, 0, 0))
    return pl.pallas_call(
        _mla_meta_kernel,
        grid=(1,),
        in_specs=[blk(qk_w), blk(qk_w), blk(MLA_WIDTH), blk(2 * MLA_WIDTH)],
        out_specs=blk(MLA_WIDTH),
        out_shape=jax.ShapeDtypeStruct((1, BLOCK, MLA_WIDTH), BF16),
        scratch_shapes=[pltpu.VMEM((1, MLA_HEADS, LANES, BLOCK), F32), pltpu.VMEM((1, MLA_HEADS, 8, BLOCK), F32),
                        pltpu.VMEM((1, MLA_HEADS, 8, BLOCK), F32)],
        compiler_params=_params(("arbitrary",)),
        name="mla_meta",
    )(q, k, v, gate)


def _mid_kernel(osb_ref, omla_ref, h_ref, wo_ref, g_ref, w_ref, h1_ref, q_ref, kv_ref, gate_ref):
    rows = h_ref.shape[1]
    halves = [slice(0, rows // 2), slice(rows // 2, rows)] if rows >= 2 * BLOCK else [slice(0, rows)]
    ys = [_dot(osb_ref[0, r, :], wo_ref[0:512, :]) + _dot(omla_ref[0, r, :], wo_ref[512:1024, :]) for r in halves]
    for r, y in zip(halves, ys):
        h1 = h_ref[0, r, :] + y
        h1_ref[0, r, :] = h1
        xn = _rms(h1, g_ref[...]).astype(BF16)
        q_ref[0, r, :] = (_dot(xn, w_ref[:, 0:1024]) * LOG2E).astype(BF16)
        kv_ref[0, r, :] = _dot(xn, w_ref[:, 1024:1280]).astype(BF16)
        gate_ref[0, r, :] = _dot(xn, w_ref[:, 1280:2304]).astype(BF16)


def _mid(osb, omla, h, tile, wo, g, w):
    B, L, _ = h.shape
    row = lambda width: pl.BlockSpec((1, tile, width), lambda b, t: (b, t, 0))
    full = lambda a: pl.BlockSpec(a.shape, lambda b, t: (0,) * a.ndim)
    return pl.pallas_call(
        _mid_kernel,
        grid=(B, L // tile),
        in_specs=[row(512), row(512), row(D_MODEL), full(wo), full(g), full(w)],
        out_specs=[row(D_MODEL), row(1024), row(256), row(1024)],
        out_shape=[jax.ShapeDtypeStruct((B, L, D_MODEL), F32),
                   jax.ShapeDtypeStruct((B, L, 1024), BF16),
                   jax.ShapeDtypeStruct((B, L, 256), BF16),
                   jax.ShapeDtypeStruct((B, L, 1024), BF16)],
        compiler_params=_params(("parallel", "parallel")),
        name="mid",
    )(osb, omla, h, wo, g, w)


def _swa_kernel(sink_ref, q_ref, kvp_ref, kvc_ref, kvm_ref, g_ref, bb_ref, mb_ref, h_ref, wo_ref, fg_ref,
                out_ref, o_scr):
    n = pl.program_id(1)
    first = jnp.where(n > 0, 0, 1)
    in_cur = _iota((BLOCK, BLOCK), 0) <= _iota((BLOCK, BLOCK), 1)
    blocks_before = (n + 1).astype(F32) * float(BLOCK)
    pairs = SWA_HEADS // 2
    half = LANES // 2
    items = [(b, p) for b in range(q_ref.shape[0]) for p in range(pairs)]
    k_sl, v_sl = slice(0, LANES), slice(LANES, 2 * LANES)
    keys_of = lambda b, sl: jnp.concatenate([kvp_ref[b, :, sl], kvc_ref[b, :, sl], kvm_ref[0, :, sl]], axis=0)

    def scores(b, p):
        q_stack = jnp.concatenate(_split_pair(q_ref[b, :, p * LANES:(p + 1) * LANES]), axis=0)
        return _dot_nt(keys_of(b, k_sl), q_stack)

    def probs(p, z_pair):
        p_cols, invs = [], []
        for hh in range(2):
            h = p + pairs * hh
            z = z_pair[:, hh * BLOCK:(hh + 1) * BLOCK]
            slope = 2.0 ** (-8.0 * (h + 1.0) / SWA_HEADS) * LOG2E
            s_band = jnp.where(in_cur, z[BLOCK:2 * BLOCK], z[:BLOCK]) - bb_ref[first, h]
            s_meta = z[2 * BLOCK:] - mb_ref[h] - slope * blocks_before
            sink = sink_ref[h] * LOG2E
            m = jnp.maximum(jnp.max(jnp.maximum(s_band, s_meta), axis=0, keepdims=True), sink)
            p_band, p_meta = jnp.exp2(s_band - m), jnp.exp2(s_meta - m)
            denom = jnp.sum(p_band + p_meta, axis=0, keepdims=True) + jnp.exp2(sink - m)
            p_cols.append(jnp.concatenate(
                [jnp.where(in_cur, 0.0, p_band), jnp.where(in_cur, p_band, 0.0), p_meta], axis=0).astype(BF16))
            invs.append(1.0 / denom)
        return jnp.concatenate(p_cols, axis=1), jnp.concatenate(invs, axis=1)

    def values(b, p, p_pair, inv):
        o_t = lax.dot_general(keys_of(b, v_sl), p_pair, (((0,), (0,)), ((), ())), preferred_element_type=F32) * inv
        o = jnp.concatenate([o_t[:half, :BLOCK], o_t[half:, BLOCK:]], axis=0).T
        sl = slice(p * LANES, (p + 1) * LANES)
        o_scr[b, :, sl] = (o * _silu(g_ref[b, :, sl].astype(F32))).astype(BF16)

    zs, ps = {}, {}
    for step in range(len(items) + 2 * SWA_SKEW):
        if step < len(items):
            zs[step] = scores(*items[step])
        if 0 <= step - SWA_SKEW < len(items):
            ps[step - SWA_SKEW] = probs(items[step - SWA_SKEW][1], zs.pop(step - SWA_SKEW))
        t = step - 2 * SWA_SKEW
        if 0 <= t < len(items):
            b, p = items[t]
            values(b, p, *ps.pop(t))
            if p == pairs - 1:
                out_ref[b] = _rms(h_ref[b] + _dot(o_scr[b], wo_ref[...]), fg_ref[...])


def _swa_bias_tables():
    col, row = np.arange(BLOCK)[:, None], np.arange(BLOCK)[None, :]
    slopes = (2.0 ** (-8.0 * (np.arange(SWA_HEADS) + 1.0) / SWA_HEADS) * LOG2E)[:, None, None]
    band = slopes * np.where(col <= row, row - col, row - col + BLOCK)
    band_first = band + np.where(col <= row, 0.0, -NEG)
    meta = np.where(col >= N_PAD, 0.0, -NEG) + slopes * (row - col)
    return jnp.asarray(np.stack([band, band_first]), F32), jnp.asarray(meta, F32)


def _swa_attn(sinks, q, kv, kv_meta, gate, h1, wo, final_g):
    B, L, _ = q.shape
    kvw = kv.shape[-1]
    nb = SWA_BATCH
    band_bias, meta_bias = _swa_bias_tables()
    row = lambda w: pl.BlockSpec((nb, BLOCK, w), lambda b, n, s: (b, n, 0))
    const = lambda a: pl.BlockSpec(a.shape, lambda b, n, s: (0,) * a.ndim)
    grid_spec = pltpu.PrefetchScalarGridSpec(
        num_scalar_prefetch=1,
        grid=(B // nb, L // BLOCK),
        in_specs=[row(SWA_WIDTH),
                  pl.BlockSpec((nb, BLOCK, kvw), lambda b, n, s: (b, jnp.maximum(n - 1, 0), 0)),
                  row(kvw), const(kv_meta), row(SWA_WIDTH), const(band_bias), const(meta_bias),
                  row(D_MODEL), const(wo), const(final_g)],
        out_specs=row(D_MODEL),
        scratch_shapes=[pltpu.VMEM((nb, BLOCK, SWA_WIDTH), BF16)],
    )
    return pl.pallas_call(
        _swa_kernel,
        grid_spec=grid_spec,
        out_shape=jax.ShapeDtypeStruct((B, L, D_MODEL), F32),
        compiler_params=_params(("parallel", "parallel")),
        name="swa_attn",
    )(sinks, q, kv, kv, kv_meta, gate, band_bias, meta_bias, h1, wo, final_g)


def _layer0_weights(w_in, w_uq, w_ukv):
    w_in, w_uq, w_ukv = w_in.astype(BF16), w_uq.astype(BF16), w_ukv.astype(BF16)
    q, k, v, g_sb = (w_in[:, i * 512:(i + 1) * 512] for i in range(4))
    c_q, c_kv = w_in[:, 2048:2304], w_in[:, 2304:2432]
    k_r, g_mla = w_in[:, 2432:2464], w_in[:, 2464:2976]
    half = MLA_ROPE // 2
    r1, r2 = k_r[:, :half], k_r[:, half:]
    z = lambda n: jnp.zeros((D_MODEL, n), w_in.dtype)
    kr_blk = jnp.concatenate([z(MLA_NOPE), r1, r2, z(LANES - MLA_NOPE - MLA_ROPE)], axis=1)
    w0 = jnp.concatenate([q * (SB_DIM ** -0.5), k, v, g_sb, g_mla, c_q, c_kv, kr_blk], axis=1)

    uq = w_uq.reshape(MLA_Q_LORA, MLA_HEADS, MLA_NOPE + MLA_ROPE)
    nope, u1, u2 = uq[..., :MLA_NOPE], uq[..., MLA_NOPE:MLA_NOPE + half], uq[..., MLA_NOPE + half:]
    zq = lambda n: jnp.zeros((MLA_Q_LORA, MLA_HEADS, n), w_uq.dtype)
    uq_main = jnp.concatenate([nope, u1, u2, zq(LANES - MLA_NOPE - MLA_ROPE)], axis=-1)
    uq_rot = jnp.concatenate([zq(MLA_NOPE), -u2, u1, zq(LANES - MLA_NOPE - MLA_ROPE)], axis=-1)
    wuq = jnp.concatenate([uq_main.reshape(MLA_Q_LORA, -1), uq_rot.reshape(MLA_Q_LORA, -1)], axis=1)

    ukv = w_ukv.reshape(MLA_KV_LORA, MLA_HEADS, MLA_NOPE + MLA_V)
    k_nope = jnp.concatenate([ukv[..., :MLA_NOPE],
                              jnp.zeros((MLA_KV_LORA, MLA_HEADS, LANES - MLA_NOPE), w_ukv.dtype)], axis=-1)
    wukv = jnp.concatenate([k_nope.reshape(MLA_KV_LORA, -1),
                            ukv[..., MLA_NOPE:].reshape(MLA_KV_LORA, -1)], axis=1)
    return w0, wuq, wukv


def _pair_heads(w, axis):
    shape = w.shape
    w = w.reshape(shape[:axis] + (SWA_KV_HEADS, SWA_HEADS // SWA_KV_HEADS, SWA_DIM) + shape[axis + 1:])
    return jnp.swapaxes(w, axis, axis + 1).reshape(shape)


def _layer1_weights(w_in, w_out):
    w_in, w_out = w_in.astype(BF16), w_out.astype(BF16)
    q, kv, g = w_in[:, :1024], w_in[:, 1024:1280], w_in[:, 1280:2304]
    w1 = jnp.concatenate([_pair_heads(q * (SWA_DIM ** -0.5), 1), kv, _pair_heads(g, 1)], axis=1)
    return w1, _pair_heads(w_out, 0)


def _rope_tables():
    half = MLA_ROPE // 2
    pos = np.arange(N_META + SEQ, dtype=np.float64)
    inv = ROPE_BASE ** (-np.arange(half, dtype=np.float64) / half)
    ang = pos[:, None] * inv[None, :]
    cos, sin = np.cos(ang), np.sin(ang)
    n = pos.shape[0]
    z = lambda w: np.zeros((n, w))
    tail = LANES - MLA_NOPE - MLA_ROPE
    c = np.concatenate([np.ones((n, MLA_NOPE)), cos, cos, z(tail)], axis=1)
    s = np.concatenate([z(MLA_NOPE), sin, sin, z(tail)], axis=1)
    s1 = np.concatenate([z(MLA_NOPE), -sin, z(half), z(tail)], axis=1)
    s2 = np.concatenate([z(MLA_NOPE), z(half), sin, z(tail)], axis=1)
    scale = (MLA_NOPE + MLA_ROPE) ** -0.5 * LOG2E
    tabs = (c * scale, s * scale, c, s1, s2)
    pad = lambda t: np.concatenate([np.zeros((N_PAD, LANES)), t[:N_META]], axis=0)
    return (tuple(jnp.asarray(pad(t), F32) for t in tabs), tuple(jnp.asarray(t[N_META:], F32) for t in tabs))


def _suffix_matrix(n):
    return jnp.asarray(np.where(np.arange(n)[:, None] >= np.arange(n)[None, :], -1.0, 0.0), BF16)


def kernel(x, meta, norm_g, final_g, ev_w_in, ev_q_norm_g, ev_kv_norm_g, ev_w_uq, ev_w_ukv,
           ev_w_out, od_w_in, od_sinks, od_w_out):
    w0, wuq, wukv = _layer0_weights(ev_w_in[0], ev_w_uq[0], ev_w_ukv[0])
    w1, wo1 = _layer1_weights(od_w_in[0], od_w_out[0])
    wo0 = ev_w_out[0].astype(BF16)
    tabs_meta, tabs_real = _rope_tables()
    u, u_meta = _suffix_matrix(ATT_TILE), _suffix_matrix(BLOCK)
    l0 = (norm_g[0:1], w0, ev_q_norm_g[0:1], ev_kv_norm_g[0:1], wuq, wukv)

    hm = jnp.concatenate([jnp.zeros((N_PAD, D_MODEL), x.dtype), meta.astype(x.dtype)], axis=0)[None]
    qsb_m, ksb_m, vsb_m, gate_m, qm_m, km_m, vm_m = _proj0(hm, BLOCK, *l0, tabs_meta)
    osb_m = _sb_meta(qsb_m, ksb_m, vsb_m, gate_m, u_meta)
    omla_m = _mla_meta(qm_m, km_m, vm_m, gate_m)
    _, _, kv1_m, _ = _mid(osb_m, omla_m, hm, BLOCK, wo0, norm_g[1:2], w1)

    q_sb, k_sb, v_sb, gate0, q_mla, k_mla, v_mla = _proj0(x, ROW_TILE, *l0, tabs_real)
    o_sb = _sb_attn(q_sb, k_sb, v_sb, ksb_m, vsb_m, gate0, u, u_meta)
    o_mla = _mla_attn(q_mla, k_mla, v_mla, km_m, vm_m, gate0)
    h1, q1, kv1, gate1 = _mid(o_sb, o_mla, x, ROW_TILE, wo0, norm_g[1:2], w1)
    return _swa_attn(od_sinks[0], q1, kv1, kv1_m, gate1, h1, wo1, final_g[None, :])
```

```python
import math

import numpy as np
import jax
import jax.numpy as jnp
from jax import lax
from jax.experimental import pallas as pl
from jax.experimental.pallas import tpu as pltpu

D_MODEL = 1024
SEQ = 2048
N_META = 16
BLOCK = 128
N_PAD = BLOCK - N_META
NORM_EPS = 1e-6
NEG = -1e30

SB_HEADS = 8
SB_DIM = 64
SB_WIDTH = SB_HEADS * SB_DIM
MLA_HEADS = 8
MLA_Q_LORA = 256
MLA_KV_LORA = 128
MLA_NOPE = 64
MLA_ROPE = 32
MLA_V = 64
MLA_WIDTH = MLA_HEADS * MLA_V
ROPE_BASE = 10000.0
SWA_HEADS = 16
SWA_KV_HEADS = 2
SWA_DIM = 64
SWA_WIDTH = SWA_HEADS * SWA_DIM

LANES = 128
ROW_TILE = 1024
ATT_TILE = 256
ATT_BATCH = 2
VMEM_LIMIT = 48 * 1024 * 1024
LOG2E = math.log2(math.e)
SWA_BATCH = 4
MLA_SKEW = 2
SWA_SKEW = 2
SKEW = 1
DEAD_CARRY = -256.0
SOFTPLUS_CLAMP = 64.0

BF16 = jnp.bfloat16
F32 = jnp.float32


def _dot(a, b):
    return jnp.dot(a, b, preferred_element_type=F32)


def _dot_nt(a, b):
    return lax.dot_general(a, b, (((1,), (1,)), ((), ())), preferred_element_type=F32)


def _rms(x, g):
    ms = jnp.mean(x * x, axis=-1, keepdims=True)
    return x * lax.rsqrt(ms + NORM_EPS) * g


def _silu(g):
    return g * (1.0 / (1.0 + jnp.exp(-g)))


def _params(sem):
    return pltpu.CompilerParams(dimension_semantics=sem, vmem_limit_bytes=VMEM_LIMIT)


def _iota(shape, dim):
    return lax.broadcasted_iota(jnp.int32, shape, dim)


def _split_pair(x):
    lane = _iota(x.shape, 1)
    zero = jnp.zeros_like(x)
    return jnp.where(lane < LANES // 2, x, zero), jnp.where(lane >= LANES // 2, x, zero)


def _merge_pair(lo, hi):
    lane = _iota(lo.shape, 1)
    return jnp.where(lane < LANES // 2, lo, hi)


def _proj0_kernel(x_ref, g_ref, w_ref, qg_ref, kvg_ref, wuq_ref, wukv_ref,
                  cq_ref, sq_ref, ck_ref, sk1_ref, sk2_ref,
                  qsb_ref, ksb_ref, vsb_ref, gate_ref, qm_ref, km_ref, vm_ref):
    rows = x_ref.shape[1]
    halves = [slice(0, rows // 2), slice(rows // 2, rows)] if rows >= 2 * BLOCK else [slice(0, rows)]
    qk_w = MLA_HEADS * LANES
    for r in halves:
        xn = _rms(x_ref[0, r, :], g_ref[...]).astype(BF16)
        qsb_ref[0, r, :] = (_dot(xn, w_ref[:, 0:512]) * LOG2E).astype(BF16)
        ksb_ref[0, r, :] = _dot(xn, w_ref[:, 512:1024]).astype(BF16)
        vsb_ref[0, r, :] = _dot(xn, w_ref[:, 1024:1536]).astype(BF16)
        gate_ref[0, r, :] = _dot(xn, w_ref[:, 1536:2560]).astype(BF16)
        lat = _dot(xn, w_ref[:, 2560:3072])
        cqn = _rms(lat[:, 0:256], qg_ref[...]).astype(BF16)
        ckvn = _rms(lat[:, 256:384], kvg_ref[...]).astype(BF16)
        kr = lat[:, 384:512]
        k_rope = (kr * ck_ref[r, :] + pltpu.roll(kr, LANES - MLA_ROPE // 2, 1) * sk1_ref[r, :]
                  + pltpu.roll(kr, MLA_ROPE // 2, 1) * sk2_ref[r, :])
        q_all = _dot(cqn, wuq_ref[...])
        kv_all = _dot(ckvn, wukv_ref[...])
        cq = cq_ref[r, :]
        sq = sq_ref[r, :]
        for h in range(MLA_HEADS):
            lo, hi = h * LANES, (h + 1) * LANES
            qm_ref[0, r, lo:hi] = (q_all[:, lo:hi] * cq + q_all[:, qk_w + lo:qk_w + hi] * sq).astype(BF16)
            km_ref[0, r, lo:hi] = (kv_all[:, lo:hi] + k_rope).astype(BF16)
        vm_ref[0, r, :] = kv_all[:, qk_w:].astype(BF16)


def _proj0(h, tile, g, w, qg, kvg, wuq, wukv, tables):
    B, L, _ = h.shape
    row = lambda width: pl.BlockSpec((1, tile, width), lambda b, t: (b, t, 0))
    full = lambda a: pl.BlockSpec(a.shape, lambda b, t: (0,) * a.ndim)
    tab = pl.BlockSpec((tile, LANES), lambda b, t: (t, 0))
    out = lambda width: jax.ShapeDtypeStruct((B, L, width), BF16)
    return pl.pallas_call(
        _proj0_kernel,
        grid=(B, L // tile),
        in_specs=[row(D_MODEL), full(g), full(w), full(qg), full(kvg), full(wuq), full(wukv),
                  tab, tab, tab, tab, tab],
        out_specs=[row(512), row(512), row(512), row(1024), row(1024), row(1024), row(512)],
        out_shape=[out(512), out(512), out(512), out(1024), out(1024), out(1024), out(512)],
        compiler_params=_params(("parallel", "parallel")),
        name="proj0",
    )(h, g, w, qg, kvg, wuq, wukv, *tables)


def _sb_chunk(qh_ref, kv, u, mask, acc_ref, c_ref, first=False):
    keys = kv[0][0].shape[0]
    items = [(b, h) for b in range(len(kv)) for h in range(SB_HEADS)]
    sl = lambda h: slice((h // 2) * LANES, (h // 2 + 1) * LANES)

    def scores(b, h):
        return _dot_nt(qh_ref[b, h], kv[b][0][:, sl(h)])

    def suffix(z):
        sp = jnp.maximum(jnp.log2(1.0 + jnp.exp2(jnp.minimum(z, SOFTPLUS_CLAMP))), z)
        if mask is not None:
            sp = jnp.where(mask, sp, 0.0)
        return _dot(sp.astype(BF16), u)

    def weights(b, h, z, sfx):
        total = jnp.broadcast_to(sfx[:, 0:1], c_ref.shape[2:])
        if first:
            a = jnp.exp2(z + sfx)
        else:
            c = c_ref[b, h]
            a = jnp.exp2(z + sfx + jnp.concatenate([c] * (keys // LANES), axis=1))
        if mask is not None:
            a = jnp.where(mask, a, 0.0)
        av = _dot(a.astype(BF16), kv[b][1][:, sl(h)])
        if first:
            acc_ref[b, h], c_ref[b, h] = av, total
        else:
            acc_ref[b, h] += av
            c_ref[b, h] = c + total

    zs, sfxs = {}, {}
    for step in range(len(items) + 2 * SKEW):
        if step < len(items):
            zs[step] = scores(*items[step])
        if 0 <= step - SKEW < len(items):
            sfxs[step - SKEW] = suffix(zs[step - SKEW])
        if 0 <= step - 2 * SKEW < len(items):
            weights(*items[step - 2 * SKEW], zs.pop(step - 2 * SKEW), sfxs.pop(step - 2 * SKEW))


def _sb_prologue(q_ref, qh_ref):
    for b in range(q_ref.shape[0]):
        for p in range(SB_HEADS // 2):
            qh_ref[b, 2 * p], qh_ref[b, 2 * p + 1] = _split_pair(q_ref[b, :, p * LANES:(p + 1) * LANES])


def _sb_epilogue(acc_ref, g_ref, o_ref):
    for b in range(o_ref.shape[0]):
        for p in range(SB_HEADS // 2):
            sl = slice(p * LANES, (p + 1) * LANES)
            o = _merge_pair(acc_ref[b, 2 * p], acc_ref[b, 2 * p + 1])
            o_ref[b, :, sl] = (o * _silu(g_ref[b, :, sl].astype(F32))).astype(BF16)


def _sb_kernel(q_ref, k_ref, v_ref, km_ref, vm_ref, g_ref, u_ref, um_ref, o_ref, qh_ref, acc_ref, c_ref):
    i = pl.program_id(1)
    T = ATT_TILE
    nb = q_ref.shape[0]
    _sb_prologue(q_ref, qh_ref)

    def real_chunk(j, mask, first=False):
        start = pl.multiple_of(j * T, T)
        kv = [(k_ref[b, pl.ds(start, T), :], v_ref[b, pl.ds(start, T), :]) for b in range(nb)]
        _sb_chunk(qh_ref, kv, u_ref[...], mask, acc_ref, c_ref, first)

    real_chunk(i, _iota((T, T), 1) < _iota((T, T), 0), first=True)

    def alive():
        return jnp.max(c_ref[...]) > DEAD_CARRY

    def earlier(state):
        idx, _ = state
        real_chunk(i - 1 - idx, None)
        return idx + 1, alive()

    _, still_alive = lax.while_loop(lambda state: (state[0] < i) & state[1], earlier, (jnp.int32(0), alive()))

    @pl.when(still_alive)
    def _():
        _sb_chunk(qh_ref, [(km_ref[0], vm_ref[0])] * nb, um_ref[...], _iota((T, BLOCK), 1) >= N_PAD,
                  acc_ref, c_ref)

    _sb_epilogue(acc_ref, g_ref, o_ref)


def _sb_attn(q, k, v, k_meta, v_meta, gate, u, u_meta):
    B, L, _ = q.shape
    T, nb = ATT_TILE, ATT_BATCH
    tile = pl.BlockSpec((nb, T, SB_WIDTH), lambda b, i: (b, i, 0))
    seq = pl.BlockSpec((nb, L, SB_WIDTH), lambda b, i: (b, 0, 0))
    meta = pl.BlockSpec((1, BLOCK, SB_WIDTH), lambda b, i: (0, 0, 0))
    const = lambda a: pl.BlockSpec(a.shape, lambda b, i: (0, 0))
    return pl.pallas_call(
        _sb_kernel,
        grid=(B // nb, L // T),
        in_specs=[tile, seq, seq, meta, meta, tile, const(u), const(u_meta)],
        out_specs=tile,
        out_shape=jax.ShapeDtypeStruct((B, L, SB_WIDTH), BF16),
        scratch_shapes=[pltpu.VMEM((nb, SB_HEADS, T, LANES), BF16), pltpu.VMEM((nb, SB_HEADS, T, LANES), F32),
                        pltpu.VMEM((nb, SB_HEADS, T, LANES), F32)],
        compiler_params=_params(("parallel", "arbitrary")),
        name="sb_attn",
    )(q, k, v, k_meta, v_meta, gate, u, u_meta)


def _sb_meta_kernel(q_ref, k_ref, v_ref, g_ref, u_ref, o_ref, qh_ref, acc_ref, c_ref):
    _sb_prologue(q_ref, qh_ref)
    row, col = _iota((BLOCK, BLOCK), 0), _iota((BLOCK, BLOCK), 1)
    _sb_chunk(qh_ref, [(k_ref[0], v_ref[0])], u_ref[...], (col < row) & (col >= N_PAD), acc_ref, c_ref, first=True)
    _sb_epilogue(acc_ref, g_ref, o_ref)


def _sb_meta(q, k, v, gate, u_meta):
    blk = pl.BlockSpec((1, BLOCK, SB_WIDTH), lambda i: (0, 0, 0))
    return pl.pallas_call(
        _sb_meta_kernel,
        grid=(1,),
        in_specs=[blk, blk, blk, blk, pl.BlockSpec(u_meta.shape, lambda i: (0, 0))],
        out_specs=blk,
        out_shape=jax.ShapeDtypeStruct((1, BLOCK, SB_WIDTH), BF16),
        scratch_shapes=[pltpu.VMEM((1, SB_HEADS, BLOCK, LANES), BF16), pltpu.VMEM((1, SB_HEADS, BLOCK, LANES), F32),
                        pltpu.VMEM((1, SB_HEADS, BLOCK, LANES), F32)],
        compiler_params=_params(("arbitrary",)),
        name="sb_meta",
    )(q, k, v, gate, u_meta)


def _mla_chunk(q_ref, kv, mask, acc_ref, m_ref, l_ref, first=False):
    items = [(b, h) for b in range(len(kv)) for h in range(MLA_HEADS)]
    rep = lambda row: jnp.broadcast_to(row, m_ref.shape[2:])

    def scores(b, h):
        hs = slice(h * LANES, (h + 1) * LANES)
        s = _dot_nt(kv[b][0][:, hs], q_ref[b, :, hs])
        return s if mask is None else jnp.where(mask, s, NEG)

    def probs(b, h, s):
        m_new = jnp.max(s, axis=0, keepdims=True)
        if first:
            alpha = None
        else:
            m_old = m_ref[b, h, 0:1, :]
            m_new = jnp.maximum(m_old, m_new)
            alpha = jnp.exp2(m_old - m_new)
        m_ref[b, h] = rep(m_new)
        p = jnp.exp2(s - m_new)
        return p.astype(BF16), jnp.sum(p, axis=0, keepdims=True), alpha

    def update(b, h, p, l_new, alpha):
        vs = slice((h // 2) * LANES, (h // 2 + 1) * LANES)
        pv = lax.dot_general(kv[b][1][:, vs], p, (((0,), (0,)), ((), ())), preferred_element_type=F32)
        if first:
            acc_ref[b, h], l_ref[b, h] = pv, rep(l_new)
        else:
            acc_ref[b, h] = alpha * acc_ref[b, h] + pv
            l_ref[b, h] = rep(alpha * l_ref[b, h, 0:1, :] + l_new)

    ss, ps = {}, {}
    for step in range(len(items) + 2 * MLA_SKEW):
        if step < len(items):
            ss[step] = scores(*items[step])
        if 0 <= step - MLA_SKEW < len(items):
            ps[step - MLA_SKEW] = probs(*items[step - MLA_SKEW], ss.pop(step - MLA_SKEW))
        if 0 <= step - 2 * MLA_SKEW < len(items):
            update(*items[step - 2 * MLA_SKEW], *ps.pop(step - 2 * MLA_SKEW))


def _mla_finish(acc_ref, l_ref, g_ref, o_ref):
    half = LANES // 2
    for b in range(o_ref.shape[0]):
        for p in range(MLA_HEADS // 2):
            sl = slice(p * LANES, (p + 1) * LANES)
            lo = acc_ref[b, 2 * p, :half, :] * (1.0 / l_ref[b, 2 * p, 0:1, :])
            hi = acc_ref[b, 2 * p + 1, half:, :] * (1.0 / l_ref[b, 2 * p + 1, 0:1, :])
            o = jnp.concatenate([lo, hi], axis=0).T
            gate = g_ref[b, :, MLA_WIDTH + p * LANES:MLA_WIDTH + (p + 1) * LANES].astype(F32)
            o_ref[b, :, sl] = (o * _silu(gate)).astype(BF16)


def _mla_kernel(q_ref, k_ref, v_ref, km_ref, vm_ref, g_ref, o_ref, acc_ref, m_ref, l_ref):
    i = pl.program_id(1)
    T = ATT_TILE
    nb = q_ref.shape[0]

    def real_kv(j):
        start = pl.multiple_of(j * T, T)
        return [(k_ref[b, pl.ds(start, T), :], v_ref[b, pl.ds(start, T), :]) for b in range(nb)]

    def first_chunk(extra):
        parts = [real_kv(i)] + ([real_kv(0)] if extra else []) + [[(km_ref[0, N_PAD:, :], vm_ref[0, N_PAD:, :])] * nb]
        kv = [tuple(jnp.concatenate([part[b][j] for part in parts], axis=0) for j in range(2)) for b in range(nb)]
        masks = ([_iota((T, T), 0) <= _iota((T, T), 1)] + ([_iota((T, T), 0) >= 0] if extra else [])
                 + [_iota((N_META, T), 0) >= 0])
        _mla_chunk(q_ref, kv, jnp.concatenate(masks, axis=0), acc_ref, m_ref, l_ref, first=True)

    odd = i % 2

    @pl.when(odd == 1)
    def _():
        first_chunk(True)

    @pl.when(odd == 0)
    def _():
        first_chunk(False)

    def earlier(idx, carry):
        start = pl.multiple_of((odd + 2 * idx) * T, T)
        kv = [(k_ref[b, pl.ds(start, 2 * T), :], v_ref[b, pl.ds(start, 2 * T), :]) for b in range(nb)]
        _mla_chunk(q_ref, kv, None, acc_ref, m_ref, l_ref)
        return carry

    lax.fori_loop(0, i // 2, earlier, 0)
    _mla_finish(acc_ref, l_ref, g_ref, o_ref)


def _mla_attn(q, k, v, k_meta, v_meta, gate):
    B, L, _ = q.shape
    T, nb = ATT_TILE, ATT_BATCH
    qk_w = MLA_HEADS * LANES
    return pl.pallas_call(
        _mla_kernel,
        grid=(B // nb, L // T),
        in_specs=[pl.BlockSpec((nb, T, qk_w), lambda b, i: (b, i, 0)),
                  pl.BlockSpec((nb, L, qk_w), lambda b, i: (b, 0, 0)),
                  pl.BlockSpec((nb, L, MLA_WIDTH), lambda b, i: (b, 0, 0)),
                  pl.BlockSpec((1, BLOCK, qk_w), lambda b, i: (0, 0, 0)),
                  pl.BlockSpec((1, BLOCK, MLA_WIDTH), lambda b, i: (0, 0, 0)),
                  pl.BlockSpec((nb, T, 2 * MLA_WIDTH), lambda b, i: (b, i, 0))],
        out_specs=pl.BlockSpec((nb, T, MLA_WIDTH), lambda b, i: (b, i, 0)),
        out_shape=jax.ShapeDtypeStruct((B, L, MLA_WIDTH), BF16),
        scratch_shapes=[pltpu.VMEM((nb, MLA_HEADS, LANES, T), F32), pltpu.VMEM((nb, MLA_HEADS, 8, T), F32),
                        pltpu.VMEM((nb, MLA_HEADS, 8, T), F32)],
        compiler_params=_params(("parallel", "arbitrary")),
        name="mla_attn",
    )(q, k, v, k_meta, v_meta, gate)


def _mla_meta_kernel(q_ref, k_ref, v_ref, g_ref, o_ref, acc_ref, m_ref, l_ref):
    key, query = _iota((BLOCK, BLOCK), 0), _iota((BLOCK, BLOCK), 1)
    _mla_chunk(q_ref, [(k_ref[0], v_ref[0])], (key <= query) & (key >= N_PAD), acc_ref, m_ref, l_ref, first=True)
    _mla_finish(acc_ref, l_ref, g_ref, o_ref)


def _mla_meta(q, k, v, gate):
    qk_w = MLA_HEADS * LANES
    blk = lambda w: pl.BlockSpec((1, BLOCK, w), lambda i: (0, 0, 0))
    return pl.pallas_call(
        _mla_meta_kernel,
        grid=(1,),
        in_specs=[blk(qk_w), blk(qk_w), blk(MLA_WIDTH), blk(2 * MLA_WIDTH)],
        out_specs=blk(MLA_WIDTH),
        out_shape=jax.ShapeDtypeStruct((1, BLOCK, MLA_WIDTH), BF16),
        scratch_shapes=[pltpu.VMEM((1, MLA_HEADS, LANES, BLOCK), F32), pltpu.VMEM((1, MLA_HEADS, 8, BLOCK), F32),
                        pltpu.VMEM((1, MLA_HEADS, 8, BLOCK), F32)],
        compiler_params=_params(("arbitrary",)),
        name="mla_meta",
    )(q, k, v, gate)


def _mid_kernel(osb_ref, omla_ref, h_ref, wo_ref, g_ref, w_ref, h1_ref, q_ref, kv_ref, gate_ref):
    rows = h_ref.shape[1]
    halves = [slice(0, rows // 2), slice(rows // 2, rows)] if rows >= 2 * BLOCK else [slice(0, rows)]
    ys = [_dot(osb_ref[0, r, :], wo_ref[0:512, :]) + _dot(omla_ref[0, r, :], wo_ref[512:1024, :]) for r in halves]
    for r, y in zip(halves, ys):
        h1 = h_ref[0, r, :] + y
        h1_ref[0, r, :] = h1
        xn = _rms(h1, g_ref[...]).astype(BF16)
        q_ref[0, r, :] = (_dot(xn, w_ref[:, 0:1024]) * LOG2E).astype(BF16)
        kv_ref[0, r, :] = _dot(xn, w_ref[:, 1024:1280]).astype(BF16)
        gate_ref[0, r, :] = _dot(xn, w_ref[:, 1280:2304]).astype(BF16)


def _mid(osb, omla, h, tile, wo, g, w):
    B, L, _ = h.shape
    row = lambda width: pl.BlockSpec((1, tile, width), lambda b, t: (b, t, 0))
    full = lambda a: pl.BlockSpec(a.shape, lambda b, t: (0,) * a.ndim)
    return pl.pallas_call(
        _mid_kernel,
        grid=(B, L // tile),
        in_specs=[row(512), row(512), row(D_MODEL), full(wo), full(g), full(w)],
        out_specs=[row(D_MODEL), row(1024), row(256), row(1024)],
        out_shape=[jax.ShapeDtypeStruct((B, L, D_MODEL), F32),
                   jax.ShapeDtypeStruct((B, L, 1024), BF16),
                   jax.ShapeDtypeStruct((B, L, 256), BF16),
                   jax.ShapeDtypeStruct((B, L, 1024), BF16)],
        compiler_params=_params(("parallel", "parallel")),
        name="mid",
    )(osb, omla, h, wo, g, w)


def _swa_kernel(sink_ref, q_ref, kvp_ref, kvc_ref, kvm_ref, g_ref, bb_ref, mb_ref, h_ref, wo_ref, fg_ref,
                out_ref, o_scr):
    n = pl.program_id(1)
    first = jnp.where(n > 0, 0, 1)
    in_cur = _iota((BLOCK, BLOCK), 0) <= _iota((BLOCK, BLOCK), 1)
    blocks_before = (n + 1).astype(F32) * float(BLOCK)
    pairs = SWA_HEADS // 2
    half = LANES // 2
    items = [(b, p) for b in range(q_ref.shape[0]) for p in range(pairs)]
    k_sl, v_sl = slice(0, LANES), slice(LANES, 2 * LANES)
    keys_of = lambda b, sl: jnp.concatenate([kvp_ref[b, :, sl], kvc_ref[b, :, sl], kvm_ref[0, N_PAD:, sl]], axis=0)

    def scores(b, p):
        q_stack = jnp.concatenate(_split_pair(q_ref[b, :, p * LANES:(p + 1) * LANES]), axis=0)
        return _dot_nt(keys_of(b, k_sl), q_stack)

    def probs(p, z_pair):
        p_cols, invs = [], []
        for hh in range(2):
            h = p + pairs * hh
            z = z_pair[:, hh * BLOCK:(hh + 1) * BLOCK]
            slope = 2.0 ** (-8.0 * (h + 1.0) / SWA_HEADS) * LOG2E
            s_band = jnp.where(in_cur, z[BLOCK:2 * BLOCK], z[:BLOCK]) - bb_ref[first, h]
            s_meta = z[2 * BLOCK:] - mb_ref[h] - slope * blocks_before
            sink = sink_ref[h] * LOG2E
            m = jnp.maximum(jnp.maximum(jnp.max(s_band, axis=0, keepdims=True),
                                        jnp.max(s_meta, axis=0, keepdims=True)), sink)
            p_band, p_meta = jnp.exp2(s_band - m), jnp.exp2(s_meta - m)
            denom = (jnp.sum(p_band, axis=0, keepdims=True) + jnp.sum(p_meta, axis=0, keepdims=True)
                     + jnp.exp2(sink - m))
            p_cols.append(jnp.concatenate(
                [jnp.where(in_cur, 0.0, p_band), jnp.where(in_cur, p_band, 0.0), p_meta], axis=0).astype(BF16))
            invs.append(1.0 / denom)
        return jnp.concatenate(p_cols, axis=1), jnp.concatenate(invs, axis=1)

    def values(b, p, p_pair, inv):
        o_t = lax.dot_general(keys_of(b, v_sl), p_pair, (((0,), (0,)), ((), ())), preferred_element_type=F32) * inv
        o = jnp.concatenate([o_t[:half, :BLOCK], o_t[half:, BLOCK:]], axis=0).T
        sl = slice(p * LANES, (p + 1) * LANES)
        o_scr[b, :, sl] = (o * _silu(g_ref[b, :, sl].astype(F32))).astype(BF16)

    zs, ps = {}, {}
    for step in range(len(items) + 2 * SWA_SKEW):
        if step < len(items):
            zs[step] = scores(*items[step])
        if 0 <= step - SWA_SKEW < len(items):
            ps[step - SWA_SKEW] = probs(items[step - SWA_SKEW][1], zs.pop(step - SWA_SKEW))
        t = step - 2 * SWA_SKEW
        if 0 <= t < len(items):
            b, p = items[t]
            values(b, p, *ps.pop(t))
            if p == pairs - 1:
                out_ref[b] = _rms(h_ref[b] + _dot(o_scr[b], wo_ref[...]), fg_ref[...])


def _swa_bias_tables():
    col, row = np.arange(BLOCK)[:, None], np.arange(BLOCK)[None, :]
    slopes = (2.0 ** (-8.0 * (np.arange(SWA_HEADS) + 1.0) / SWA_HEADS) * LOG2E)[:, None, None]
    band = slopes * np.where(col <= row, row - col, row - col + BLOCK)
    band_first = band + np.where(col <= row, 0.0, -NEG)
    meta = (slopes * (row - col))[:, N_PAD:, :]
    return jnp.asarray(np.stack([band, band_first]), F32), jnp.asarray(meta, F32)


def _swa_attn(sinks, q, kv, kv_meta, gate, h1, wo, final_g):
    B, L, _ = q.shape
    kvw = kv.shape[-1]
    nb = SWA_BATCH
    band_bias, meta_bias = _swa_bias_tables()
    row = lambda w: pl.BlockSpec((nb, BLOCK, w), lambda b, n, s: (b, n, 0))
    const = lambda a: pl.BlockSpec(a.shape, lambda b, n, s: (0,) * a.ndim)
    grid_spec = pltpu.PrefetchScalarGridSpec(
        num_scalar_prefetch=1,
        grid=(B // nb, L // BLOCK),
        in_specs=[row(SWA_WIDTH),
                  pl.BlockSpec((nb, BLOCK, kvw), lambda b, n, s: (b, jnp.maximum(n - 1, 0), 0)),
                  row(kvw), const(kv_meta), row(SWA_WIDTH), const(band_bias), const(meta_bias),
                  row(D_MODEL), const(wo), const(final_g)],
        out_specs=row(D_MODEL),
        scratch_shapes=[pltpu.VMEM((nb, BLOCK, SWA_WIDTH), BF16)],
    )
    return pl.pallas_call(
        _swa_kernel,
        grid_spec=grid_spec,
        out_shape=jax.ShapeDtypeStruct((B, L, D_MODEL), F32),
        compiler_params=_params(("parallel", "parallel")),
        name="swa_attn",
    )(sinks, q, kv, kv, kv_meta, gate, band_bias, meta_bias, h1, wo, final_g)


def _layer0_weights(w_in, w_uq, w_ukv):
    w_in, w_uq, w_ukv = w_in.astype(BF16), w_uq.astype(BF16), w_ukv.astype(BF16)
    q, k, v, g_sb = (w_in[:, i * 512:(i + 1) * 512] for i in range(4))
    c_q, c_kv = w_in[:, 2048:2304], w_in[:, 2304:2432]
    k_r, g_mla = w_in[:, 2432:2464], w_in[:, 2464:2976]
    half = MLA_ROPE // 2
    r1, r2 = k_r[:, :half], k_r[:, half:]
    z = lambda n: jnp.zeros((D_MODEL, n), w_in.dtype)
    kr_blk = jnp.concatenate([z(MLA_NOPE), r1, r2, z(LANES - MLA_NOPE - MLA_ROPE)], axis=1)
    w0 = jnp.concatenate([q * (SB_DIM ** -0.5), k, v, g_sb, g_mla, c_q, c_kv, kr_blk], axis=1)

    uq = w_uq.reshape(MLA_Q_LORA, MLA_HEADS, MLA_NOPE + MLA_ROPE)
    nope, u1, u2 = uq[..., :MLA_NOPE], uq[..., MLA_NOPE:MLA_NOPE + half], uq[..., MLA_NOPE + half:]
    zq = lambda n: jnp.zeros((MLA_Q_LORA, MLA_HEADS, n), w_uq.dtype)
    uq_main = jnp.concatenate([nope, u1, u2, zq(LANES - MLA_NOPE - MLA_ROPE)], axis=-1)
    uq_rot = jnp.concatenate([zq(MLA_NOPE), -u2, u1, zq(LANES - MLA_NOPE - MLA_ROPE)], axis=-1)
    wuq = jnp.concatenate([uq_main.reshape(MLA_Q_LORA, -1), uq_rot.reshape(MLA_Q_LORA, -1)], axis=1)

    ukv = w_ukv.reshape(MLA_KV_LORA, MLA_HEADS, MLA_NOPE + MLA_V)
    k_nope = jnp.concatenate([ukv[..., :MLA_NOPE],
                              jnp.zeros((MLA_KV_LORA, MLA_HEADS, LANES - MLA_NOPE), w_ukv.dtype)], axis=-1)
    wukv = jnp.concatenate([k_nope.reshape(MLA_KV_LORA, -1),
                            ukv[..., MLA_NOPE:].reshape(MLA_KV_LORA, -1)], axis=1)
    return w0, wuq, wukv


def _pair_heads(w, axis):
    shape = w.shape
    w = w.reshape(shape[:axis] + (SWA_KV_HEADS, SWA_HEADS // SWA_KV_HEADS, SWA_DIM) + shape[axis + 1:])
    return jnp.swapaxes(w, axis, axis + 1).reshape(shape)


def _layer1_weights(w_in, w_out):
    w_in, w_out = w_in.astype(BF16), w_out.astype(BF16)
    q, kv, g = w_in[:, :1024], w_in[:, 1024:1280], w_in[:, 1280:2304]
    w1 = jnp.concatenate([_pair_heads(q * (SWA_DIM ** -0.5), 1), kv, _pair_heads(g, 1)], axis=1)
    return w1, _pair_heads(w_out, 0)


def _rope_tables():
    half = MLA_ROPE // 2
    pos = np.arange(N_META + SEQ, dtype=np.float64)
    inv = ROPE_BASE ** (-np.arange(half, dtype=np.float64) / half)
    ang = pos[:, None] * inv[None, :]
    cos, sin = np.cos(ang), np.sin(ang)
    n = pos.shape[0]
    z = lambda w: np.zeros((n, w))
    tail = LANES - MLA_NOPE - MLA_ROPE
    c = np.concatenate([np.ones((n, MLA_NOPE)), cos, cos, z(tail)], axis=1)
    s = np.concatenate([z(MLA_NOPE), sin, sin, z(tail)], axis=1)
    s1 = np.concatenate([z(MLA_NOPE), -sin, z(half), z(tail)], axis=1)
    s2 = np.concatenate([z(MLA_NOPE), z(half), sin, z(tail)], axis=1)
    scale = (MLA_NOPE + MLA_ROPE) ** -0.5 * LOG2E
    tabs = (c * scale, s * scale, c, s1, s2)
    pad = lambda t: np.concatenate([np.zeros((N_PAD, LANES)), t[:N_META]], axis=0)
    return (tuple(jnp.asarray(pad(t), F32) for t in tabs), tuple(jnp.asarray(t[N_META:], F32) for t in tabs))


def _suffix_matrix(n):
    return jnp.asarray(np.where(np.arange(n)[:, None] >= np.arange(n)[None, :], -1.0, 0.0), BF16)


def kernel(x, meta, norm_g, final_g, ev_w_in, ev_q_norm_g, ev_kv_norm_g, ev_w_uq, ev_w_ukv,
           ev_w_out, od_w_in, od_sinks, od_w_out):
    w0, wuq, wukv = _layer0_weights(ev_w_in[0], ev_w_uq[0], ev_w_ukv[0])
    w1, wo1 = _layer1_weights(od_w_in[0], od_w_out[0])
    wo0 = ev_w_out[0].astype(BF16)
    tabs_meta, tabs_real = _rope_tables()
    u, u_meta = _suffix_matrix(ATT_TILE), _suffix_matrix(BLOCK)
    l0 = (norm_g[0:1], w0, ev_q_norm_g[0:1], ev_kv_norm_g[0:1], wuq, wukv)

    hm = jnp.concatenate([jnp.zeros((N_PAD, D_MODEL), x.dtype), meta.astype(x.dtype)], axis=0)[None]
    qsb_m, ksb_m, vsb_m, gate_m, qm_m, km_m, vm_m = _proj0(hm, BLOCK, *l0, tabs_meta)
    osb_m = _sb_meta(qsb_m, ksb_m, vsb_m, gate_m, u_meta)
    omla_m = _mla_meta(qm_m, km_m, vm_m, gate_m)
    _, _, kv1_m, _ = _mid(osb_m, omla_m, hm, BLOCK, wo0, norm_g[1:2], w1)

    q_sb, k_sb, v_sb, gate0, q_mla, k_mla, v_mla = _proj0(x, ROW_TILE, *l0, tabs_real)
    o_sb = _sb_attn(q_sb, k_sb, v_sb, ksb_m, vsb_m, gate0, u, u_meta)
    o_mla = _mla_attn(q_mla, k_mla, v_mla, km_m, vm_m, gate0)
    h1, q1, kv1, gate1 = _mid(o_sb, o_mla, x, ROW_TILE, wo0, norm_g[1:2], w1)
    return _swa_attn(od_sinks[0], q1, kv1, kv1_m, gate1, h1, wo1, final_g[None, :])
```

```python
import math

import numpy as np
import jax
import jax.numpy as jnp
from jax import lax
from jax.experimental import pallas as pl
from jax.experimental.pallas import tpu as pltpu

D_MODEL = 1024
SEQ = 2048
N_META = 16
BLOCK = 128
N_PAD = BLOCK - N_META
NORM_EPS = 1e-6
NEG = -1e30

SB_HEADS = 8
SB_DIM = 64
SB_WIDTH = SB_HEADS * SB_DIM
MLA_HEADS = 8
MLA_Q_LORA = 256
MLA_KV_LORA = 128
MLA_NOPE = 64
MLA_ROPE = 32
MLA_V = 64
MLA_WIDTH = MLA_HEADS * MLA_V
ROPE_BASE = 10000.0
SWA_HEADS = 16
SWA_KV_HEADS = 2
SWA_DIM = 64
SWA_WIDTH = SWA_HEADS * SWA_DIM

LANES = 128
ROW_TILE = 1024
ATT_TILE = 256
ATT_BATCH = 2
VMEM_LIMIT = 48 * 1024 * 1024
LOG2E = math.log2(math.e)
SWA_BATCH = 4
SB_SKEW = MLA_SKEW = SWA_SKEW = 2
DEAD_CARRY = -256.0
SOFTPLUS_CLAMP = 64.0

BF16 = jnp.bfloat16
F32 = jnp.float32


def _dot(a, b):
    return jnp.dot(a, b, preferred_element_type=F32)


def _dot_nt(a, b):
    return lax.dot_general(a, b, (((1,), (1,)), ((), ())), preferred_element_type=F32)


def _rms(x, g):
    ms = jnp.mean(x * x, axis=-1, keepdims=True)
    return x * lax.rsqrt(ms + NORM_EPS) * g


def _silu(g):
    return g * (1.0 / (1.0 + jnp.exp(-g)))


def _params(sem):
    return pltpu.CompilerParams(dimension_semantics=sem, vmem_limit_bytes=VMEM_LIMIT)


def _iota(shape, dim):
    return lax.broadcasted_iota(jnp.int32, shape, dim)


def _split_pair(x):
    lane = _iota(x.shape, 1)
    zero = jnp.zeros_like(x)
    return jnp.where(lane < LANES // 2, x, zero), jnp.where(lane >= LANES // 2, x, zero)


def _proj0_kernel(x_ref, g_ref, w_ref, qg_ref, kvg_ref, wuq_ref, wukv_ref,
                  cq_ref, sq_ref, ck_ref, sk1_ref, sk2_ref,
                  qsb_ref, ksb_ref, vsb_ref, gate_ref, qm_ref, km_ref, vm_ref):
    rows = x_ref.shape[1]
    halves = [slice(0, rows // 2), slice(rows // 2, rows)] if rows >= 2 * BLOCK else [slice(0, rows)]
    qk_w = MLA_HEADS * LANES
    for r in halves:
        xn = _rms(x_ref[0, r, :], g_ref[...]).astype(BF16)
        qsb_ref[0, r, :] = (_dot(xn, w_ref[:, 0:512]) * LOG2E).astype(BF16)
        ksb_ref[0, r, :] = _dot(xn, w_ref[:, 512:1024]).astype(BF16)
        vsb_ref[0, r, :] = _dot(xn, w_ref[:, 1024:1536]).astype(BF16)
        gate_ref[0, r, :] = _dot(xn, w_ref[:, 1536:2560]).astype(BF16)
        lat = _dot(xn, w_ref[:, 2560:3072])
        cqn = _rms(lat[:, 0:256], qg_ref[...]).astype(BF16)
        ckvn = _rms(lat[:, 256:384], kvg_ref[...]).astype(BF16)
        kr = lat[:, 384:512]
        k_rope = (kr * ck_ref[r, :] + pltpu.roll(kr, LANES - MLA_ROPE // 2, 1) * sk1_ref[r, :]
                  + pltpu.roll(kr, MLA_ROPE // 2, 1) * sk2_ref[r, :])
        q_all = _dot(cqn, wuq_ref[...])
        kv_all = _dot(ckvn, wukv_ref[...])
        cq = cq_ref[r, :]
        sq = sq_ref[r, :]
        for h in range(MLA_HEADS):
            lo, hi = h * LANES, (h + 1) * LANES
            qm_ref[0, r, lo:hi] = (q_all[:, lo:hi] * cq + q_all[:, qk_w + lo:qk_w + hi] * sq).astype(BF16)
            km_ref[0, r, lo:hi] = (kv_all[:, lo:hi] + k_rope).astype(BF16)
        vm_ref[0, r, :] = kv_all[:, qk_w:].astype(BF16)


def _proj0(h, tile, g, w, qg, kvg, wuq, wukv, tables):
    B, L, _ = h.shape
    row = lambda width: pl.BlockSpec((1, tile, width), lambda b, t: (b, t, 0))
    full = lambda a: pl.BlockSpec(a.shape, lambda b, t: (0,) * a.ndim)
    tab = pl.BlockSpec((tile, LANES), lambda b, t: (t, 0))
    out = lambda width: jax.ShapeDtypeStruct((B, L, width), BF16)
    return pl.pallas_call(
        _proj0_kernel,
        grid=(B, L // tile),
        in_specs=[row(D_MODEL), full(g), full(w), full(qg), full(kvg), full(wuq), full(wukv),
                  tab, tab, tab, tab, tab],
        out_specs=[row(512), row(512), row(512), row(1024), row(1024), row(1024), row(512)],
        out_shape=[out(512), out(512), out(512), out(1024), out(1024), out(1024), out(512)],
        compiler_params=_params(("parallel", "parallel")),
        name="proj0",
    )(h, g, w, qg, kvg, wuq, wukv, *tables)


def _sb_chunk(qh_ref, kv, u, mask, acc_ref, c_ref, first=False):
    items = [(b, h) for b in range(len(kv)) for h in range(SB_HEADS)]
    sl = lambda h: slice((h // 2) * LANES, (h // 2 + 1) * LANES)
    rep = lambda row: jnp.broadcast_to(row, c_ref.shape[2:])

    def scores(b, h):
        return _dot_nt(kv[b][0][:, sl(h)], qh_ref[b, h])

    def suffix(z):
        sp = jnp.maximum(jnp.log2(1.0 + jnp.exp2(jnp.minimum(z, SOFTPLUS_CLAMP))), z)
        if mask is not None:
            sp = jnp.where(mask, sp, 0.0)
        return _dot(u, sp.astype(BF16))

    def weights(b, h, z, sfx):
        total = sfx[0:1, :]
        if first:
            a = jnp.exp2(z + sfx)
        else:
            c = c_ref[b, h, 0:1, :]
            a = jnp.exp2(z + sfx + c)
        if mask is not None:
            a = jnp.where(mask, a, 0.0)
        av = lax.dot_general(kv[b][1][:, sl(h)], a.astype(BF16), (((0,), (0,)), ((), ())),
                             preferred_element_type=F32)
        if first:
            acc_ref[b, h], c_ref[b, h] = av, rep(total)
        else:
            acc_ref[b, h] += av
            c_ref[b, h] = rep(c + total)

    zs, sfxs = {}, {}
    for step in range(len(items) + 2 * SB_SKEW):
        if step < len(items):
            zs[step] = scores(*items[step])
        if 0 <= step - SB_SKEW < len(items):
            sfxs[step - SB_SKEW] = suffix(zs[step - SB_SKEW])
        if 0 <= step - 2 * SB_SKEW < len(items):
            weights(*items[step - 2 * SB_SKEW], zs.pop(step - 2 * SB_SKEW), sfxs.pop(step - 2 * SB_SKEW))


def _sb_prologue(q_ref, qh_ref):
    for b in range(q_ref.shape[0]):
        for p in range(SB_HEADS // 2):
            qh_ref[b, 2 * p], qh_ref[b, 2 * p + 1] = _split_pair(q_ref[b, :, p * LANES:(p + 1) * LANES])


def _sb_epilogue(acc_ref, g_ref, o_ref):
    for b in range(o_ref.shape[0]):
        for p in range(SB_HEADS // 2):
            sl = slice(p * LANES, (p + 1) * LANES)
            o = jnp.concatenate([acc_ref[b, 2 * p, :LANES // 2, :], acc_ref[b, 2 * p + 1, LANES // 2:, :]], axis=0).T
            o_ref[b, :, sl] = (o * _silu(g_ref[b, :, sl].astype(F32))).astype(BF16)


def _sb_kernel(q_ref, k_ref, v_ref, km_ref, vm_ref, g_ref, u_ref, um_ref, o_ref, qh_ref, acc_ref, c_ref):
    i = pl.program_id(1)
    T = ATT_TILE
    nb = q_ref.shape[0]
    _sb_prologue(q_ref, qh_ref)

    def real_chunk(j, mask, first=False):
        start = pl.multiple_of(j * T, T)
        kv = [(k_ref[b, pl.ds(start, T), :], v_ref[b, pl.ds(start, T), :]) for b in range(nb)]
        _sb_chunk(qh_ref, kv, u_ref[...], mask, acc_ref, c_ref, first)

    real_chunk(i, _iota((T, T), 0) < _iota((T, T), 1), first=True)

    def alive():
        return jnp.max(c_ref[...]) > DEAD_CARRY

    def earlier(state):
        idx, _ = state
        real_chunk(i - 1 - idx, None)
        return idx + 1, alive()

    _, still_alive = lax.while_loop(lambda state: (state[0] < i) & state[1], earlier, (jnp.int32(0), alive()))

    @pl.when(still_alive)
    def _():
        _sb_chunk(qh_ref, [(km_ref[0, N_PAD:, :], vm_ref[0, N_PAD:, :])] * nb, um_ref[...], None, acc_ref, c_ref)

    _sb_epilogue(acc_ref, g_ref, o_ref)


def _sb_attn(q, k, v, k_meta, v_meta, gate, u, u_meta):
    B, L, _ = q.shape
    T, nb = ATT_TILE, ATT_BATCH
    tile = pl.BlockSpec((nb, T, SB_WIDTH), lambda b, i: (b, i, 0))
    seq = pl.BlockSpec((nb, L, SB_WIDTH), lambda b, i: (b, 0, 0))
    meta = pl.BlockSpec((1, BLOCK, SB_WIDTH), lambda b, i: (0, 0, 0))
    const = lambda a: pl.BlockSpec(a.shape, lambda b, i: (0, 0))
    return pl.pallas_call(
        _sb_kernel,
        grid=(B // nb, L // T),
        in_specs=[tile, seq, seq, meta, meta, tile, const(u), const(u_meta)],
        out_specs=tile,
        out_shape=jax.ShapeDtypeStruct((B, L, SB_WIDTH), BF16),
        scratch_shapes=[pltpu.VMEM((nb, SB_HEADS, T, LANES), BF16), pltpu.VMEM((nb, SB_HEADS, LANES, T), F32),
                        pltpu.VMEM((nb, SB_HEADS, 8, T), F32)],
        compiler_params=_params(("parallel", "arbitrary")),
        name="sb_attn",
    )(q, k, v, k_meta, v_meta, gate, u, u_meta)


def _sb_meta_kernel(q_ref, k_ref, v_ref, g_ref, u_ref, o_ref, qh_ref, acc_ref, c_ref):
    _sb_prologue(q_ref, qh_ref)
    key, query = _iota((BLOCK, BLOCK), 0), _iota((BLOCK, BLOCK), 1)
    _sb_chunk(qh_ref, [(k_ref[0], v_ref[0])], u_ref[...], (key < query) & (key >= N_PAD), acc_ref, c_ref, first=True)
    _sb_epilogue(acc_ref, g_ref, o_ref)


def _sb_meta(q, k, v, gate, u_meta):
    blk = pl.BlockSpec((1, BLOCK, SB_WIDTH), lambda i: (0, 0, 0))
    return pl.pallas_call(
        _sb_meta_kernel,
        grid=(1,),
        in_specs=[blk, blk, blk, blk, pl.BlockSpec(u_meta.shape, lambda i: (0, 0))],
        out_specs=blk,
        out_shape=jax.ShapeDtypeStruct((1, BLOCK, SB_WIDTH), BF16),
        scratch_shapes=[pltpu.VMEM((1, SB_HEADS, BLOCK, LANES), BF16), pltpu.VMEM((1, SB_HEADS, LANES, BLOCK), F32),
                        pltpu.VMEM((1, SB_HEADS, 8, BLOCK), F32)],
        compiler_params=_params(("arbitrary",)),
        name="sb_meta",
    )(q, k, v, gate, u_meta)


def _mla_chunk(q_ref, kv, mask, acc_ref, m_ref, l_ref, first=False):
    items = [(b, h) for b in range(len(kv)) for h in range(MLA_HEADS)]
    rep = lambda row: jnp.broadcast_to(row, m_ref.shape[2:])

    def scores(b, h):
        hs = slice(h * LANES, (h + 1) * LANES)
        s = _dot_nt(kv[b][0][:, hs], q_ref[b, :, hs])
        return s if mask is None else jnp.where(mask, s, NEG)

    def probs(b, h, s):
        m_new = jnp.max(s, axis=0, keepdims=True)
        if first:
            alpha = None
        else:
            m_old = m_ref[b, h, 0:1, :]
            m_new = jnp.maximum(m_old, m_new)
            alpha = jnp.exp2(m_old - m_new)
        m_ref[b, h] = rep(m_new)
        p = jnp.exp2(s - m_new)
        return p.astype(BF16), jnp.sum(p, axis=0, keepdims=True), alpha

    def update(b, h, p, l_new, alpha):
        vs = slice((h // 2) * LANES, (h // 2 + 1) * LANES)
        pv = lax.dot_general(kv[b][1][:, vs], p, (((0,), (0,)), ((), ())), preferred_element_type=F32)
        if first:
            acc_ref[b, h], l_ref[b, h] = pv, rep(l_new)
        else:
            acc_ref[b, h] = alpha * acc_ref[b, h] + pv
            l_ref[b, h] = rep(alpha * l_ref[b, h, 0:1, :] + l_new)

    ss, ps = {}, {}
    for step in range(len(items) + 2 * MLA_SKEW):
        if step < len(items):
            ss[step] = scores(*items[step])
        if 0 <= step - MLA_SKEW < len(items):
            ps[step - MLA_SKEW] = probs(*items[step - MLA_SKEW], ss.pop(step - MLA_SKEW))
        if 0 <= step - 2 * MLA_SKEW < len(items):
            update(*items[step - 2 * MLA_SKEW], *ps.pop(step - 2 * MLA_SKEW))


def _mla_finish(acc_ref, l_ref, g_ref, o_ref):
    half = LANES // 2
    for b in range(o_ref.shape[0]):
        for p in range(MLA_HEADS // 2):
            sl = slice(p * LANES, (p + 1) * LANES)
            lo = acc_ref[b, 2 * p, :half, :] * (1.0 / l_ref[b, 2 * p, 0:1, :])
            hi = acc_ref[b, 2 * p + 1, half:, :] * (1.0 / l_ref[b, 2 * p + 1, 0:1, :])
            o = jnp.concatenate([lo, hi], axis=0).T
            gate = g_ref[b, :, MLA_WIDTH + p * LANES:MLA_WIDTH + (p + 1) * LANES].astype(F32)
            o_ref[b, :, sl] = (o * _silu(gate)).astype(BF16)


def _mla_kernel(q_ref, k_ref, v_ref, km_ref, vm_ref, g_ref, o_ref, acc_ref, m_ref, l_ref):
    i = pl.program_id(1)
    T = ATT_TILE
    nb = q_ref.shape[0]

    def real_kv(j):
        start = pl.multiple_of(j * T, T)
        return [(k_ref[b, pl.ds(start, T), :], v_ref[b, pl.ds(start, T), :]) for b in range(nb)]

    def first_chunk(extra):
        parts = [real_kv(i)] + ([real_kv(0)] if extra else []) + [[(km_ref[0, N_PAD:, :], vm_ref[0, N_PAD:, :])] * nb]
        kv = [tuple(jnp.concatenate([part[b][j] for part in parts], axis=0) for j in range(2)) for b in range(nb)]
        masks = ([_iota((T, T), 0) <= _iota((T, T), 1)] + ([_iota((T, T), 0) >= 0] if extra else [])
                 + [_iota((N_META, T), 0) >= 0])
        _mla_chunk(q_ref, kv, jnp.concatenate(masks, axis=0), acc_ref, m_ref, l_ref, first=True)

    odd = i % 2

    @pl.when(odd == 1)
    def _():
        first_chunk(True)

    @pl.when(odd == 0)
    def _():
        first_chunk(False)

    def earlier(idx, carry):
        start = pl.multiple_of((odd + 2 * idx) * T, T)
        kv = [(k_ref[b, pl.ds(start, 2 * T), :], v_ref[b, pl.ds(start, 2 * T), :]) for b in range(nb)]
        _mla_chunk(q_ref, kv, None, acc_ref, m_ref, l_ref)
        return carry

    lax.fori_loop(0, i // 2, earlier, 0)
    _mla_finish(acc_ref, l_ref, g_ref, o_ref)


def _mla_attn(q, k, v, k_meta, v_meta, gate):
    B, L, _ = q.shape
    T, nb = ATT_TILE, ATT_BATCH
    qk_w = MLA_HEADS * LANES
    return pl.pallas_call(
        _mla_kernel,
        grid=(B // nb, L // T),
        in_specs=[pl.BlockSpec((nb, T, qk_w), lambda b, i: (b, i, 0)),
                  pl.BlockSpec((nb, L, qk_w), lambda b, i: (b, 0, 0)),
                  pl.BlockSpec((nb, L, MLA_WIDTH), lambda b, i: (b, 0, 0)),
                  pl.BlockSpec((1, BLOCK, qk_w), lambda b, i: (0, 0, 0)),
                  pl.BlockSpec((1, BLOCK, MLA_WIDTH), lambda b, i: (0, 0, 0)),
                  pl.BlockSpec((nb, T, 2 * MLA_WIDTH), lambda b, i: (b, i, 0))],
        out_specs=pl.BlockSpec((nb, T, MLA_WIDTH), lambda b, i: (b, i, 0)),
        out_shape=jax.ShapeDtypeStruct((B, L, MLA_WIDTH), BF16),
        scratch_shapes=[pltpu.VMEM((nb, MLA_HEADS, LANES, T), F32), pltpu.VMEM((nb, MLA_HEADS, 8, T), F32),
                        pltpu.VMEM((nb, MLA_HEADS, 8, T), F32)],
        compiler_params=_params(("parallel", "arbitrary")),
        name="mla_attn",
    )(q, k, v, k_meta, v_meta, gate)


def _mla_meta_kernel(q_ref, k_ref, v_ref, g_ref, o_ref, acc_ref, m_ref, l_ref):
    key, query = _iota((BLOCK, BLOCK), 0), _iota((BLOCK, BLOCK), 1)
    _mla_chunk(q_ref, [(k_ref[0], v_ref[0])], (key <= query) & (key >= N_PAD), acc_ref, m_ref, l_ref, first=True)
    _mla_finish(acc_ref, l_ref, g_ref, o_ref)


def _mla_meta(q, k, v, gate):
    qk_w = MLA_HEADS * LANES
    blk = lambda w: pl.BlockSpec((1, BLOCK, w), lambda i: (0, 0, 0))
    return pl.pallas_call(
        _mla_meta_kernel,
        grid=(1,),
        in_specs=[blk(qk_w), blk(qk_w), blk(MLA_WIDTH), blk(2 * MLA_WIDTH)],
        out_specs=blk(MLA_WIDTH),
        out_shape=jax.ShapeDtypeStruct((1, BLOCK, MLA_WIDTH), BF16),
        scratch_shapes=[pltpu.VMEM((1, MLA_HEADS, LANES, BLOCK), F32), pltpu.VMEM((1, MLA_HEADS, 8, BLOCK), F32),
                        pltpu.VMEM((1, MLA_HEADS, 8, BLOCK), F32)],
        compiler_params=_params(("arbitrary",)),
        name="mla_meta",
    )(q, k, v, gate)


def _mid_kernel(osb_ref, omla_ref, h_ref, wo_ref, g_ref, w_ref, h1_ref, q_ref, kv_ref, gate_ref):
    rows = h_ref.shape[1]
    halves = [slice(0, rows // 2), slice(rows // 2, rows)] if rows >= 2 * BLOCK else [slice(0, rows)]
    ys = [_dot(osb_ref[0, r, :], wo_ref[0:512, :]) + _dot(omla_ref[0, r, :], wo_ref[512:1024, :]) for r in halves]
    for r, y in zip(halves, ys):
        h1 = h_ref[0, r, :] + y
        h1_ref[0, r, :] = h1
        xn = _rms(h1, g_ref[...]).astype(BF16)
        q_ref[0, r, :] = (_dot(xn, w_ref[:, 0:1024]) * LOG2E).astype(BF16)
        kv_ref[0, r, :] = _dot(xn, w_ref[:, 1024:1280]).astype(BF16)
        gate_ref[0, r, :] = _dot(xn, w_ref[:, 1280:2304]).astype(BF16)


def _mid(osb, omla, h, tile, wo, g, w):
    B, L, _ = h.shape
    row = lambda width: pl.BlockSpec((1, tile, width), lambda b, t: (b, t, 0))
    full = lambda a: pl.BlockSpec(a.shape, lambda b, t: (0,) * a.ndim)
    return pl.pallas_call(
        _mid_kernel,
        grid=(B, L // tile),
        in_specs=[row(512), row(512), row(D_MODEL), full(wo), full(g), full(w)],
        out_specs=[row(D_MODEL), row(1024), row(256), row(1024)],
        out_shape=[jax.ShapeDtypeStruct((B, L, D_MODEL), F32),
                   jax.ShapeDtypeStruct((B, L, 1024), BF16),
                   jax.ShapeDtypeStruct((B, L, 256), BF16),
                   jax.ShapeDtypeStruct((B, L, 1024), BF16)],
        compiler_params=_params(("parallel", "parallel")),
        name="mid",
    )(osb, omla, h, wo, g, w)


def _swa_kernel(sink_ref, q_ref, kvp_ref, kvc_ref, kvm_ref, g_ref, bb_ref, mb_ref, h_ref, wo_ref, fg_ref,
                out_ref, o_scr):
    n = pl.program_id(1)
    first = jnp.where(n > 0, 0, 1)
    in_cur = _iota((BLOCK, BLOCK), 0) <= _iota((BLOCK, BLOCK), 1)
    blocks_before = (n + 1).astype(F32) * float(BLOCK)
    pairs = SWA_HEADS // 2
    half = LANES // 2
    items = [(b, p) for b in range(q_ref.shape[0]) for p in range(pairs)]
    k_sl, v_sl = slice(0, LANES), slice(LANES, 2 * LANES)
    keys_of = lambda b, sl: jnp.concatenate([kvp_ref[b, :, sl], kvc_ref[b, :, sl], kvm_ref[0, N_PAD:, sl]], axis=0)

    def scores(b, p):
        q_stack = jnp.concatenate(_split_pair(q_ref[b, :, p * LANES:(p + 1) * LANES]), axis=0)
        return _dot_nt(keys_of(b, k_sl), q_stack)

    def probs(p, z_pair):
        p_cols, invs = [], []
        for hh in range(2):
            h = p + pairs * hh
            z = z_pair[:, hh * BLOCK:(hh + 1) * BLOCK]
            slope = 2.0 ** (-8.0 * (h + 1.0) / SWA_HEADS) * LOG2E
            s_band = jnp.where(in_cur, z[BLOCK:2 * BLOCK], z[:BLOCK]) - bb_ref[first, h]
            s_meta = z[2 * BLOCK:] - mb_ref[h] - slope * blocks_before
            sink = sink_ref[h] * LOG2E
            m = jnp.maximum(jnp.maximum(jnp.max(s_band, axis=0, keepdims=True),
                                        jnp.max(s_meta, axis=0, keepdims=True)), sink)
            p_band, p_meta = jnp.exp2(s_band - m), jnp.exp2(s_meta - m)
            denom = (jnp.sum(p_band, axis=0, keepdims=True) + jnp.sum(p_meta, axis=0, keepdims=True)
                     + jnp.exp2(sink - m))
            p_cols.append(jnp.concatenate(
                [jnp.where(in_cur, 0.0, p_band), jnp.where(in_cur, p_band, 0.0), p_meta], axis=0).astype(BF16))
            invs.append(1.0 / denom)
        return jnp.concatenate(p_cols, axis=1), jnp.concatenate(invs, axis=1)

    def values(b, p, p_pair, inv):
        o_t = lax.dot_general(keys_of(b, v_sl), p_pair, (((0,), (0,)), ((), ())), preferred_element_type=F32) * inv
        o = jnp.concatenate([o_t[:half, :BLOCK], o_t[half:, BLOCK:]], axis=0).T
        sl = slice(p * LANES, (p + 1) * LANES)
        o_scr[b, :, sl] = (o * _silu(g_ref[b, :, sl].astype(F32))).astype(BF16)

    zs, ps = {}, {}
    for step in range(len(items) + 2 * SWA_SKEW):
        if step < len(items):
            zs[step] = scores(*items[step])
        if 0 <= step - SWA_SKEW < len(items):
            ps[step - SWA_SKEW] = probs(items[step - SWA_SKEW][1], zs.pop(step - SWA_SKEW))
        t = step - 2 * SWA_SKEW
        if 0 <= t < len(items):
            b, p = items[t]
            values(b, p, *ps.pop(t))
            if p == pairs - 1:
                out_ref[b] = _rms(h_ref[b] + _dot(o_scr[b], wo_ref[...]), fg_ref[...])


def _swa_bias_tables():
    col, row = np.arange(BLOCK)[:, None], np.arange(BLOCK)[None, :]
    slopes = (2.0 ** (-8.0 * (np.arange(SWA_HEADS) + 1.0) / SWA_HEADS) * LOG2E)[:, None, None]
    band = slopes * np.where(col <= row, row - col, row - col + BLOCK)
    band_first = band + np.where(col <= row, 0.0, -NEG)
    meta = (slopes * (row - col))[:, N_PAD:, :]
    return jnp.asarray(np.stack([band, band_first]), F32), jnp.asarray(meta, F32)


def _swa_attn(sinks, q, kv, kv_meta, gate, h1, wo, final_g):
    B, L, _ = q.shape
    kvw = kv.shape[-1]
    nb = SWA_BATCH
    band_bias, meta_bias = _swa_bias_tables()
    row = lambda w: pl.BlockSpec((nb, BLOCK, w), lambda b, n, s: (b, n, 0))
    const = lambda a: pl.BlockSpec(a.shape, lambda b, n, s: (0,) * a.ndim)
    grid_spec = pltpu.PrefetchScalarGridSpec(
        num_scalar_prefetch=1,
        grid=(B // nb, L // BLOCK),
        in_specs=[row(SWA_WIDTH),
                  pl.BlockSpec((nb, BLOCK, kvw), lambda b, n, s: (b, jnp.maximum(n - 1, 0), 0)),
                  row(kvw), const(kv_meta), row(SWA_WIDTH), const(band_bias), const(meta_bias),
                  row(D_MODEL), const(wo), const(final_g)],
        out_specs=row(D_MODEL),
        scratch_shapes=[pltpu.VMEM((nb, BLOCK, SWA_WIDTH), BF16)],
    )
    return pl.pallas_call(
        _swa_kernel,
        grid_spec=grid_spec,
        out_shape=jax.ShapeDtypeStruct((B, L, D_MODEL), F32),
        compiler_params=_params(("parallel", "parallel")),
        name="swa_attn",
    )(sinks, q, kv, kv, kv_meta, gate, band_bias, meta_bias, h1, wo, final_g)


def _layer0_weights(w_in, w_uq, w_ukv):
    w_in, w_uq, w_ukv = w_in.astype(BF16), w_uq.astype(BF16), w_ukv.astype(BF16)
    q, k, v, g_sb = (w_in[:, i * 512:(i + 1) * 512] for i in range(4))
    c_q, c_kv = w_in[:, 2048:2304], w_in[:, 2304:2432]
    k_r, g_mla = w_in[:, 2432:2464], w_in[:, 2464:2976]
    half = MLA_ROPE // 2
    r1, r2 = k_r[:, :half], k_r[:, half:]
    z = lambda n: jnp.zeros((D_MODEL, n), w_in.dtype)
    kr_blk = jnp.concatenate([z(MLA_NOPE), r1, r2, z(LANES - MLA_NOPE - MLA_ROPE)], axis=1)
    w0 = jnp.concatenate([q * (SB_DIM ** -0.5), k, v, g_sb, g_mla, c_q, c_kv, kr_blk], axis=1)

    uq = w_uq.reshape(MLA_Q_LORA, MLA_HEADS, MLA_NOPE + MLA_ROPE)
    nope, u1, u2 = uq[..., :MLA_NOPE], uq[..., MLA_NOPE:MLA_NOPE + half], uq[..., MLA_NOPE + half:]
    zq = lambda n: jnp.zeros((MLA_Q_LORA, MLA_HEADS, n), w_uq.dtype)
    uq_main = jnp.concatenate([nope, u1, u2, zq(LANES - MLA_NOPE - MLA_ROPE)], axis=-1)
    uq_rot = jnp.concatenate([zq(MLA_NOPE), -u2, u1, zq(LANES - MLA_NOPE - MLA_ROPE)], axis=-1)
    wuq = jnp.concatenate([uq_main.reshape(MLA_Q_LORA, -1), uq_rot.reshape(MLA_Q_LORA, -1)], axis=1)

    ukv = w_ukv.reshape(MLA_KV_LORA, MLA_HEADS, MLA_NOPE + MLA_V)
    k_nope = jnp.concatenate([ukv[..., :MLA_NOPE],
                              jnp.zeros((MLA_KV_LORA, MLA_HEADS, LANES - MLA_NOPE), w_ukv.dtype)], axis=-1)
    wukv = jnp.concatenate([k_nope.reshape(MLA_KV_LORA, -1),
                            ukv[..., MLA_NOPE:].reshape(MLA_KV_LORA, -1)], axis=1)
    return w0, wuq, wukv


def _pair_heads(w, axis):
    shape = w.shape
    w = w.reshape(shape[:axis] + (SWA_KV_HEADS, SWA_HEADS // SWA_KV_HEADS, SWA_DIM) + shape[axis + 1:])
    return jnp.swapaxes(w, axis, axis + 1).reshape(shape)


def _layer1_weights(w_in, w_out):
    w_in, w_out = w_in.astype(BF16), w_out.astype(BF16)
    q, kv, g = w_in[:, :1024], w_in[:, 1024:1280], w_in[:, 1280:2304]
    w1 = jnp.concatenate([_pair_heads(q * (SWA_DIM ** -0.5), 1), kv, _pair_heads(g, 1)], axis=1)
    return w1, _pair_heads(w_out, 0)


def _rope_tables():
    half = MLA_ROPE // 2
    pos = np.arange(N_META + SEQ, dtype=np.float64)
    inv = ROPE_BASE ** (-np.arange(half, dtype=np.float64) / half)
    ang = pos[:, None] * inv[None, :]
    cos, sin = np.cos(ang), np.sin(ang)
    n = pos.shape[0]
    z = lambda w: np.zeros((n, w))
    tail = LANES - MLA_NOPE - MLA_ROPE
    c = np.concatenate([np.ones((n, MLA_NOPE)), cos, cos, z(tail)], axis=1)
    s = np.concatenate([z(MLA_NOPE), sin, sin, z(tail)], axis=1)
    s1 = np.concatenate([z(MLA_NOPE), -sin, z(half), z(tail)], axis=1)
    s2 = np.concatenate([z(MLA_NOPE), z(half), sin, z(tail)], axis=1)
    scale = (MLA_NOPE + MLA_ROPE) ** -0.5 * LOG2E
    tabs = (c * scale, s * scale, c, s1, s2)
    pad = lambda t: np.concatenate([np.zeros((N_PAD, LANES)), t[:N_META]], axis=0)
    return (tuple(jnp.asarray(pad(t), F32) for t in tabs), tuple(jnp.asarray(t[N_META:], F32) for t in tabs))


def _suffix_matrix(n):
    return jnp.asarray(np.where(np.arange(n)[:, None] <= np.arange(n)[None, :], -1.0, 0.0), BF16)


def kernel(x, meta, norm_g, final_g, ev_w_in, ev_q_norm_g, ev_kv_norm_g, ev_w_uq, ev_w_ukv,
           ev_w_out, od_w_in, od_sinks, od_w_out):
    w0, wuq, wukv = _layer0_weights(ev_w_in[0], ev_w_uq[0], ev_w_ukv[0])
    w1, wo1 = _layer1_weights(od_w_in[0], od_w_out[0])
    wo0 = ev_w_out[0].astype(BF16)
    tabs_meta, tabs_real = _rope_tables()
    u, u_block, u_meta = _suffix_matrix(ATT_TILE), _suffix_matrix(BLOCK), _suffix_matrix(N_META)
    l0 = (norm_g[0:1], w0, ev_q_norm_g[0:1], ev_kv_norm_g[0:1], wuq, wukv)

    hm = jnp.concatenate([jnp.zeros((N_PAD, D_MODEL), x.dtype), meta.astype(x.dtype)], axis=0)[None]
    qsb_m, ksb_m, vsb_m, gate_m, qm_m, km_m, vm_m = _proj0(hm, BLOCK, *l0, tabs_meta)
    osb_m = _sb_meta(qsb_m, ksb_m, vsb_m, gate_m, u_block)
    omla_m = _mla_meta(qm_m, km_m, vm_m, gate_m)
    _, _, kv1_m, _ = _mid(osb_m, omla_m, hm, BLOCK, wo0, norm_g[1:2], w1)

    q_sb, k_sb, v_sb, gate0, q_mla, k_mla, v_mla = _proj0(x, ROW_TILE, *l0, tabs_real)
    o_sb = _sb_attn(q_sb, k_sb, v_sb, ksb_m, vsb_m, gate0, u, u_meta)
    o_mla = _mla_attn(q_mla, k_mla, v_mla, km_m, vm_m, gate0)
    h1, q1, kv1, gate1 = _mid(o_sb, o_mla, x, ROW_TILE, wo0, norm_g[1:2], w1)
    return _swa_attn(od_sinks[0], q1, kv1, kv1_m, gate1, h1, wo1, final_g[None, :])
```

```python
import math

import numpy as np
import jax
import jax.numpy as jnp
from jax import lax
from jax.experimental import pallas as pl
from jax.experimental.pallas import tpu as pltpu

D_MODEL = 1024
SEQ = 2048
N_META = 16
BLOCK = 128
N_PAD = BLOCK - N_META
NORM_EPS = 1e-6
NEG = -1e30

SB_HEADS = 8
SB_DIM = 64
SB_WIDTH = SB_HEADS * SB_DIM
MLA_HEADS = 8
MLA_Q_LORA = 256
MLA_KV_LORA = 128
MLA_NOPE = 64
MLA_ROPE = 32
MLA_V = 64
MLA_WIDTH = MLA_HEADS * MLA_V
ROPE_BASE = 10000.0
SWA_HEADS = 16
SWA_KV_HEADS = 2
SWA_DIM = 64
SWA_WIDTH = SWA_HEADS * SWA_DIM

LANES = 128
ROW_TILE = 1024
ATT_TILE = 256
ATT_BATCH = 2
VMEM_LIMIT = 48 * 1024 * 1024
LOG2E = math.log2(math.e)
SWA_BATCH = 8
SB_SKEW = MLA_SKEW = SWA_SKEW = 2
DEAD_CARRY = -256.0
SOFTPLUS_CLAMP = 64.0

BF16 = jnp.bfloat16
F32 = jnp.float32


def _dot(a, b):
    return jnp.dot(a, b, preferred_element_type=F32)


def _dot_nt(a, b):
    return lax.dot_general(a, b, (((1,), (1,)), ((), ())), preferred_element_type=F32)


def _rms(x, g):
    ms = jnp.mean(x * x, axis=-1, keepdims=True)
    return x * lax.rsqrt(ms + NORM_EPS) * g


def _silu(g):
    return g * (1.0 / (1.0 + jnp.exp(-g)))


def _params(sem):
    return pltpu.CompilerParams(dimension_semantics=sem, vmem_limit_bytes=VMEM_LIMIT)


def _iota(shape, dim):
    return lax.broadcasted_iota(jnp.int32, shape, dim)


def _split_pair(x):
    lane = _iota(x.shape, 1)
    zero = jnp.zeros_like(x)
    return jnp.where(lane < LANES // 2, x, zero), jnp.where(lane >= LANES // 2, x, zero)


def _proj0_kernel(x_ref, g_ref, w_ref, qg_ref, kvg_ref, wuq_ref, wukv_ref,
                  cq_ref, sq_ref, ck_ref, sk1_ref, sk2_ref,
                  qsb_ref, ksb_ref, vsb_ref, gate_ref, qm_ref, km_ref, vm_ref):
    rows = x_ref.shape[1]
    halves = [slice(0, rows // 2), slice(rows // 2, rows)] if rows >= 2 * BLOCK else [slice(0, rows)]
    qk_w = MLA_HEADS * LANES
    for r in halves:
        xn = _rms(x_ref[0, r, :], g_ref[...]).astype(BF16)
        qsb_ref[0, r, :] = (_dot(xn, w_ref[:, 0:512]) * LOG2E).astype(BF16)
        ksb_ref[0, r, :] = _dot(xn, w_ref[:, 512:1024]).astype(BF16)
        vsb_ref[0, r, :] = _dot(xn, w_ref[:, 1024:1536]).astype(BF16)
        gate_ref[0, r, :] = _dot(xn, w_ref[:, 1536:2560]).astype(BF16)
        lat = _dot(xn, w_ref[:, 2560:3072])
        cqn = _rms(lat[:, 0:256], qg_ref[...]).astype(BF16)
        ckvn = _rms(lat[:, 256:384], kvg_ref[...]).astype(BF16)
        kr = lat[:, 384:512]
        k_rope = (kr * ck_ref[r, :] + pltpu.roll(kr, LANES - MLA_ROPE // 2, 1) * sk1_ref[r, :]
                  + pltpu.roll(kr, MLA_ROPE // 2, 1) * sk2_ref[r, :])
        q_all = _dot(cqn, wuq_ref[...])
        kv_all = _dot(ckvn, wukv_ref[...])
        cq = cq_ref[r, :]
        sq = sq_ref[r, :]
        for h in range(MLA_HEADS):
            lo, hi = h * LANES, (h + 1) * LANES
            qm_ref[0, r, lo:hi] = (q_all[:, lo:hi] * cq + q_all[:, qk_w + lo:qk_w + hi] * sq).astype(BF16)
            km_ref[0, r, lo:hi] = (kv_all[:, lo:hi] + k_rope).astype(BF16)
        vm_ref[0, r, :] = kv_all[:, qk_w:].astype(BF16)


def _proj0(h, tile, g, w, qg, kvg, wuq, wukv, tables):
    B, L, _ = h.shape
    assert L % tile == 0, (L, tile)
    row = lambda width: pl.BlockSpec((1, tile, width), lambda b, t: (b, t, 0))
    full = lambda a: pl.BlockSpec(a.shape, lambda b, t: (0,) * a.ndim)
    tab = pl.BlockSpec((tile, LANES), lambda b, t: (t, 0))
    out = lambda width: jax.ShapeDtypeStruct((B, L, width), BF16)
    return pl.pallas_call(
        _proj0_kernel,
        grid=(B, L // tile),
        in_specs=[row(D_MODEL), full(g), full(w), full(qg), full(kvg), full(wuq), full(wukv),
                  tab, tab, tab, tab, tab],
        out_specs=[row(512), row(512), row(512), row(1024), row(1024), row(1024), row(512)],
        out_shape=[out(512), out(512), out(512), out(1024), out(1024), out(1024), out(512)],
        compiler_params=_params(("parallel", "parallel")),
        name="proj0",
    )(h, g, w, qg, kvg, wuq, wukv, *tables)


def _sb_chunks(qh_ref, chunks, acc_ref, c_ref):
    nb = len(chunks[0][0])
    items = [(ci, b, h) for ci in range(len(chunks)) for b in range(nb) for h in range(SB_HEADS)]
    sl = lambda h: slice((h // 2) * LANES, (h // 2 + 1) * LANES)
    rep = lambda row: jnp.broadcast_to(row, c_ref.shape[2:])

    def scores(ci, b, h):
        return _dot_nt(chunks[ci][0][b][0][:, sl(h)], qh_ref[b, h])

    def suffix(ci, z):
        _, u, mask, _ = chunks[ci]
        sp = jnp.maximum(jnp.log2(1.0 + jnp.exp2(jnp.minimum(z, SOFTPLUS_CLAMP))), z)
        if mask is not None:
            sp = jnp.where(mask, sp, 0.0)
        return _dot(u, sp.astype(BF16))

    def weights(ci, b, h, z, sfx):
        kv, _, mask, first = chunks[ci]
        total = sfx[0:1, :]
        if first:
            a = jnp.exp2(z + sfx)
        else:
            c = c_ref[b, h, 0:1, :]
            a = jnp.exp2(z + sfx + c)
        if mask is not None:
            a = jnp.where(mask, a, 0.0)
        av = lax.dot_general(kv[b][1][:, sl(h)], a.astype(BF16), (((0,), (0,)), ((), ())),
                             preferred_element_type=F32)
        if first:
            acc_ref[b, h], c_ref[b, h] = av, rep(total)
        else:
            acc_ref[b, h] += av
            c_ref[b, h] = rep(c + total)

    zs, sfxs = {}, {}
    for step in range(len(items) + 2 * SB_SKEW):
        if step < len(items):
            zs[step] = scores(*items[step])
        if 0 <= step - SB_SKEW < len(items):
            sfxs[step - SB_SKEW] = suffix(items[step - SB_SKEW][0], zs[step - SB_SKEW])
        if 0 <= step - 2 * SB_SKEW < len(items):
            weights(*items[step - 2 * SB_SKEW], zs.pop(step - 2 * SB_SKEW), sfxs.pop(step - 2 * SB_SKEW))


def _sb_prologue(q_ref, qh_ref):
    for b in range(q_ref.shape[0]):
        for p in range(SB_HEADS // 2):
            qh_ref[b, 2 * p], qh_ref[b, 2 * p + 1] = _split_pair(q_ref[b, :, p * LANES:(p + 1) * LANES])


def _sb_epilogue(acc_ref, g_ref, o_ref):
    for b in range(o_ref.shape[0]):
        for p in range(SB_HEADS // 2):
            sl = slice(p * LANES, (p + 1) * LANES)
            o = jnp.concatenate([acc_ref[b, 2 * p, :LANES // 2, :], acc_ref[b, 2 * p + 1, LANES // 2:, :]], axis=0).T
            o_ref[b, :, sl] = (o * _silu(g_ref[b, :, sl].astype(F32))).astype(BF16)


def _sb_kernel(q_ref, k_ref, v_ref, km_ref, vm_ref, g_ref, u_ref, um_ref, o_ref, qh_ref, acc_ref, c_ref):
    i = pl.program_id(1)
    T = ATT_TILE
    nb = q_ref.shape[0]
    _sb_prologue(q_ref, qh_ref)

    def real_chunk(j, mask=None, first=False):
        start = pl.multiple_of(j * T, T)
        kv = [(k_ref[b, pl.ds(start, T), :], v_ref[b, pl.ds(start, T), :]) for b in range(nb)]
        return kv, u_ref[...], mask, first

    diagonal = lambda: real_chunk(i, _iota((T, T), 0) < _iota((T, T), 1), first=True)

    @pl.when(i == 0)
    def _():
        _sb_chunks(qh_ref, [diagonal()], acc_ref, c_ref)

    @pl.when(i > 0)
    def _():
        _sb_chunks(qh_ref, [diagonal(), real_chunk(i - 1)], acc_ref, c_ref)

    def alive():
        return jnp.max(c_ref[...]) > DEAD_CARRY

    def earlier(state):
        idx, _ = state
        _sb_chunks(qh_ref, [real_chunk(i - 1 - idx)], acc_ref, c_ref)
        return idx + 1, alive()

    _, still_alive = lax.while_loop(lambda state: (state[0] < i) & state[1], earlier, (jnp.int32(1), alive()))

    @pl.when(still_alive)
    def _():
        _sb_chunks(qh_ref, [([(km_ref[0, N_PAD:, :], vm_ref[0, N_PAD:, :])] * nb, um_ref[...], None, False)],
                   acc_ref, c_ref)

    _sb_epilogue(acc_ref, g_ref, o_ref)


def _sb_attn(q, k, v, k_meta, v_meta, gate, u, u_meta):
    B, L, _ = q.shape
    T, nb = ATT_TILE, ATT_BATCH
    assert B % nb == 0 and L % T == 0, (B, L)
    tile = pl.BlockSpec((nb, T, SB_WIDTH), lambda b, i: (b, i, 0))
    seq = pl.BlockSpec((nb, L, SB_WIDTH), lambda b, i: (b, 0, 0))
    meta = pl.BlockSpec((1, BLOCK, SB_WIDTH), lambda b, i: (0, 0, 0))
    const = lambda a: pl.BlockSpec(a.shape, lambda b, i: (0, 0))
    return pl.pallas_call(
        _sb_kernel,
        grid=(B // nb, L // T),
        in_specs=[tile, seq, seq, meta, meta, tile, const(u), const(u_meta)],
        out_specs=tile,
        out_shape=jax.ShapeDtypeStruct((B, L, SB_WIDTH), BF16),
        scratch_shapes=[pltpu.VMEM((nb, SB_HEADS, T, LANES), BF16), pltpu.VMEM((nb, SB_HEADS, LANES, T), F32),
                        pltpu.VMEM((nb, SB_HEADS, 8, T), F32)],
        compiler_params=_params(("parallel", "arbitrary")),
        name="sb_attn",
    )(q, k, v, k_meta, v_meta, gate, u, u_meta)


def _sb_meta_kernel(q_ref, k_ref, v_ref, g_ref, u_ref, o_ref, qh_ref, acc_ref, c_ref):
    _sb_prologue(q_ref, qh_ref)
    key, query = _iota((BLOCK, BLOCK), 0), _iota((BLOCK, BLOCK), 1)
    _sb_chunks(qh_ref, [([(k_ref[0], v_ref[0])], u_ref[...], (key < query) & (key >= N_PAD), True)], acc_ref, c_ref)
    _sb_epilogue(acc_ref, g_ref, o_ref)


def _sb_meta(q, k, v, gate, u_meta):
    blk = pl.BlockSpec((1, BLOCK, SB_WIDTH), lambda i: (0, 0, 0))
    return pl.pallas_call(
        _sb_meta_kernel,
        grid=(1,),
        in_specs=[blk, blk, blk, blk, pl.BlockSpec(u_meta.shape, lambda i: (0, 0))],
        out_specs=blk,
        out_shape=jax.ShapeDtypeStruct((1, BLOCK, SB_WIDTH), BF16),
        scratch_shapes=[pltpu.VMEM((1, SB_HEADS, BLOCK, LANES), BF16), pltpu.VMEM((1, SB_HEADS, LANES, BLOCK), F32),
                        pltpu.VMEM((1, SB_HEADS, 8, BLOCK), F32)],
        compiler_params=_params(("arbitrary",)),
        name="sb_meta",
    )(q, k, v, gate, u_meta)


def _mla_chunk(q_ref, kv, mask, acc_ref, m_ref, l_ref, first=False):
    items = [(b, h) for b in range(len(kv)) for h in range(MLA_HEADS)]
    rep = lambda row: jnp.broadcast_to(row, m_ref.shape[2:])

    def scores(b, h):
        hs = slice(h * LANES, (h + 1) * LANES)
        s = _dot_nt(kv[b][0][:, hs], q_ref[b, :, hs])
        return s if mask is None else jnp.where(mask, s, NEG)

    def probs(b, h, s):
        m_new = jnp.max(s, axis=0, keepdims=True)
        if first:
            alpha = None
        else:
            m_old = m_ref[b, h, 0:1, :]
            m_new = jnp.maximum(m_old, m_new)
            alpha = jnp.exp2(m_old - m_new)
        m_ref[b, h] = rep(m_new)
        p = jnp.exp2(s - m_new)
        return p.astype(BF16), jnp.sum(p, axis=0, keepdims=True), alpha

    def update(b, h, p, l_new, alpha):
        vs = slice((h // 2) * LANES, (h // 2 + 1) * LANES)
        pv = lax.dot_general(kv[b][1][:, vs], p, (((0,), (0,)), ((), ())), preferred_element_type=F32)
        if first:
            acc_ref[b, h], l_ref[b, h] = pv, rep(l_new)
        else:
            acc_ref[b, h] = alpha * acc_ref[b, h] + pv
            l_ref[b, h] = rep(alpha * l_ref[b, h, 0:1, :] + l_new)

    ss, ps = {}, {}
    for step in range(len(items) + 2 * MLA_SKEW):
        if step < len(items):
            ss[step] = scores(*items[step])
        if 0 <= step - MLA_SKEW < len(items):
            ps[step - MLA_SKEW] = probs(*items[step - MLA_SKEW], ss.pop(step - MLA_SKEW))
        if 0 <= step - 2 * MLA_SKEW < len(items):
            update(*items[step - 2 * MLA_SKEW], *ps.pop(step - 2 * MLA_SKEW))


def _mla_finish(acc_ref, l_ref, g_ref, o_ref):
    half = LANES // 2
    for b in range(o_ref.shape[0]):
        for p in range(MLA_HEADS // 2):
            sl = slice(p * LANES, (p + 1) * LANES)
            lo = acc_ref[b, 2 * p, :half, :] * (1.0 / l_ref[b, 2 * p, 0:1, :])
            hi = acc_ref[b, 2 * p + 1, half:, :] * (1.0 / l_ref[b, 2 * p + 1, 0:1, :])
            o = jnp.concatenate([lo, hi], axis=0).T
            gate = g_ref[b, :, MLA_WIDTH + p * LANES:MLA_WIDTH + (p + 1) * LANES].astype(F32)
            o_ref[b, :, sl] = (o * _silu(gate)).astype(BF16)


def _mla_kernel(q_ref, k_ref, v_ref, km_ref, vm_ref, g_ref, o_ref, acc_ref, m_ref, l_ref):
    i = pl.program_id(1)
    T = ATT_TILE
    nb = q_ref.shape[0]

    def real_kv(j):
        start = pl.multiple_of(j * T, T)
        return [(k_ref[b, pl.ds(start, T), :], v_ref[b, pl.ds(start, T), :]) for b in range(nb)]

    def first_chunk(extra):
        parts = [real_kv(i)] + ([real_kv(0)] if extra else []) + [[(km_ref[0, N_PAD:, :], vm_ref[0, N_PAD:, :])] * nb]
        kv = [tuple(jnp.concatenate([part[b][j] for part in parts], axis=0) for j in range(2)) for b in range(nb)]
        masks = ([_iota((T, T), 0) <= _iota((T, T), 1)] + ([_iota((T, T), 0) >= 0] if extra else [])
                 + [_iota((N_META, T), 0) >= 0])
        _mla_chunk(q_ref, kv, jnp.concatenate(masks, axis=0), acc_ref, m_ref, l_ref, first=True)

    odd = i % 2

    @pl.when(odd == 1)
    def _():
        first_chunk(True)

    @pl.when(odd == 0)
    def _():
        first_chunk(False)

    def earlier(idx, carry):
        start = pl.multiple_of((odd + 2 * idx) * T, T)
        kv = [(k_ref[b, pl.ds(start, 2 * T), :], v_ref[b, pl.ds(start, 2 * T), :]) for b in range(nb)]
        _mla_chunk(q_ref, kv, None, acc_ref, m_ref, l_ref)
        return carry

    lax.fori_loop(0, i // 2, earlier, 0)
    _mla_finish(acc_ref, l_ref, g_ref, o_ref)


def _mla_attn(q, k, v, k_meta, v_meta, gate):
    B, L, _ = q.shape
    T, nb = ATT_TILE, ATT_BATCH
    assert B % nb == 0 and L % (2 * T) == 0, (B, L)
    qk_w = MLA_HEADS * LANES
    return pl.pallas_call(
        _mla_kernel,
        grid=(B // nb, L // T),
        in_specs=[pl.BlockSpec((nb, T, qk_w), lambda b, i: (b, i, 0)),
                  pl.BlockSpec((nb, L, qk_w), lambda b, i: (b, 0, 0)),
                  pl.BlockSpec((nb, L, MLA_WIDTH), lambda b, i: (b, 0, 0)),
                  pl.BlockSpec((1, BLOCK, qk_w), lambda b, i: (0, 0, 0)),
                  pl.BlockSpec((1, BLOCK, MLA_WIDTH), lambda b, i: (0, 0, 0)),
                  pl.BlockSpec((nb, T, 2 * MLA_WIDTH), lambda b, i: (b, i, 0))],
        out_specs=pl.BlockSpec((nb, T, MLA_WIDTH), lambda b, i: (b, i, 0)),
        out_shape=jax.ShapeDtypeStruct((B, L, MLA_WIDTH), BF16),
        scratch_shapes=[pltpu.VMEM((nb, MLA_HEADS, LANES, T), F32), pltpu.VMEM((nb, MLA_HEADS, 8, T), F32),
                        pltpu.VMEM((nb, MLA_HEADS, 8, T), F32)],
        compiler_params=_params(("parallel", "arbitrary")),
        name="mla_attn",
    )(q, k, v, k_meta, v_meta, gate)


def _mla_meta_kernel(q_ref, k_ref, v_ref, g_ref, o_ref, acc_ref, m_ref, l_ref):
    key, query = _iota((BLOCK, BLOCK), 0), _iota((BLOCK, BLOCK), 1)
    _mla_chunk(q_ref, [(k_ref[0], v_ref[0])], (key <= query) & (key >= N_PAD), acc_ref, m_ref, l_ref, first=True)
    _mla_finish(acc_ref, l_ref, g_ref, o_ref)


def _mla_meta(q, k, v, gate):
    qk_w = MLA_HEADS * LANES
    blk = lambda w: pl.BlockSpec((1, BLOCK, w), lambda i: (0, 0, 0))
    return pl.pallas_call(
        _mla_meta_kernel,
        grid=(1,),
        in_specs=[blk(qk_w), blk(qk_w), blk(MLA_WIDTH), blk(2 * MLA_WIDTH)],
        out_specs=blk(MLA_WIDTH),
        out_shape=jax.ShapeDtypeStruct((1, BLOCK, MLA_WIDTH), BF16),
        scratch_shapes=[pltpu.VMEM((1, MLA_HEADS, LANES, BLOCK), F32), pltpu.VMEM((1, MLA_HEADS, 8, BLOCK), F32),
                        pltpu.VMEM((1, MLA_HEADS, 8, BLOCK), F32)],
        compiler_params=_params(("arbitrary",)),
        name="mla_meta",
    )(q, k, v, gate)


def _mid_kernel(osb_ref, omla_ref, h_ref, wo_ref, g_ref, w_ref, h1_ref, q_ref, kv_ref, gate_ref):
    rows = h_ref.shape[1]
    halves = [slice(0, rows // 2), slice(rows // 2, rows)] if rows >= 2 * BLOCK else [slice(0, rows)]
    ys = [_dot(osb_ref[0, r, :], wo_ref[0:512, :]) + _dot(omla_ref[0, r, :], wo_ref[512:1024, :]) for r in halves]
    for r, y in zip(halves, ys):
        h1 = h_ref[0, r, :] + y
        h1_ref[0, r, :] = h1
        xn = _rms(h1, g_ref[...]).astype(BF16)
        q_ref[0, r, :] = (_dot(xn, w_ref[:, 0:1024]) * LOG2E).astype(BF16)
        kv_ref[0, r, :] = _dot(xn, w_ref[:, 1024:1280]).astype(BF16)
        gate_ref[0, r, :] = _dot(xn, w_ref[:, 1280:2304]).astype(BF16)


def _mid(osb, omla, h, tile, wo, g, w):
    B, L, _ = h.shape
    assert L % tile == 0, (L, tile)
    row = lambda width: pl.BlockSpec((1, tile, width), lambda b, t: (b, t, 0))
    full = lambda a: pl.BlockSpec(a.shape, lambda b, t: (0,) * a.ndim)
    return pl.pallas_call(
        _mid_kernel,
        grid=(B, L // tile),
        in_specs=[row(512), row(512), row(D_MODEL), full(wo), full(g), full(w)],
        out_specs=[row(D_MODEL), row(1024), row(256), row(1024)],
        out_shape=[jax.ShapeDtypeStruct((B, L, D_MODEL), F32),
                   jax.ShapeDtypeStruct((B, L, 1024), BF16),
                   jax.ShapeDtypeStruct((B, L, 256), BF16),
                   jax.ShapeDtypeStruct((B, L, 1024), BF16)],
        compiler_params=_params(("parallel", "parallel")),
        name="mid",
    )(osb, omla, h, wo, g, w)


def _swa_kernel(sink_ref, q_ref, kvp_ref, kvc_ref, kvm_ref, g_ref, bb_ref, mb_ref, h_ref, wo_ref, fg_ref,
                out_ref, o_scr):
    n = pl.program_id(1)
    first = jnp.where(n > 0, 0, 1)
    in_cur = _iota((BLOCK, BLOCK), 0) <= _iota((BLOCK, BLOCK), 1)
    blocks_before = (n + 1).astype(F32) * float(BLOCK)
    pairs = SWA_HEADS // 2
    half = LANES // 2
    items = [(b, p) for b in range(q_ref.shape[0]) for p in range(pairs)]
    k_sl, v_sl = slice(0, LANES), slice(LANES, 2 * LANES)
    keys_of = lambda b, sl: jnp.concatenate([kvp_ref[b, :, sl], kvc_ref[b, :, sl], kvm_ref[0, N_PAD:, sl]], axis=0)

    def scores(b, p):
        q_stack = jnp.concatenate(_split_pair(q_ref[b, :, p * LANES:(p + 1) * LANES]), axis=0)
        return _dot_nt(keys_of(b, k_sl), q_stack)

    def probs(p, z_pair):
        p_cols, invs = [], []
        for hh in range(2):
            h = p + pairs * hh
            z = z_pair[:, hh * BLOCK:(hh + 1) * BLOCK]
            slope = 2.0 ** (-8.0 * (h + 1.0) / SWA_HEADS) * LOG2E
            s_band = jnp.where(in_cur, z[BLOCK:2 * BLOCK], z[:BLOCK]) - bb_ref[first, h]
            s_meta = z[2 * BLOCK:] - mb_ref[h] - slope * blocks_before
            sink = sink_ref[h] * LOG2E
            m = jnp.maximum(jnp.maximum(jnp.max(s_band, axis=0, keepdims=True),
                                        jnp.max(s_meta, axis=0, keepdims=True)), sink)
            p_band, p_meta = jnp.exp2(s_band - m), jnp.exp2(s_meta - m)
            denom = (jnp.sum(p_band, axis=0, keepdims=True) + jnp.sum(p_meta, axis=0, keepdims=True)
                     + jnp.exp2(sink - m))
            p_cols.append(jnp.concatenate(
                [jnp.where(in_cur, 0.0, p_band), jnp.where(in_cur, p_band, 0.0), p_meta], axis=0).astype(BF16))
            invs.append(1.0 / denom)
        return jnp.concatenate(p_cols, axis=1), jnp.concatenate(invs, axis=1)

    def values(b, p, p_pair, inv):
        o_t = lax.dot_general(keys_of(b, v_sl), p_pair, (((0,), (0,)), ((), ())), preferred_element_type=F32) * inv
        o = jnp.concatenate([o_t[:half, :BLOCK], o_t[half:, BLOCK:]], axis=0).T
        sl = slice(p * LANES, (p + 1) * LANES)
        o_scr[b, :, sl] = (o * _silu(g_ref[b, :, sl].astype(F32))).astype(BF16)

    zs, ps = {}, {}
    for step in range(len(items) + 2 * SWA_SKEW):
        if step < len(items):
            zs[step] = scores(*items[step])
        if 0 <= step - SWA_SKEW < len(items):
            ps[step - SWA_SKEW] = probs(items[step - SWA_SKEW][1], zs.pop(step - SWA_SKEW))
        t = step - 2 * SWA_SKEW
        if 0 <= t < len(items):
            b, p = items[t]
            values(b, p, *ps.pop(t))
            if p == pairs - 1:
                out_ref[b] = _rms(h_ref[b] + _dot(o_scr[b], wo_ref[...]), fg_ref[...])


def _swa_bias_tables():
    col, row = np.arange(BLOCK)[:, None], np.arange(BLOCK)[None, :]
    slopes = (2.0 ** (-8.0 * (np.arange(SWA_HEADS) + 1.0) / SWA_HEADS) * LOG2E)[:, None, None]
    band = slopes * np.where(col <= row, row - col, row - col + BLOCK)
    band_first = band + np.where(col <= row, 0.0, -NEG)
    meta = (slopes * (row - col))[:, N_PAD:, :]
    return jnp.asarray(np.stack([band, band_first]), F32), jnp.asarray(meta, F32)


def _swa_attn(sinks, q, kv, kv_meta, gate, h1, wo, final_g):
    B, L, _ = q.shape
    kvw = kv.shape[-1]
    nb = SWA_BATCH
    assert B % nb == 0 and L % BLOCK == 0, (B, L)
    band_bias, meta_bias = _swa_bias_tables()
    row = lambda w: pl.BlockSpec((nb, BLOCK, w), lambda b, n, s: (b, n, 0))
    const = lambda a: pl.BlockSpec(a.shape, lambda b, n, s: (0,) * a.ndim)
    grid_spec = pltpu.PrefetchScalarGridSpec(
        num_scalar_prefetch=1,
        grid=(B // nb, L // BLOCK),
        in_specs=[row(SWA_WIDTH),
                  pl.BlockSpec((nb, BLOCK, kvw), lambda b, n, s: (b, jnp.maximum(n - 1, 0), 0)),
                  row(kvw), const(kv_meta), row(SWA_WIDTH), const(band_bias), const(meta_bias),
                  row(D_MODEL), const(wo), const(final_g)],
        out_specs=row(D_MODEL),
        scratch_shapes=[pltpu.VMEM((nb, BLOCK, SWA_WIDTH), BF16)],
    )
    return pl.pallas_call(
        _swa_kernel,
        grid_spec=grid_spec,
        out_shape=jax.ShapeDtypeStruct((B, L, D_MODEL), F32),
        compiler_params=_params(("parallel", "parallel")),
        name="swa_attn",
    )(sinks, q, kv, kv, kv_meta, gate, band_bias, meta_bias, h1, wo, final_g)


def _layer0_weights(w_in, w_uq, w_ukv):
    w_in, w_uq, w_ukv = w_in.astype(BF16), w_uq.astype(BF16), w_ukv.astype(BF16)
    q, k, v, g_sb = (w_in[:, i * 512:(i + 1) * 512] for i in range(4))
    c_q, c_kv = w_in[:, 2048:2304], w_in[:, 2304:2432]
    k_r, g_mla = w_in[:, 2432:2464], w_in[:, 2464:2976]
    half = MLA_ROPE // 2
    r1, r2 = k_r[:, :half], k_r[:, half:]
    z = lambda n: jnp.zeros((D_MODEL, n), w_in.dtype)
    kr_blk = jnp.concatenate([z(MLA_NOPE), r1, r2, z(LANES - MLA_NOPE - MLA_ROPE)], axis=1)
    w0 = jnp.concatenate([q * (SB_DIM ** -0.5), k, v, g_sb, g_mla, c_q, c_kv, kr_blk], axis=1)

    uq = w_uq.reshape(MLA_Q_LORA, MLA_HEADS, MLA_NOPE + MLA_ROPE)
    nope, u1, u2 = uq[..., :MLA_NOPE], uq[..., MLA_NOPE:MLA_NOPE + half], uq[..., MLA_NOPE + half:]
    zq = lambda n: jnp.zeros((MLA_Q_LORA, MLA_HEADS, n), w_uq.dtype)
    uq_main = jnp.concatenate([nope, u1, u2, zq(LANES - MLA_NOPE - MLA_ROPE)], axis=-1)
    uq_rot = jnp.concatenate([zq(MLA_NOPE), -u2, u1, zq(LANES - MLA_NOPE - MLA_ROPE)], axis=-1)
    wuq = jnp.concatenate([uq_main.reshape(MLA_Q_LORA, -1), uq_rot.reshape(MLA_Q_LORA, -1)], axis=1)

    ukv = w_ukv.reshape(MLA_KV_LORA, MLA_HEADS, MLA_NOPE + MLA_V)
    k_nope = jnp.concatenate([ukv[..., :MLA_NOPE],
                              jnp.zeros((MLA_KV_LORA, MLA_HEADS, LANES - MLA_NOPE), w_ukv.dtype)], axis=-1)
    wukv = jnp.concatenate([k_nope.reshape(MLA_KV_LORA, -1),
                            ukv[..., MLA_NOPE:].reshape(MLA_KV_LORA, -1)], axis=1)
    return w0, wuq, wukv


def _pair_heads(w, axis):
    shape = w.shape
    w = w.reshape(shape[:axis] + (SWA_KV_HEADS, SWA_HEADS // SWA_KV_HEADS, SWA_DIM) + shape[axis + 1:])
    return jnp.swapaxes(w, axis, axis + 1).reshape(shape)


def _layer1_weights(w_in, w_out):
    w_in, w_out = w_in.astype(BF16), w_out.astype(BF16)
    q, kv, g = w_in[:, :1024], w_in[:, 1024:1280], w_in[:, 1280:2304]
    w1 = jnp.concatenate([_pair_heads(q * (SWA_DIM ** -0.5), 1), kv, _pair_heads(g, 1)], axis=1)
    return w1, _pair_heads(w_out, 0)


def _rope_tables():
    half = MLA_ROPE // 2
    pos = np.arange(N_META + SEQ, dtype=np.float64)
    inv = ROPE_BASE ** (-np.arange(half, dtype=np.float64) / half)
    ang = pos[:, None] * inv[None, :]
    cos, sin = np.cos(ang), np.sin(ang)
    n = pos.shape[0]
    z = lambda w: np.zeros((n, w))
    tail = LANES - MLA_NOPE - MLA_ROPE
    c = np.concatenate([np.ones((n, MLA_NOPE)), cos, cos, z(tail)], axis=1)
    s = np.concatenate([z(MLA_NOPE), sin, sin, z(tail)], axis=1)
    s1 = np.concatenate([z(MLA_NOPE), -sin, z(half), z(tail)], axis=1)
    s2 = np.concatenate([z(MLA_NOPE), z(half), sin, z(tail)], axis=1)
    scale = (MLA_NOPE + MLA_ROPE) ** -0.5 * LOG2E
    tabs = (c * scale, s * scale, c, s1, s2)
    pad = lambda t: np.concatenate([np.zeros((N_PAD, LANES)), t[:N_META]], axis=0)
    return (tuple(jnp.asarray(pad(t), F32) for t in tabs), tuple(jnp.asarray(t[N_META:], F32) for t in tabs))


def _suffix_matrix(n):
    return jnp.asarray(np.where(np.arange(n)[:, None] <= np.arange(n)[None, :], -1.0, 0.0), BF16)


def kernel(x, meta, norm_g, final_g, ev_w_in, ev_q_norm_g, ev_kv_norm_g, ev_w_uq, ev_w_ukv,
           ev_w_out, od_w_in, od_sinks, od_w_out):
    w0, wuq, wukv = _layer0_weights(ev_w_in[0], ev_w_uq[0], ev_w_ukv[0])
    w1, wo1 = _layer1_weights(od_w_in[0], od_w_out[0])
    wo0 = ev_w_out[0].astype(BF16)
    tabs_meta, tabs_real = _rope_tables()
    u, u_block, u_meta = _suffix_matrix(ATT_TILE), _suffix_matrix(BLOCK), _suffix_matrix(N_META)
    l0 = (norm_g[0:1], w0, ev_q_norm_g[0:1], ev_kv_norm_g[0:1], wuq, wukv)

    hm = jnp.concatenate([jnp.zeros((N_PAD, D_MODEL), x.dtype), meta.astype(x.dtype)], axis=0)[None]
    qsb_m, ksb_m, vsb_m, gate_m, qm_m, km_m, vm_m = _proj0(hm, BLOCK, *l0, tabs_meta)
    osb_m = _sb_meta(qsb_m, ksb_m, vsb_m, gate_m, u_block)
    omla_m = _mla_meta(qm_m, km_m, vm_m, gate_m)
    _, _, kv1_m, _ = _mid(osb_m, omla_m, hm, BLOCK, wo0, norm_g[1:2], w1)

    q_sb, k_sb, v_sb, gate0, q_mla, k_mla, v_mla = _proj0(x, ROW_TILE, *l0, tabs_real)
    o_sb = _sb_attn(q_sb, k_sb, v_sb, ksb_m, vsb_m, gate0, u, u_meta)
    o_mla = _mla_attn(q_mla, k_mla, v_mla, km_m, vm_m, gate0)
    h1, q1, kv1, gate1 = _mid(o_sb, o_mla, x, ROW_TILE, wo0, norm_g[1:2], w1)
    return _swa_attn(od_sinks[0], q1, kv1, kv1_m, gate1, h1, wo1, final_g[None, :])
```

```python
import math

import numpy as np
import jax
import jax.numpy as jnp
from jax import lax
from jax.experimental import pallas as pl
from jax.experimental.pallas import tpu as pltpu

D_MODEL = 1024
SEQ = 2048
N_META = 16
BLOCK = 128
N_PAD = BLOCK - N_META
NORM_EPS = 1e-6
NEG = -1e30

SB_HEADS = 8
SB_DIM = 64
SB_WIDTH = SB_HEADS * SB_DIM
MLA_HEADS = 8
MLA_Q_LORA = 256
MLA_KV_LORA = 128
MLA_NOPE = 64
MLA_ROPE = 32
MLA_V = 64
MLA_WIDTH = MLA_HEADS * MLA_V
ROPE_BASE = 10000.0
SWA_HEADS = 16
SWA_KV_HEADS = 2
SWA_DIM = 64
SWA_WIDTH = SWA_HEADS * SWA_DIM

LANES = 128
SUBLANES = 8
ROW_TILE = 1024
ATT_TILE = 256
VMEM_LIMIT = 48 * 1024 * 1024
LOG2E = math.log2(math.e)
SB_BATCH, MLA_BATCH, SWA_BATCH = 4, 2, 8
SB_SKEW = MLA_SKEW = SWA_SKEW = 2
DEAD_CARRY = -256.0
SOFTPLUS_CLAMP = 64.0

BF16 = jnp.bfloat16
F32 = jnp.float32


def _dot(a, b):
    return jnp.dot(a, b, preferred_element_type=F32)


def _dot_nt(a, b):
    return lax.dot_general(a, b, (((1,), (1,)), ((), ())), preferred_element_type=F32)


def _rms(x, g):
    ms = jnp.mean(x * x, axis=-1, keepdims=True)
    return x * lax.rsqrt(ms + NORM_EPS) * g


def _silu(g):
    return g * (1.0 / (1.0 + jnp.exp(-g)))


def _params(sem):
    return pltpu.CompilerParams(dimension_semantics=sem, vmem_limit_bytes=VMEM_LIMIT)


def _iota(shape, dim):
    return lax.broadcasted_iota(jnp.int32, shape, dim)


def _split_pair(x):
    lane = _iota(x.shape, 1)
    zero = jnp.zeros_like(x)
    return jnp.where(lane < LANES // 2, x, zero), jnp.where(lane >= LANES // 2, x, zero)


def _proj0_kernel(x_ref, g_ref, w_ref, qg_ref, kvg_ref, wuq_ref, wukv_ref,
                  cq_ref, sq_ref, ck_ref, sk1_ref, sk2_ref,
                  qsb_ref, ksb_ref, vsb_ref, gate_ref, qm_ref, km_ref, vm_ref):
    rows = x_ref.shape[1]
    halves = [slice(0, rows // 2), slice(rows // 2, rows)] if rows >= 2 * BLOCK else [slice(0, rows)]
    qk_w = MLA_HEADS * LANES
    for r in halves:
        xn = _rms(x_ref[0, r, :], g_ref[...]).astype(BF16)
        qsb_ref[0, r, :] = (_dot(xn, w_ref[:, 0:512]) * LOG2E).astype(BF16)
        ksb_ref[0, r, :] = _dot(xn, w_ref[:, 512:1024]).astype(BF16)
        vsb_ref[0, r, :] = _dot(xn, w_ref[:, 1024:1536]).astype(BF16)
        gate_ref[0, r, :] = _dot(xn, w_ref[:, 1536:2560]).astype(BF16)
        lat = _dot(xn, w_ref[:, 2560:3072])
        cqn = _rms(lat[:, 0:256], qg_ref[...]).astype(BF16)
        ckvn = _rms(lat[:, 256:384], kvg_ref[...]).astype(BF16)
        kr = lat[:, 384:512]
        k_rope = (kr * ck_ref[r, :] + pltpu.roll(kr, LANES - MLA_ROPE // 2, 1) * sk1_ref[r, :]
                  + pltpu.roll(kr, MLA_ROPE // 2, 1) * sk2_ref[r, :])
        q_all = _dot(cqn, wuq_ref[...])
        kv_all = _dot(ckvn, wukv_ref[...])
        cq = cq_ref[r, :]
        sq = sq_ref[r, :]
        for h in range(MLA_HEADS):
            lo, hi = h * LANES, (h + 1) * LANES
            qm_ref[0, r, lo:hi] = (q_all[:, lo:hi] * cq + q_all[:, qk_w + lo:qk_w + hi] * sq).astype(BF16)
            km_ref[0, r, lo:hi] = (kv_all[:, lo:hi] + k_rope).astype(BF16)
        vm_ref[0, r, :] = kv_all[:, qk_w:].astype(BF16)


def _proj0(h, tile, g, w, qg, kvg, wuq, wukv, tables):
    B, L, _ = h.shape
    assert L % tile == 0, (L, tile)
    row = lambda width: pl.BlockSpec((1, tile, width), lambda b, t: (b, t, 0))
    full = lambda a: pl.BlockSpec(a.shape, lambda b, t: (0,) * a.ndim)
    tab = pl.BlockSpec((tile, LANES), lambda b, t: (t, 0))
    out = lambda width: jax.ShapeDtypeStruct((B, L, width), BF16)
    return pl.pallas_call(
        _proj0_kernel,
        grid=(B, L // tile),
        in_specs=[row(D_MODEL), full(g), full(w), full(qg), full(kvg), full(wuq), full(wukv),
                  tab, tab, tab, tab, tab],
        out_specs=[row(512), row(512), row(512), row(1024), row(1024), row(1024), row(512)],
        out_shape=[out(512), out(512), out(512), out(1024), out(1024), out(1024), out(512)],
        compiler_params=_params(("parallel", "parallel")),
        name="proj0",
    )(h, g, w, qg, kvg, wuq, wukv, *tables)


def _sb_chunks(qh_ref, chunks, acc_ref, c_ref):
    nb = len(chunks[0][0])
    items = [(ci, b, h) for ci in range(len(chunks)) for b in range(nb) for h in range(SB_HEADS)]
    sl = lambda h: slice((h // 2) * LANES, (h // 2 + 1) * LANES)
    rep = lambda row: jnp.broadcast_to(row, c_ref.shape[2:])

    def scores(ci, b, h):
        return _dot_nt(chunks[ci][0][b][0][:, sl(h)], qh_ref[b, h])

    def suffix(ci, z):
        _, u, mask, _ = chunks[ci]
        sp = jnp.maximum(jnp.log2(1.0 + jnp.exp2(jnp.minimum(z, SOFTPLUS_CLAMP))), z)
        if mask is not None:
            sp = jnp.where(mask, sp, 0.0)
        return _dot(u, sp.astype(BF16))

    def weights(ci, b, h, z, sfx):
        kv, _, mask, first = chunks[ci]
        total = sfx[0:1, :]
        if first:
            a = jnp.exp2(z + sfx)
        else:
            c = c_ref[b, h, 0:1, :]
            a = jnp.exp2(z + sfx + c)
        if mask is not None:
            a = jnp.where(mask, a, 0.0)
        av = lax.dot_general(kv[b][1][:, sl(h)], a.astype(BF16), (((0,), (0,)), ((), ())),
                             preferred_element_type=F32)
        if first:
            acc_ref[b, h], c_ref[b, h] = av, rep(total)
        else:
            acc_ref[b, h] += av
            c_ref[b, h] = rep(c + total)

    zs, sfxs = {}, {}
    for step in range(len(items) + 2 * SB_SKEW):
        if step < len(items):
            zs[step] = scores(*items[step])
        if 0 <= step - SB_SKEW < len(items):
            sfxs[step - SB_SKEW] = suffix(items[step - SB_SKEW][0], zs[step - SB_SKEW])
        if 0 <= step - 2 * SB_SKEW < len(items):
            weights(*items[step - 2 * SB_SKEW], zs.pop(step - 2 * SB_SKEW), sfxs.pop(step - 2 * SB_SKEW))


def _sb_prologue(q_ref, qh_ref):
    for b in range(q_ref.shape[0]):
        for p in range(SB_HEADS // 2):
            qh_ref[b, 2 * p], qh_ref[b, 2 * p + 1] = _split_pair(q_ref[b, :, p * LANES:(p + 1) * LANES])


def _sb_epilogue(acc_ref, g_ref, o_ref):
    for b in range(o_ref.shape[0]):
        for p in range(SB_HEADS // 2):
            sl = slice(p * LANES, (p + 1) * LANES)
            o = jnp.concatenate([acc_ref[b, 2 * p, :LANES // 2, :], acc_ref[b, 2 * p + 1, LANES // 2:, :]], axis=0).T
            o_ref[b, :, sl] = (o * _silu(g_ref[b, :, sl].astype(F32))).astype(BF16)


def _sb_kernel(q_ref, k_ref, v_ref, km_ref, vm_ref, g_ref, u_ref, um_ref, o_ref, qh_ref, acc_ref, c_ref):
    i = pl.program_id(1)
    T = ATT_TILE
    nb = q_ref.shape[0]
    _sb_prologue(q_ref, qh_ref)

    def real_chunk(j, mask=None, first=False):
        start = pl.multiple_of(j * T, T)
        kv = [(k_ref[b, pl.ds(start, T), :], v_ref[b, pl.ds(start, T), :]) for b in range(nb)]
        return kv, u_ref[...], mask, first

    diagonal = lambda: real_chunk(i, _iota((T, T), 0) < _iota((T, T), 1), first=True)

    @pl.when(i == 0)
    def _():
        _sb_chunks(qh_ref, [diagonal()], acc_ref, c_ref)

    @pl.when(i > 0)
    def _():
        _sb_chunks(qh_ref, [diagonal(), real_chunk(i - 1)], acc_ref, c_ref)

    def alive():
        return jnp.max(c_ref[...]) > DEAD_CARRY

    def earlier(state):
        idx, _ = state
        _sb_chunks(qh_ref, [real_chunk(i - 1 - idx)], acc_ref, c_ref)
        return idx + 1, alive()

    _, still_alive = lax.while_loop(lambda state: (state[0] < i) & state[1], earlier, (jnp.int32(1), alive()))

    @pl.when(still_alive)
    def _():
        _sb_chunks(qh_ref, [([(km_ref[0, N_PAD:, :], vm_ref[0, N_PAD:, :])] * nb, um_ref[...], None, False)],
                   acc_ref, c_ref)

    _sb_epilogue(acc_ref, g_ref, o_ref)


def _sb_attn(q, k, v, k_meta, v_meta, gate, u, u_meta):
    B, L, _ = q.shape
    T, nb = ATT_TILE, SB_BATCH
    assert B % nb == 0 and L % T == 0, (B, L)
    tile = pl.BlockSpec((nb, T, SB_WIDTH), lambda b, i: (b, i, 0))
    seq = pl.BlockSpec((nb, L, SB_WIDTH), lambda b, i: (b, 0, 0))
    meta = pl.BlockSpec((1, BLOCK, SB_WIDTH), lambda b, i: (0, 0, 0))
    const = lambda a: pl.BlockSpec(a.shape, lambda b, i: (0, 0))
    return pl.pallas_call(
        _sb_kernel,
        grid=(B // nb, L // T),
        in_specs=[tile, seq, seq, meta, meta, tile, const(u), const(u_meta)],
        out_specs=tile,
        out_shape=jax.ShapeDtypeStruct((B, L, SB_WIDTH), BF16),
        scratch_shapes=[pltpu.VMEM((nb, SB_HEADS, T, LANES), BF16), pltpu.VMEM((nb, SB_HEADS, LANES, T), F32),
                        pltpu.VMEM((nb, SB_HEADS, SUBLANES, T), F32)],
        compiler_params=_params(("parallel", "arbitrary")),
        name="sb_attn",
    )(q, k, v, k_meta, v_meta, gate, u, u_meta)


def _sb_meta_kernel(q_ref, k_ref, v_ref, g_ref, u_ref, o_ref, qh_ref, acc_ref, c_ref):
    _sb_prologue(q_ref, qh_ref)
    key, query = _iota((BLOCK, BLOCK), 0), _iota((BLOCK, BLOCK), 1)
    _sb_chunks(qh_ref, [([(k_ref[0], v_ref[0])], u_ref[...], (key < query) & (key >= N_PAD), True)], acc_ref, c_ref)
    _sb_epilogue(acc_ref, g_ref, o_ref)


def _sb_meta(q, k, v, gate, u_meta):
    blk = pl.BlockSpec((1, BLOCK, SB_WIDTH), lambda i: (0, 0, 0))
    return pl.pallas_call(
        _sb_meta_kernel,
        grid=(1,),
        in_specs=[blk, blk, blk, blk, pl.BlockSpec(u_meta.shape, lambda i: (0, 0))],
        out_specs=blk,
        out_shape=jax.ShapeDtypeStruct((1, BLOCK, SB_WIDTH), BF16),
        scratch_shapes=[pltpu.VMEM((1, SB_HEADS, BLOCK, LANES), BF16), pltpu.VMEM((1, SB_HEADS, LANES, BLOCK), F32),
                        pltpu.VMEM((1, SB_HEADS, SUBLANES, BLOCK), F32)],
        compiler_params=_params(("arbitrary",)),
        name="sb_meta",
    )(q, k, v, gate, u_meta)


def _mla_chunk(q_ref, kv, mask, acc_ref, m_ref, l_ref, first=False):
    items = [(b, h) for b in range(len(kv)) for h in range(MLA_HEADS)]
    rep = lambda row: jnp.broadcast_to(row, m_ref.shape[2:])

    def scores(b, h):
        hs = slice(h * LANES, (h + 1) * LANES)
        s = _dot_nt(kv[b][0][:, hs], q_ref[b, :, hs])
        return s if mask is None else jnp.where(mask, s, NEG)

    def probs(b, h, s):
        m_new = jnp.max(s, axis=0, keepdims=True)
        if first:
            alpha = None
        else:
            m_old = m_ref[b, h, 0:1, :]
            m_new = jnp.maximum(m_old, m_new)
            alpha = jnp.exp2(m_old - m_new)
        m_ref[b, h] = rep(m_new)
        p = jnp.exp2(s - m_new)
        return p.astype(BF16), jnp.sum(p, axis=0, keepdims=True), alpha

    def update(b, h, p, l_new, alpha):
        vs = slice((h // 2) * LANES, (h // 2 + 1) * LANES)
        pv = lax.dot_general(kv[b][1][:, vs], p, (((0,), (0,)), ((), ())), preferred_element_type=F32)
        if first:
            acc_ref[b, h], l_ref[b, h] = pv, rep(l_new)
        else:
            acc_ref[b, h] = alpha * acc_ref[b, h] + pv
            l_ref[b, h] = rep(alpha * l_ref[b, h, 0:1, :] + l_new)

    ss, ps = {}, {}
    for step in range(len(items) + 2 * MLA_SKEW):
        if step < len(items):
            ss[step] = scores(*items[step])
        if 0 <= step - MLA_SKEW < len(items):
            ps[step - MLA_SKEW] = probs(*items[step - MLA_SKEW], ss.pop(step - MLA_SKEW))
        if 0 <= step - 2 * MLA_SKEW < len(items):
            update(*items[step - 2 * MLA_SKEW], *ps.pop(step - 2 * MLA_SKEW))


def _mla_finish(acc_ref, l_ref, g_ref, o_ref):
    half = LANES // 2
    for b in range(o_ref.shape[0]):
        for p in range(MLA_HEADS // 2):
            sl = slice(p * LANES, (p + 1) * LANES)
            lo = acc_ref[b, 2 * p, :half, :] * (1.0 / l_ref[b, 2 * p, 0:1, :])
            hi = acc_ref[b, 2 * p + 1, half:, :] * (1.0 / l_ref[b, 2 * p + 1, 0:1, :])
            o = jnp.concatenate([lo, hi], axis=0).T
            gate = g_ref[b, :, MLA_WIDTH + p * LANES:MLA_WIDTH + (p + 1) * LANES].astype(F32)
            o_ref[b, :, sl] = (o * _silu(gate)).astype(BF16)


def _mla_kernel(q_ref, k_ref, v_ref, km_ref, vm_ref, g_ref, o_ref, acc_ref, m_ref, l_ref):
    i = pl.program_id(1)
    T = ATT_TILE
    nb = q_ref.shape[0]

    def real_kv(j):
        start = pl.multiple_of(j * T, T)
        return [(k_ref[b, pl.ds(start, T), :], v_ref[b, pl.ds(start, T), :]) for b in range(nb)]

    def first_chunk(extra):
        parts = [real_kv(i)] + ([real_kv(0)] if extra else []) + [[(km_ref[0, N_PAD:, :], vm_ref[0, N_PAD:, :])] * nb]
        kv = [tuple(jnp.concatenate([part[b][j] for part in parts], axis=0) for j in range(2)) for b in range(nb)]
        masks = ([_iota((T, T), 0) <= _iota((T, T), 1)] + ([_iota((T, T), 0) >= 0] if extra else [])
                 + [_iota((N_META, T), 0) >= 0])
        _mla_chunk(q_ref, kv, jnp.concatenate(masks, axis=0), acc_ref, m_ref, l_ref, first=True)

    odd = i % 2

    @pl.when(odd == 1)
    def _():
        first_chunk(True)

    @pl.when(odd == 0)
    def _():
        first_chunk(False)

    def earlier(idx, carry):
        start = pl.multiple_of((odd + 2 * idx) * T, T)
        kv = [(k_ref[b, pl.ds(start, 2 * T), :], v_ref[b, pl.ds(start, 2 * T), :]) for b in range(nb)]
        _mla_chunk(q_ref, kv, None, acc_ref, m_ref, l_ref)
        return carry

    lax.fori_loop(0, i // 2, earlier, 0)
    _mla_finish(acc_ref, l_ref, g_ref, o_ref)


def _mla_attn(q, k, v, k_meta, v_meta, gate):
    B, L, _ = q.shape
    T, nb = ATT_TILE, MLA_BATCH
    assert B % nb == 0 and L % (2 * T) == 0, (B, L)
    qk_w = MLA_HEADS * LANES
    return pl.pallas_call(
        _mla_kernel,
        grid=(B // nb, L // T),
        in_specs=[pl.BlockSpec((nb, T, qk_w), lambda b, i: (b, i, 0)),
                  pl.BlockSpec((nb, L, qk_w), lambda b, i: (b, 0, 0)),
                  pl.BlockSpec((nb, L, MLA_WIDTH), lambda b, i: (b, 0, 0)),
                  pl.BlockSpec((1, BLOCK, qk_w), lambda b, i: (0, 0, 0)),
                  pl.BlockSpec((1, BLOCK, MLA_WIDTH), lambda b, i: (0, 0, 0)),
                  pl.BlockSpec((nb, T, 2 * MLA_WIDTH), lambda b, i: (b, i, 0))],
        out_specs=pl.BlockSpec((nb, T, MLA_WIDTH), lambda b, i: (b, i, 0)),
        out_shape=jax.ShapeDtypeStruct((B, L, MLA_WIDTH), BF16),
        scratch_shapes=[pltpu.VMEM((nb, MLA_HEADS, LANES, T), F32), pltpu.VMEM((nb, MLA_HEADS, SUBLANES, T), F32),
                        pltpu.VMEM((nb, MLA_HEADS, SUBLANES, T), F32)],
        compiler_params=_params(("parallel", "arbitrary")),
        name="mla_attn",
    )(q, k, v, k_meta, v_meta, gate)


def _mla_meta_kernel(q_ref, k_ref, v_ref, g_ref, o_ref, acc_ref, m_ref, l_ref):
    key, query = _iota((BLOCK, BLOCK), 0), _iota((BLOCK, BLOCK), 1)
    _mla_chunk(q_ref, [(k_ref[0], v_ref[0])], (key <= query) & (key >= N_PAD), acc_ref, m_ref, l_ref, first=True)
    _mla_finish(acc_ref, l_ref, g_ref, o_ref)


def _mla_meta(q, k, v, gate):
    qk_w = MLA_HEADS * LANES
    blk = lambda w: pl.BlockSpec((1, BLOCK, w), lambda i: (0, 0, 0))
    return pl.pallas_call(
        _mla_meta_kernel,
        grid=(1,),
        in_specs=[blk(qk_w), blk(qk_w), blk(MLA_WIDTH), blk(2 * MLA_WIDTH)],
        out_specs=blk(MLA_WIDTH),
        out_shape=jax.ShapeDtypeStruct((1, BLOCK, MLA_WIDTH), BF16),
        scratch_shapes=[pltpu.VMEM((1, MLA_HEADS, LANES, BLOCK), F32), pltpu.VMEM((1, MLA_HEADS, SUBLANES, BLOCK), F32),
                        pltpu.VMEM((1, MLA_HEADS, SUBLANES, BLOCK), F32)],
        compiler_params=_params(("arbitrary",)),
        name="mla_meta",
    )(q, k, v, gate)


def _mid_kernel(osb_ref, omla_ref, h_ref, wo_ref, g_ref, w_ref, h1_ref, q_ref, kv_ref, gate_ref):
    rows = h_ref.shape[1]
    halves = [slice(0, rows // 2), slice(rows // 2, rows)] if rows >= 2 * BLOCK else [slice(0, rows)]
    ys = [_dot(osb_ref[0, r, :], wo_ref[0:512, :]) + _dot(omla_ref[0, r, :], wo_ref[512:1024, :]) for r in halves]
    for r, y in zip(halves, ys):
        h1 = h_ref[0, r, :] + y
        h1_ref[0, r, :] = h1
        xn = _rms(h1, g_ref[...]).astype(BF16)
        q_ref[0, r, :] = (_dot(xn, w_ref[:, 0:1024]) * LOG2E).astype(BF16)
        kv_ref[0, r, :] = _dot(xn, w_ref[:, 1024:1280]).astype(BF16)
        gate_ref[0, r, :] = _dot(xn, w_ref[:, 1280:2304]).astype(BF16)


def _mid(osb, omla, h, tile, wo, g, w):
    B, L, _ = h.shape
    assert L % tile == 0, (L, tile)
    row = lambda width: pl.BlockSpec((1, tile, width), lambda b, t: (b, t, 0))
    full = lambda a: pl.BlockSpec(a.shape, lambda b, t: (0,) * a.ndim)
    return pl.pallas_call(
        _mid_kernel,
        grid=(B, L // tile),
        in_specs=[row(512), row(512), row(D_MODEL), full(wo), full(g), full(w)],
        out_specs=[row(D_MODEL), row(1024), row(256), row(1024)],
        out_shape=[jax.ShapeDtypeStruct((B, L, D_MODEL), F32),
                   jax.ShapeDtypeStruct((B, L, 1024), BF16),
                   jax.ShapeDtypeStruct((B, L, 256), BF16),
                   jax.ShapeDtypeStruct((B, L, 1024), BF16)],
        compiler_params=_params(("parallel", "parallel")),
        name="mid",
    )(osb, omla, h, wo, g, w)


def _swa_kernel(sink_ref, q_ref, kvp_ref, kvc_ref, kvm_ref, g_ref, bb_ref, mb_ref, h_ref, wo_ref, fg_ref,
                out_ref, o_scr):
    n = pl.program_id(1)
    first = jnp.where(n > 0, 0, 1)
    in_cur = _iota((BLOCK, BLOCK), 0) <= _iota((BLOCK, BLOCK), 1)
    blocks_before = (n + 1).astype(F32) * float(BLOCK)
    pairs = SWA_HEADS // 2
    half = LANES // 2
    items = [(b, p) for b in range(q_ref.shape[0]) for p in range(pairs)]
    k_sl, v_sl = slice(0, LANES), slice(LANES, 2 * LANES)
    keys_of = lambda b, sl: jnp.concatenate([kvp_ref[b, :, sl], kvc_ref[b, :, sl], kvm_ref[0, N_PAD:, sl]], axis=0)

    def scores(b, p):
        q_stack = jnp.concatenate(_split_pair(q_ref[b, :, p * LANES:(p + 1) * LANES]), axis=0)
        return _dot_nt(keys_of(b, k_sl), q_stack)

    def probs(p, z_pair):
        p_cols, invs = [], []
        for hh in range(2):
            h = p + pairs * hh
            z = z_pair[:, hh * BLOCK:(hh + 1) * BLOCK]
            slope = 2.0 ** (-8.0 * (h + 1.0) / SWA_HEADS) * LOG2E
            s_band = jnp.where(in_cur, z[BLOCK:2 * BLOCK], z[:BLOCK]) - bb_ref[first, h]
            s_meta = z[2 * BLOCK:] - mb_ref[h] - slope * blocks_before
            sink = sink_ref[h] * LOG2E
            m = jnp.maximum(jnp.maximum(jnp.max(s_band, axis=0, keepdims=True),
                                        jnp.max(s_meta, axis=0, keepdims=True)), sink)
            p_band, p_meta = jnp.exp2(s_band - m), jnp.exp2(s_meta - m)
            denom = (jnp.sum(p_band, axis=0, keepdims=True) + jnp.sum(p_meta, axis=0, keepdims=True)
                     + jnp.exp2(sink - m))
            p_cols.append(jnp.concatenate(
                [jnp.where(in_cur, 0.0, p_band), jnp.where(in_cur, p_band, 0.0), p_meta], axis=0).astype(BF16))
            invs.append(1.0 / denom)
        return jnp.concatenate(p_cols, axis=1), jnp.concatenate(invs, axis=1)

    def values(b, p, p_pair, inv):
        o_t = lax.dot_general(keys_of(b, v_sl), p_pair, (((0,), (0,)), ((), ())), preferred_element_type=F32) * inv
        o = jnp.concatenate([o_t[:half, :BLOCK], o_t[half:, BLOCK:]], axis=0).T
        sl = slice(p * LANES, (p + 1) * LANES)
        o_scr[b, :, sl] = (o * _silu(g_ref[b, :, sl].astype(F32))).astype(BF16)

    zs, ps = {}, {}
    for step in range(len(items) + 2 * SWA_SKEW):
        if step < len(items):
            zs[step] = scores(*items[step])
        if 0 <= step - SWA_SKEW < len(items):
            ps[step - SWA_SKEW] = probs(items[step - SWA_SKEW][1], zs.pop(step - SWA_SKEW))
        t = step - 2 * SWA_SKEW
        if 0 <= t < len(items):
            b, p = items[t]
            values(b, p, *ps.pop(t))
            if p == pairs - 1:
                out_ref[b] = _rms(h_ref[b] + _dot(o_scr[b], wo_ref[...]), fg_ref[...])


def _swa_bias_tables():
    col, row = np.arange(BLOCK)[:, None], np.arange(BLOCK)[None, :]
    slopes = (2.0 ** (-8.0 * (np.arange(SWA_HEADS) + 1.0) / SWA_HEADS) * LOG2E)[:, None, None]
    band = slopes * np.where(col <= row, row - col, row - col + BLOCK)
    band_first = band + np.where(col <= row, 0.0, -NEG)
    meta = (slopes * (row - col))[:, N_PAD:, :]
    return jnp.asarray(np.stack([band, band_first]), F32), jnp.asarray(meta, F32)


def _swa_attn(sinks, q, kv, kv_meta, gate, h1, wo, final_g):
    B, L, _ = q.shape
    kvw = kv.shape[-1]
    nb = SWA_BATCH
    assert B % nb == 0 and L % BLOCK == 0, (B, L)
    band_bias, meta_bias = _swa_bias_tables()
    row = lambda w: pl.BlockSpec((nb, BLOCK, w), lambda b, n, s: (b, n, 0))
    const = lambda a: pl.BlockSpec(a.shape, lambda b, n, s: (0,) * a.ndim)
    grid_spec = pltpu.PrefetchScalarGridSpec(
        num_scalar_prefetch=1,
        grid=(B // nb, L // BLOCK),
        in_specs=[row(SWA_WIDTH),
                  pl.BlockSpec((nb, BLOCK, kvw), lambda b, n, s: (b, jnp.maximum(n - 1, 0), 0)),
                  row(kvw), const(kv_meta), row(SWA_WIDTH), const(band_bias), const(meta_bias),
                  row(D_MODEL), const(wo), const(final_g)],
        out_specs=row(D_MODEL),
        scratch_shapes=[pltpu.VMEM((nb, BLOCK, SWA_WIDTH), BF16)],
    )
    return pl.pallas_call(
        _swa_kernel,
        grid_spec=grid_spec,
        out_shape=jax.ShapeDtypeStruct((B, L, D_MODEL), F32),
        compiler_params=_params(("parallel", "parallel")),
        name="swa_attn",
    )(sinks, q, kv, kv, kv_meta, gate, band_bias, meta_bias, h1, wo, final_g)


def _layer0_weights(w_in, w_uq, w_ukv):
    w_in, w_uq, w_ukv = w_in.astype(BF16), w_uq.astype(BF16), w_ukv.astype(BF16)
    q, k, v, g_sb = (w_in[:, i * 512:(i + 1) * 512] for i in range(4))
    c_q, c_kv = w_in[:, 2048:2304], w_in[:, 2304:2432]
    k_r, g_mla = w_in[:, 2432:2464], w_in[:, 2464:2976]
    half = MLA_ROPE // 2
    r1, r2 = k_r[:, :half], k_r[:, half:]
    z = lambda n: jnp.zeros((D_MODEL, n), w_in.dtype)
    kr_blk = jnp.concatenate([z(MLA_NOPE), r1, r2, z(LANES - MLA_NOPE - MLA_ROPE)], axis=1)
    w0 = jnp.concatenate([q * (SB_DIM ** -0.5), k, v, g_sb, g_mla, c_q, c_kv, kr_blk], axis=1)

    uq = w_uq.reshape(MLA_Q_LORA, MLA_HEADS, MLA_NOPE + MLA_ROPE)
    nope, u1, u2 = uq[..., :MLA_NOPE], uq[..., MLA_NOPE:MLA_NOPE + half], uq[..., MLA_NOPE + half:]
    zq = lambda n: jnp.zeros((MLA_Q_LORA, MLA_HEADS, n), w_uq.dtype)
    uq_main = jnp.concatenate([nope, u1, u2, zq(LANES - MLA_NOPE - MLA_ROPE)], axis=-1)
    uq_rot = jnp.concatenate([zq(MLA_NOPE), -u2, u1, zq(LANES - MLA_NOPE - MLA_ROPE)], axis=-1)
    wuq = jnp.concatenate([uq_main.reshape(MLA_Q_LORA, -1), uq_rot.reshape(MLA_Q_LORA, -1)], axis=1)

    ukv = w_ukv.reshape(MLA_KV_LORA, MLA_HEADS, MLA_NOPE + MLA_V)
    k_nope = jnp.concatenate([ukv[..., :MLA_NOPE],
                              jnp.zeros((MLA_KV_LORA, MLA_HEADS, LANES - MLA_NOPE), w_ukv.dtype)], axis=-1)
    wukv = jnp.concatenate([k_nope.reshape(MLA_KV_LORA, -1),
                            ukv[..., MLA_NOPE:].reshape(MLA_KV_LORA, -1)], axis=1)
    return w0, wuq, wukv


def _pair_heads(w, axis):
    shape = w.shape
    w = w.reshape(shape[:axis] + (SWA_KV_HEADS, SWA_HEADS // SWA_KV_HEADS, SWA_DIM) + shape[axis + 1:])
    return jnp.swapaxes(w, axis, axis + 1).reshape(shape)


def _layer1_weights(w_in, w_out):
    w_in, w_out = w_in.astype(BF16), w_out.astype(BF16)
    q, kv, g = w_in[:, :1024], w_in[:, 1024:1280], w_in[:, 1280:2304]
    w1 = jnp.concatenate([_pair_heads(q * (SWA_DIM ** -0.5), 1), kv, _pair_heads(g, 1)], axis=1)
    return w1, _pair_heads(w_out, 0)


def _rope_tables():
    half = MLA_ROPE // 2
    pos = np.arange(N_META + SEQ, dtype=np.float64)
    inv = ROPE_BASE ** (-np.arange(half, dtype=np.float64) / half)
    ang = pos[:, None] * inv[None, :]
    cos, sin = np.cos(ang), np.sin(ang)
    n = pos.shape[0]
    z = lambda w: np.zeros((n, w))
    tail = LANES - MLA_NOPE - MLA_ROPE
    c = np.concatenate([np.ones((n, MLA_NOPE)), cos, cos, z(tail)], axis=1)
    s = np.concatenate([z(MLA_NOPE), sin, sin, z(tail)], axis=1)
    s1 = np.concatenate([z(MLA_NOPE), -sin, z(half), z(tail)], axis=1)
    s2 = np.concatenate([z(MLA_NOPE), z(half), sin, z(tail)], axis=1)
    scale = (MLA_NOPE + MLA_ROPE) ** -0.5 * LOG2E
    tabs = (c * scale, s * scale, c, s1, s2)
    pad = lambda t: np.concatenate([np.zeros((N_PAD, LANES)), t[:N_META]], axis=0)
    return (tuple(jnp.asarray(pad(t), F32) for t in tabs), tuple(jnp.asarray(t[N_META:], F32) for t in tabs))


def _suffix_matrix(n):
    return jnp.asarray(np.where(np.arange(n)[:, None] <= np.arange(n)[None, :], -1.0, 0.0), BF16)


def kernel(x, meta, norm_g, final_g, ev_w_in, ev_q_norm_g, ev_kv_norm_g, ev_w_uq, ev_w_ukv,
           ev_w_out, od_w_in, od_sinks, od_w_out):
    w0, wuq, wukv = _layer0_weights(ev_w_in[0], ev_w_uq[0], ev_w_ukv[0])
    w1, wo1 = _layer1_weights(od_w_in[0], od_w_out[0])
    wo0 = ev_w_out[0].astype(BF16)
    tabs_meta, tabs_real = _rope_tables()
    u, u_block, u_meta = _suffix_matrix(ATT_TILE), _suffix_matrix(BLOCK), _suffix_matrix(N_META)
    l0 = (norm_g[0:1], w0, ev_q_norm_g[0:1], ev_kv_norm_g[0:1], wuq, wukv)

    hm = jnp.concatenate([jnp.zeros((N_PAD, D_MODEL), x.dtype), meta.astype(x.dtype)], axis=0)[None]
    qsb_m, ksb_m, vsb_m, gate_m, qm_m, km_m, vm_m = _proj0(hm, BLOCK, *l0, tabs_meta)
    osb_m = _sb_meta(qsb_m, ksb_m, vsb_m, gate_m, u_block)
    omla_m = _mla_meta(qm_m, km_m, vm_m, gate_m)
    _, _, kv1_m, _ = _mid(osb_m, omla_m, hm, BLOCK, wo0, norm_g[1:2], w1)

    q_sb, k_sb, v_sb, gate0, q_mla, k_mla, v_mla = _proj0(x, ROW_TILE, *l0, tabs_real)
    o_sb = _sb_attn(q_sb, k_sb, v_sb, ksb_m, vsb_m, gate0, u, u_meta)
    o_mla = _mla_attn(q_mla, k_mla, v_mla, km_m, vm_m, gate0)
    h1, q1, kv1, gate1 = _mid(o_sb, o_mla, x, ROW_TILE, wo0, norm_g[1:2], w1)
    return _swa_attn(od_sinks[0], q1, kv1, kv1_m, gate1, h1, wo1, final_g[None, :])
```

```python
import math

import numpy as np
import jax
import jax.numpy as jnp
from jax import lax
from jax.experimental import pallas as pl
from jax.experimental.pallas import tpu as pltpu

D_MODEL = 1024
SEQ = 2048
N_META = 16
BLOCK = 128
N_PAD = BLOCK - N_META
NORM_EPS = 1e-6
NEG = -1e30

SB_HEADS = 8
SB_DIM = 64
SB_WIDTH = SB_HEADS * SB_DIM
MLA_HEADS = 8
MLA_Q_LORA = 256
MLA_KV_LORA = 128
MLA_NOPE = 64
MLA_ROPE = 32
MLA_V = 64
MLA_WIDTH = MLA_HEADS * MLA_V
ROPE_BASE = 10000.0
SWA_HEADS = 16
SWA_KV_HEADS = 2
SWA_DIM = 64
SWA_WIDTH = SWA_HEADS * SWA_DIM

LANES = 128
SUBLANES = 8
ROW_TILE = 1024
ATT_TILE = 256
VMEM_LIMIT = 48 * 1024 * 1024
LOG2E = math.log2(math.e)
SB_BATCH, MLA_BATCH, SWA_BATCH = 4, 2, 8
SB_SKEW = MLA_SKEW = SWA_SKEW = 2
SOFTPLUS_CLAMP = 64.0

BF16 = jnp.bfloat16
F32 = jnp.float32


def _dot(a, b):
    return jnp.dot(a, b, preferred_element_type=F32)


def _dot_nt(a, b):
    return lax.dot_general(a, b, (((1,), (1,)), ((), ())), preferred_element_type=F32)


def _rms(x, g):
    ms = jnp.mean(x * x, axis=-1, keepdims=True)
    return x * lax.rsqrt(ms + NORM_EPS) * g


def _silu(g):
    return g * (1.0 / (1.0 + jnp.exp(-g)))


def _params(sem):
    return pltpu.CompilerParams(dimension_semantics=sem, vmem_limit_bytes=VMEM_LIMIT)


def _iota(shape, dim):
    return lax.broadcasted_iota(jnp.int32, shape, dim)


def _split_pair(x):
    lane = _iota(x.shape, 1)
    zero = jnp.zeros_like(x)
    return jnp.where(lane < LANES // 2, x, zero), jnp.where(lane >= LANES // 2, x, zero)


def _proj0_kernel(x_ref, g_ref, w_ref, qg_ref, kvg_ref, wuq_ref, wukv_ref,
                  cq_ref, sq_ref, ck_ref, sk1_ref, sk2_ref,
                  qsb_ref, ksb_ref, vsb_ref, gate_ref, qm_ref, km_ref, vm_ref):
    rows = x_ref.shape[1]
    halves = [slice(0, rows // 2), slice(rows // 2, rows)] if rows >= 2 * BLOCK else [slice(0, rows)]
    qk_w = MLA_HEADS * LANES
    for r in halves:
        xn = _rms(x_ref[0, r, :], g_ref[...]).astype(BF16)
        qsb_ref[0, r, :] = (_dot(xn, w_ref[:, 0:512]) * LOG2E).astype(BF16)
        ksb_ref[0, r, :] = _dot(xn, w_ref[:, 512:1024]).astype(BF16)
        vsb_ref[0, r, :] = _dot(xn, w_ref[:, 1024:1536]).astype(BF16)
        gate_ref[0, r, :] = _dot(xn, w_ref[:, 1536:2560]).astype(BF16)
        lat = _dot(xn, w_ref[:, 2560:3072])
        cqn = _rms(lat[:, 0:256], qg_ref[...]).astype(BF16)
        ckvn = _rms(lat[:, 256:384], kvg_ref[...]).astype(BF16)
        kr = lat[:, 384:512]
        k_rope = (kr * ck_ref[r, :] + pltpu.roll(kr, LANES - MLA_ROPE // 2, 1) * sk1_ref[r, :]
                  + pltpu.roll(kr, MLA_ROPE // 2, 1) * sk2_ref[r, :])
        q_all = _dot(cqn, wuq_ref[...])
        kv_all = _dot(ckvn, wukv_ref[...])
        cq = cq_ref[r, :]
        sq = sq_ref[r, :]
        for h in range(MLA_HEADS):
            lo, hi = h * LANES, (h + 1) * LANES
            qm_ref[0, r, lo:hi] = (q_all[:, lo:hi] * cq + q_all[:, qk_w + lo:qk_w + hi] * sq).astype(BF16)
            km_ref[0, r, lo:hi] = (kv_all[:, lo:hi] + k_rope).astype(BF16)
        vm_ref[0, r, :] = kv_all[:, qk_w:].astype(BF16)


def _proj0(h, tile, g, w, qg, kvg, wuq, wukv, tables):
    B, L, _ = h.shape
    assert L % tile == 0, (L, tile)
    row = lambda width: pl.BlockSpec((1, tile, width), lambda b, t: (b, t, 0))
    full = lambda a: pl.BlockSpec(a.shape, lambda b, t: (0,) * a.ndim)
    tab = pl.BlockSpec((tile, LANES), lambda b, t: (t, 0))
    out = lambda width: jax.ShapeDtypeStruct((B, L, width), BF16)
    return pl.pallas_call(
        _proj0_kernel,
        grid=(B, L // tile),
        in_specs=[row(D_MODEL), full(g), full(w), full(qg), full(kvg), full(wuq), full(wukv),
                  tab, tab, tab, tab, tab],
        out_specs=[row(512), row(512), row(512), row(1024), row(1024), row(1024), row(512)],
        out_shape=[out(512), out(512), out(512), out(1024), out(1024), out(1024), out(512)],
        compiler_params=_params(("parallel", "parallel")),
        name="proj0",
    )(h, g, w, qg, kvg, wuq, wukv, *tables)


def _sb_chunks(qh_ref, chunks, acc_ref, c_ref):
    nb = len(chunks[0][0])
    items = [(ci, b, h) for ci in range(len(chunks)) for b in range(nb) for h in range(SB_HEADS)]
    sl = lambda h: slice((h // 2) * LANES, (h // 2 + 1) * LANES)
    rep = lambda row: jnp.broadcast_to(row, c_ref.shape[2:])

    def scores(ci, b, h):
        return _dot_nt(chunks[ci][0][b][0][:, sl(h)], qh_ref[b, h])

    def suffix(ci, z):
        _, u, mask, _ = chunks[ci]
        sp = jnp.maximum(jnp.log2(1.0 + jnp.exp2(jnp.minimum(z, SOFTPLUS_CLAMP))), z)
        if mask is not None:
            sp = jnp.where(mask, sp, 0.0)
        return _dot(u, sp.astype(BF16))

    def weights(ci, b, h, z, sfx):
        kv, _, mask, first = chunks[ci]
        total = sfx[0:1, :]
        if first:
            a = jnp.exp2(z + sfx)
        else:
            c = c_ref[b, h, 0:1, :]
            a = jnp.exp2(z + sfx + c)
        if mask is not None:
            a = jnp.where(mask, a, 0.0)
        av = lax.dot_general(kv[b][1][:, sl(h)], a.astype(BF16), (((0,), (0,)), ((), ())),
                             preferred_element_type=F32)
        if first:
            acc_ref[b, h], c_ref[b, h] = av, rep(total)
        else:
            acc_ref[b, h] += av
            c_ref[b, h] = rep(c + total)

    zs, sfxs = {}, {}
    for step in range(len(items) + 2 * SB_SKEW):
        if step < len(items):
            zs[step] = scores(*items[step])
        if 0 <= step - SB_SKEW < len(items):
            sfxs[step - SB_SKEW] = suffix(items[step - SB_SKEW][0], zs[step - SB_SKEW])
        if 0 <= step - 2 * SB_SKEW < len(items):
            weights(*items[step - 2 * SB_SKEW], zs.pop(step - 2 * SB_SKEW), sfxs.pop(step - 2 * SB_SKEW))


def _sb_prologue(q_ref, qh_ref):
    for b in range(q_ref.shape[0]):
        for p in range(SB_HEADS // 2):
            qh_ref[b, 2 * p], qh_ref[b, 2 * p + 1] = _split_pair(q_ref[b, :, p * LANES:(p + 1) * LANES])


def _sb_epilogue(acc_ref, g_ref, o_ref):
    for b in range(o_ref.shape[0]):
        for p in range(SB_HEADS // 2):
            sl = slice(p * LANES, (p + 1) * LANES)
            o = jnp.concatenate([acc_ref[b, 2 * p, :LANES // 2, :], acc_ref[b, 2 * p + 1, LANES // 2:, :]], axis=0).T
            o_ref[b, :, sl] = (o * _silu(g_ref[b, :, sl].astype(F32))).astype(BF16)


def _sb_kernel(q_ref, k_ref, v_ref, km_ref, vm_ref, g_ref, u_ref, um_ref, o_ref, qh_ref, acc_ref, c_ref):
    i = pl.program_id(1)
    T = ATT_TILE
    nb = q_ref.shape[0]
    _sb_prologue(q_ref, qh_ref)

    def real_chunk(j, mask=None, first=False):
        start = pl.multiple_of(j * T, T)
        kv = [(k_ref[b, pl.ds(start, T), :], v_ref[b, pl.ds(start, T), :]) for b in range(nb)]
        return kv, u_ref[...], mask, first

    diagonal = lambda: real_chunk(i, _iota((T, T), 0) < _iota((T, T), 1), first=True)

    @pl.when(i == 0)
    def _():
        _sb_chunks(qh_ref, [diagonal()], acc_ref, c_ref)

    @pl.when(i > 0)
    def _():
        _sb_chunks(qh_ref, [diagonal(), real_chunk(i - 1)], acc_ref, c_ref)

    def earlier(idx, carry):
        _sb_chunks(qh_ref, [real_chunk(i - 1 - idx)], acc_ref, c_ref)
        return carry

    lax.fori_loop(1, i, earlier, 0)
    _sb_chunks(qh_ref, [([(km_ref[0, N_PAD:, :], vm_ref[0, N_PAD:, :])] * nb, um_ref[...], None, False)],
               acc_ref, c_ref)
    _sb_epilogue(acc_ref, g_ref, o_ref)


def _sb_attn(q, k, v, k_meta, v_meta, gate, u, u_meta):
    B, L, _ = q.shape
    T, nb = ATT_TILE, SB_BATCH
    assert B % nb == 0 and L % T == 0, (B, L)
    tile = pl.BlockSpec((nb, T, SB_WIDTH), lambda b, i: (b, i, 0))
    seq = pl.BlockSpec((nb, L, SB_WIDTH), lambda b, i: (b, 0, 0))
    meta = pl.BlockSpec((1, BLOCK, SB_WIDTH), lambda b, i: (0, 0, 0))
    const = lambda a: pl.BlockSpec(a.shape, lambda b, i: (0, 0))
    return pl.pallas_call(
        _sb_kernel,
        grid=(B // nb, L // T),
        in_specs=[tile, seq, seq, meta, meta, tile, const(u), const(u_meta)],
        out_specs=tile,
        out_shape=jax.ShapeDtypeStruct((B, L, SB_WIDTH), BF16),
        scratch_shapes=[pltpu.VMEM((nb, SB_HEADS, T, LANES), BF16), pltpu.VMEM((nb, SB_HEADS, LANES, T), F32),
                        pltpu.VMEM((nb, SB_HEADS, SUBLANES, T), F32)],
        compiler_params=_params(("parallel", "arbitrary")),
        name="sb_attn",
    )(q, k, v, k_meta, v_meta, gate, u, u_meta)


def _sb_meta_kernel(q_ref, k_ref, v_ref, g_ref, u_ref, o_ref, qh_ref, acc_ref, c_ref):
    _sb_prologue(q_ref, qh_ref)
    key, query = _iota((BLOCK, BLOCK), 0), _iota((BLOCK, BLOCK), 1)
    _sb_chunks(qh_ref, [([(k_ref[0], v_ref[0])], u_ref[...], (key < query) & (key >= N_PAD), True)], acc_ref, c_ref)
    _sb_epilogue(acc_ref, g_ref, o_ref)


def _sb_meta(q, k, v, gate, u_meta):
    blk = pl.BlockSpec((1, BLOCK, SB_WIDTH), lambda i: (0, 0, 0))
    return pl.pallas_call(
        _sb_meta_kernel,
        grid=(1,),
        in_specs=[blk, blk, blk, blk, pl.BlockSpec(u_meta.shape, lambda i: (0, 0))],
        out_specs=blk,
        out_shape=jax.ShapeDtypeStruct((1, BLOCK, SB_WIDTH), BF16),
        scratch_shapes=[pltpu.VMEM((1, SB_HEADS, BLOCK, LANES), BF16), pltpu.VMEM((1, SB_HEADS, LANES, BLOCK), F32),
                        pltpu.VMEM((1, SB_HEADS, SUBLANES, BLOCK), F32)],
        compiler_params=_params(("arbitrary",)),
        name="sb_meta",
    )(q, k, v, gate, u_meta)


def _mla_chunk(q_ref, kv, mask, acc_ref, m_ref, l_ref, first=False):
    items = [(b, h) for b in range(len(kv)) for h in range(MLA_HEADS)]
    rep = lambda row: jnp.broadcast_to(row, m_ref.shape[2:])

    def scores(b, h):
        hs = slice(h * LANES, (h + 1) * LANES)
        s = _dot_nt(kv[b][0][:, hs], q_ref[b, :, hs])
        return s if mask is None else jnp.where(mask, s, NEG)

    def probs(b, h, s):
        m_new = jnp.max(s, axis=0, keepdims=True)
        if first:
            alpha = None
        else:
            m_old = m_ref[b, h, 0:1, :]
            m_new = jnp.maximum(m_old, m_new)
            alpha = jnp.exp2(m_old - m_new)
        m_ref[b, h] = rep(m_new)
        p = jnp.exp2(s - m_new)
        return p.astype(BF16), jnp.sum(p, axis=0, keepdims=True), alpha

    def update(b, h, p, l_new, alpha):
        vs = slice((h // 2) * LANES, (h // 2 + 1) * LANES)
        pv = lax.dot_general(kv[b][1][:, vs], p, (((0,), (0,)), ((), ())), preferred_element_type=F32)
        if first:
            acc_ref[b, h], l_ref[b, h] = pv, rep(l_new)
        else:
            acc_ref[b, h] = alpha * acc_ref[b, h] + pv
            l_ref[b, h] = rep(alpha * l_ref[b, h, 0:1, :] + l_new)

    ss, ps = {}, {}
    for step in range(len(items) + 2 * MLA_SKEW):
        if step < len(items):
            ss[step] = scores(*items[step])
        if 0 <= step - MLA_SKEW < len(items):
            ps[step - MLA_SKEW] = probs(*items[step - MLA_SKEW], ss.pop(step - MLA_SKEW))
        if 0 <= step - 2 * MLA_SKEW < len(items):
            update(*items[step - 2 * MLA_SKEW], *ps.pop(step - 2 * MLA_SKEW))


def _mla_finish(acc_ref, l_ref, g_ref, o_ref):
    half = LANES // 2
    for b in range(o_ref.shape[0]):
        for p in range(MLA_HEADS // 2):
            sl = slice(p * LANES, (p + 1) * LANES)
            lo = acc_ref[b, 2 * p, :half, :] * (1.0 / l_ref[b, 2 * p, 0:1, :])
            hi = acc_ref[b, 2 * p + 1, half:, :] * (1.0 / l_ref[b, 2 * p + 1, 0:1, :])
            o = jnp.concatenate([lo, hi], axis=0).T
            gate = g_ref[b, :, MLA_WIDTH + p * LANES:MLA_WIDTH + (p + 1) * LANES].astype(F32)
            o_ref[b, :, sl] = (o * _silu(gate)).astype(BF16)


def _mla_kernel(q_ref, k_ref, v_ref, km_ref, vm_ref, g_ref, o_ref, acc_ref, m_ref, l_ref):
    i = pl.program_id(1)
    T = ATT_TILE
    nb = q_ref.shape[0]

    def real_kv(j):
        start = pl.multiple_of(j * T, T)
        return [(k_ref[b, pl.ds(start, T), :], v_ref[b, pl.ds(start, T), :]) for b in range(nb)]

    def first_chunk(extra):
        parts = [real_kv(i)] + ([real_kv(0)] if extra else []) + [[(km_ref[0, N_PAD:, :], vm_ref[0, N_PAD:, :])] * nb]
        kv = [tuple(jnp.concatenate([part[b][j] for part in parts], axis=0) for j in range(2)) for b in range(nb)]
        masks = ([_iota((T, T), 0) <= _iota((T, T), 1)] + ([_iota((T, T), 0) >= 0] if extra else [])
                 + [_iota((N_META, T), 0) >= 0])
        _mla_chunk(q_ref, kv, jnp.concatenate(masks, axis=0), acc_ref, m_ref, l_ref, first=True)

    odd = i % 2

    @pl.when(odd == 1)
    def _():
        first_chunk(True)

    @pl.when(odd == 0)
    def _():
        first_chunk(False)

    def earlier(idx, carry):
        start = pl.multiple_of((odd + 2 * idx) * T, T)
        kv = [(k_ref[b, pl.ds(start, 2 * T), :], v_ref[b, pl.ds(start, 2 * T), :]) for b in range(nb)]
        _mla_chunk(q_ref, kv, None, acc_ref, m_ref, l_ref)
        return carry

    lax.fori_loop(0, i // 2, earlier, 0)
    _mla_finish(acc_ref, l_ref, g_ref, o_ref)


def _mla_attn(q, k, v, k_meta, v_meta, gate):
    B, L, _ = q.shape
    T, nb = ATT_TILE, MLA_BATCH
    assert B % nb == 0 and L % (2 * T) == 0, (B, L)
    qk_w = MLA_HEADS * LANES
    return pl.pallas_call(
        _mla_kernel,
        grid=(B // nb, L // T),
        in_specs=[pl.BlockSpec((nb, T, qk_w), lambda b, i: (b, i, 0)),
                  pl.BlockSpec((nb, L, qk_w), lambda b, i: (b, 0, 0)),
                  pl.BlockSpec((nb, L, MLA_WIDTH), lambda b, i: (b, 0, 0)),
                  pl.BlockSpec((1, BLOCK, qk_w), lambda b, i: (0, 0, 0)),
                  pl.BlockSpec((1, BLOCK, MLA_WIDTH), lambda b, i: (0, 0, 0)),
                  pl.BlockSpec((nb, T, 2 * MLA_WIDTH), lambda b, i: (b, i, 0))],
        out_specs=pl.BlockSpec((nb, T, MLA_WIDTH), lambda b, i: (b, i, 0)),
        out_shape=jax.ShapeDtypeStruct((B, L, MLA_WIDTH), BF16),
        scratch_shapes=[pltpu.VMEM((nb, MLA_HEADS, LANES, T), F32), pltpu.VMEM((nb, MLA_HEADS, SUBLANES, T), F32),
                        pltpu.VMEM((nb, MLA_HEADS, SUBLANES, T), F32)],
        compiler_params=_params(("parallel", "arbitrary")),
        name="mla_attn",
    )(q, k, v, k_meta, v_meta, gate)


def _mla_meta_kernel(q_ref, k_ref, v_ref, g_ref, o_ref, acc_ref, m_ref, l_ref):
    key, query = _iota((BLOCK, BLOCK), 0), _iota((BLOCK, BLOCK), 1)
    _mla_chunk(q_ref, [(k_ref[0], v_ref[0])], (key <= query) & (key >= N_PAD), acc_ref, m_ref, l_ref, first=True)
    _mla_finish(acc_ref, l_ref, g_ref, o_ref)


def _mla_meta(q, k, v, gate):
    qk_w = MLA_HEADS * LANES
    blk = lambda w: pl.BlockSpec((1, BLOCK, w), lambda i: (0, 0, 0))
    return pl.pallas_call(
        _mla_meta_kernel,
        grid=(1,),
        in_specs=[blk(qk_w), blk(qk_w), blk(MLA_WIDTH), blk(2 * MLA_WIDTH)],
        out_specs=blk(MLA_WIDTH),
        out_shape=jax.ShapeDtypeStruct((1, BLOCK, MLA_WIDTH), BF16),
        scratch_shapes=[pltpu.VMEM((1, MLA_HEADS, LANES, BLOCK), F32), pltpu.VMEM((1, MLA_HEADS, SUBLANES, BLOCK), F32),
                        pltpu.VMEM((1, MLA_HEADS, SUBLANES, BLOCK), F32)],
        compiler_params=_params(("arbitrary",)),
        name="mla_meta",
    )(q, k, v, gate)


def _mid_kernel(osb_ref, omla_ref, h_ref, wo_ref, g_ref, w_ref, h1_ref, q_ref, kv_ref, gate_ref):
    rows = h_ref.shape[1]
    halves = [slice(0, rows // 2), slice(rows // 2, rows)] if rows >= 2 * BLOCK else [slice(0, rows)]
    ys = [_dot(osb_ref[0, r, :], wo_ref[0:512, :]) + _dot(omla_ref[0, r, :], wo_ref[512:1024, :]) for r in halves]
    for r, y in zip(halves, ys):
        h1 = h_ref[0, r, :] + y
        h1_ref[0, r, :] = h1
        xn = _rms(h1, g_ref[...]).astype(BF16)
        q_ref[0, r, :] = (_dot(xn, w_ref[:, 0:1024]) * LOG2E).astype(BF16)
        kv_ref[0, r, :] = _dot(xn, w_ref[:, 1024:1280]).astype(BF16)
        gate_ref[0, r, :] = _dot(xn, w_ref[:, 1280:2304]).astype(BF16)


def _mid(osb, omla, h, tile, wo, g, w):
    B, L, _ = h.shape
    assert L % tile == 0, (L, tile)
    row = lambda width: pl.BlockSpec((1, tile, width), lambda b, t: (b, t, 0))
    full = lambda a: pl.BlockSpec(a.shape, lambda b, t: (0,) * a.ndim)
    return pl.pallas_call(
        _mid_kernel,
        grid=(B, L // tile),
        in_specs=[row(512), row(512), row(D_MODEL), full(wo), full(g), full(w)],
        out_specs=[row(D_MODEL), row(1024), row(256), row(1024)],
        out_shape=[jax.ShapeDtypeStruct((B, L, D_MODEL), F32),
                   jax.ShapeDtypeStruct((B, L, 1024), BF16),
                   jax.ShapeDtypeStruct((B, L, 256), BF16),
                   jax.ShapeDtypeStruct((B, L, 1024), BF16)],
        compiler_params=_params(("parallel", "parallel")),
        name="mid",
    )(osb, omla, h, wo, g, w)


def _swa_kernel(sink_ref, q_ref, kvp_ref, kvc_ref, kvm_ref, g_ref, bb_ref, mb_ref, h_ref, wo_ref, fg_ref,
                out_ref, o_scr):
    n = pl.program_id(1)
    first = jnp.where(n > 0, 0, 1)
    in_cur = _iota((BLOCK, BLOCK), 0) <= _iota((BLOCK, BLOCK), 1)
    blocks_before = (n + 1).astype(F32) * float(BLOCK)
    pairs = SWA_HEADS // 2
    half = LANES // 2
    items = [(b, p) for b in range(q_ref.shape[0]) for p in range(pairs)]
    k_sl, v_sl = slice(0, LANES), slice(LANES, 2 * LANES)
    keys_of = lambda b, sl: jnp.concatenate([kvp_ref[b, :, sl], kvc_ref[b, :, sl], kvm_ref[0, N_PAD:, sl]], axis=0)

    def scores(b, p):
        q_stack = jnp.concatenate(_split_pair(q_ref[b, :, p * LANES:(p + 1) * LANES]), axis=0)
        return _dot_nt(keys_of(b, k_sl), q_stack)

    def probs(p, z_pair):
        p_cols, invs = [], []
        for hh in range(2):
            h = p + pairs * hh
            z = z_pair[:, hh * BLOCK:(hh + 1) * BLOCK]
            slope = 2.0 ** (-8.0 * (h + 1.0) / SWA_HEADS) * LOG2E
            s_band = jnp.where(in_cur, z[BLOCK:2 * BLOCK], z[:BLOCK]) - bb_ref[first, h]
            s_meta = z[2 * BLOCK:] - mb_ref[h] - slope * blocks_before
            sink = sink_ref[h] * LOG2E
            m = jnp.maximum(jnp.maximum(jnp.max(s_band, axis=0, keepdims=True),
                                        jnp.max(s_meta, axis=0, keepdims=True)), sink)
            p_band, p_meta = jnp.exp2(s_band - m), jnp.exp2(s_meta - m)
            denom = (jnp.sum(p_band, axis=0, keepdims=True) + jnp.sum(p_meta, axis=0, keepdims=True)
                     + jnp.exp2(sink - m))
            p_cols.append(jnp.concatenate(
                [jnp.where(in_cur, 0.0, p_band), jnp.where(in_cur, p_band, 0.0), p_meta], axis=0).astype(BF16))
            invs.append(1.0 / denom)
        return jnp.concatenate(p_cols, axis=1), jnp.concatenate(invs, axis=1)

    def values(b, p, p_pair, inv):
        o_t = lax.dot_general(keys_of(b, v_sl), p_pair, (((0,), (0,)), ((), ())), preferred_element_type=F32) * inv
        o = jnp.concatenate([o_t[:half, :BLOCK], o_t[half:, BLOCK:]], axis=0).T
        sl = slice(p * LANES, (p + 1) * LANES)
        o_scr[b, :, sl] = (o * _silu(g_ref[b, :, sl].astype(F32))).astype(BF16)

    zs, ps = {}, {}
    for step in range(len(items) + 2 * SWA_SKEW):
        if step < len(items):
            zs[step] = scores(*items[step])
        if 0 <= step - SWA_SKEW < len(items):
            ps[step - SWA_SKEW] = probs(items[step - SWA_SKEW][1], zs.pop(step - SWA_SKEW))
        t = step - 2 * SWA_SKEW
        if 0 <= t < len(items):
            b, p = items[t]
            values(b, p, *ps.pop(t))
            if p == pairs - 1:
                out_ref[b] = _rms(h_ref[b] + _dot(o_scr[b], wo_ref[...]), fg_ref[...])


def _swa_bias_tables():
    col, row = np.arange(BLOCK)[:, None], np.arange(BLOCK)[None, :]
    slopes = (2.0 ** (-8.0 * (np.arange(SWA_HEADS) + 1.0) / SWA_HEADS) * LOG2E)[:, None, None]
    band = slopes * np.where(col <= row, row - col, row - col + BLOCK)
    band_first = band + np.where(col <= row, 0.0, -NEG)
    meta = (slopes * (row - col))[:, N_PAD:, :]
    return jnp.asarray(np.stack([band, band_first]), F32), jnp.asarray(meta, F32)


def _swa_attn(sinks, q, kv, kv_meta, gate, h1, wo, final_g):
    B, L, _ = q.shape
    kvw = kv.shape[-1]
    nb = SWA_BATCH
    assert B % nb == 0 and L % BLOCK == 0, (B, L)
    band_bias, meta_bias = _swa_bias_tables()
    row = lambda w: pl.BlockSpec((nb, BLOCK, w), lambda b, n, s: (b, n, 0))
    const = lambda a: pl.BlockSpec(a.shape, lambda b, n, s: (0,) * a.ndim)
    grid_spec = pltpu.PrefetchScalarGridSpec(
        num_scalar_prefetch=1,
        grid=(B // nb, L // BLOCK),
        in_specs=[row(SWA_WIDTH),
                  pl.BlockSpec((nb, BLOCK, kvw), lambda b, n, s: (b, jnp.maximum(n - 1, 0), 0)),
                  row(kvw), const(kv_meta), row(SWA_WIDTH), const(band_bias), const(meta_bias),
                  row(D_MODEL), const(wo), const(final_g)],
        out_specs=row(D_MODEL),
        scratch_shapes=[pltpu.VMEM((nb, BLOCK, SWA_WIDTH), BF16)],
    )
    return pl.pallas_call(
        _swa_kernel,
        grid_spec=grid_spec,
        out_shape=jax.ShapeDtypeStruct((B, L, D_MODEL), F32),
        compiler_params=_params(("parallel", "parallel")),
        name="swa_attn",
    )(sinks, q, kv, kv, kv_meta, gate, band_bias, meta_bias, h1, wo, final_g)


def _layer0_weights(w_in, w_uq, w_ukv):
    w_in, w_uq, w_ukv = w_in.astype(BF16), w_uq.astype(BF16), w_ukv.astype(BF16)
    q, k, v, g_sb = (w_in[:, i * 512:(i + 1) * 512] for i in range(4))
    c_q, c_kv = w_in[:, 2048:2304], w_in[:, 2304:2432]
    k_r, g_mla = w_in[:, 2432:2464], w_in[:, 2464:2976]
    half = MLA_ROPE // 2
    r1, r2 = k_r[:, :half], k_r[:, half:]
    z = lambda n: jnp.zeros((D_MODEL, n), w_in.dtype)
    kr_blk = jnp.concatenate([z(MLA_NOPE), r1, r2, z(LANES - MLA_NOPE - MLA_ROPE)], axis=1)
    w0 = jnp.concatenate([q * (SB_DIM ** -0.5), k, v, g_sb, g_mla, c_q, c_kv, kr_blk], axis=1)

    uq = w_uq.reshape(MLA_Q_LORA, MLA_HEADS, MLA_NOPE + MLA_ROPE)
    nope, u1, u2 = uq[..., :MLA_NOPE], uq[..., MLA_NOPE:MLA_NOPE + half], uq[..., MLA_NOPE + half:]
    zq = lambda n: jnp.zeros((MLA_Q_LORA, MLA_HEADS, n), w_uq.dtype)
    uq_main = jnp.concatenate([nope, u1, u2, zq(LANES - MLA_NOPE - MLA_ROPE)], axis=-1)
    uq_rot = jnp.concatenate([zq(MLA_NOPE), -u2, u1, zq(LANES - MLA_NOPE - MLA_ROPE)], axis=-1)
    wuq = jnp.concatenate([uq_main.reshape(MLA_Q_LORA, -1), uq_rot.reshape(MLA_Q_LORA, -1)], axis=1)

    ukv = w_ukv.reshape(MLA_KV_LORA, MLA_HEADS, MLA_NOPE + MLA_V)
    k_nope = jnp.concatenate([ukv[..., :MLA_NOPE],
                              jnp.zeros((MLA_KV_LORA, MLA_HEADS, LANES - MLA_NOPE), w_ukv.dtype)], axis=-1)
    wukv = jnp.concatenate([k_nope.reshape(MLA_KV_LORA, -1),
                            ukv[..., MLA_NOPE:].reshape(MLA_KV_LORA, -1)], axis=1)
    return w0, wuq, wukv


def _pair_heads(w, axis):
    shape = w.shape
    w = w.reshape(shape[:axis] + (SWA_KV_HEADS, SWA_HEADS // SWA_KV_HEADS, SWA_DIM) + shape[axis + 1:])
    return jnp.swapaxes(w, axis, axis + 1).reshape(shape)


def _layer1_weights(w_in, w_out):
    w_in, w_out = w_in.astype(BF16), w_out.astype(BF16)
    q, kv, g = w_in[:, :1024], w_in[:, 1024:1280], w_in[:, 1280:2304]
    w1 = jnp.concatenate([_pair_heads(q * (SWA_DIM ** -0.5), 1), kv, _pair_heads(g, 1)], axis=1)
    return w1, _pair_heads(w_out, 0)


def _rope_tables():
    half = MLA_ROPE // 2
    pos = np.arange(N_META + SEQ, dtype=np.float64)
    inv = ROPE_BASE ** (-np.arange(half, dtype=np.float64) / half)
    ang = pos[:, None] * inv[None, :]
    cos, sin = np.cos(ang), np.sin(ang)
    n = pos.shape[0]
    z = lambda w: np.zeros((n, w))
    tail = LANES - MLA_NOPE - MLA_ROPE
    c = np.concatenate([np.ones((n, MLA_NOPE)), cos, cos, z(tail)], axis=1)
    s = np.concatenate([z(MLA_NOPE), sin, sin, z(tail)], axis=1)
    s1 = np.concatenate([z(MLA_NOPE), -sin, z(half), z(tail)], axis=1)
    s2 = np.concatenate([z(MLA_NOPE), z(half), sin, z(tail)], axis=1)
    scale = (MLA_NOPE + MLA_ROPE) ** -0.5 * LOG2E
    tabs = (c * scale, s * scale, c, s1, s2)
    pad = lambda t: np.concatenate([np.zeros((N_PAD, LANES)), t[:N_META]], axis=0)
    return (tuple(jnp.asarray(pad(t), F32) for t in tabs), tuple(jnp.asarray(t[N_META:], F32) for t in tabs))


def _suffix_matrix(n):
    return jnp.asarray(np.where(np.arange(n)[:, None] <= np.arange(n)[None, :], -1.0, 0.0), BF16)


def kernel(x, meta, norm_g, final_g, ev_w_in, ev_q_norm_g, ev_kv_norm_g, ev_w_uq, ev_w_ukv,
           ev_w_out, od_w_in, od_sinks, od_w_out):
    w0, wuq, wukv = _layer0_weights(ev_w_in[0], ev_w_uq[0], ev_w_ukv[0])
    w1, wo1 = _layer1_weights(od_w_in[0], od_w_out[0])
    wo0 = ev_w_out[0].astype(BF16)
    tabs_meta, tabs_real = _rope_tables()
    u, u_block, u_meta = _suffix_matrix(ATT_TILE), _suffix_matrix(BLOCK), _suffix_matrix(N_META)
    l0 = (norm_g[0:1], w0, ev_q_norm_g[0:1], ev_kv_norm_g[0:1], wuq, wukv)

    hm = jnp.concatenate([jnp.zeros((N_PAD, D_MODEL), x.dtype), meta.astype(x.dtype)], axis=0)[None]
    qsb_m, ksb_m, vsb_m, gate_m, qm_m, km_m, vm_m = _proj0(hm, BLOCK, *l0, tabs_meta)
    osb_m = _sb_meta(qsb_m, ksb_m, vsb_m, gate_m, u_block)
    omla_m = _mla_meta(qm_m, km_m, vm_m, gate_m)
    _, _, kv1_m, _ = _mid(osb_m, omla_m, hm, BLOCK, wo0, norm_g[1:2], w1)

    q_sb, k_sb, v_sb, gate0, q_mla, k_mla, v_mla = _proj0(x, ROW_TILE, *l0, tabs_real)
    o_sb = _sb_attn(q_sb, k_sb, v_sb, ksb_m, vsb_m, gate0, u, u_meta)
    o_mla = _mla_attn(q_mla, k_mla, v_mla, km_m, vm_m, gate0)
    h1, q1, kv1, gate1 = _mid(o_sb, o_mla, x, ROW_TILE, wo0, norm_g[1:2], w1)
    return _swa_attn(od_sinks[0], q1, kv1, kv1_m, gate1, h1, wo1, final_g[None, :])
```

```python
import math

import numpy as np
import jax
import jax.numpy as jnp
from jax import lax
from jax.experimental import pallas as pl
from jax.experimental.pallas import tpu as pltpu

D_MODEL = 1024
SEQ = 2048
N_META = 16
BLOCK = 128
N_PAD = BLOCK - N_META
NORM_EPS = 1e-6
NEG = -1e30

SB_HEADS = 8
SB_DIM = 64
SB_WIDTH = SB_HEADS * SB_DIM
MLA_HEADS = 8
MLA_Q_LORA = 256
MLA_KV_LORA = 128
MLA_NOPE = 64
MLA_ROPE = 32
MLA_V = 64
MLA_WIDTH = MLA_HEADS * MLA_V
ROPE_BASE = 10000.0
SWA_HEADS = 16
SWA_KV_HEADS = 2
SWA_DIM = 64
SWA_WIDTH = SWA_HEADS * SWA_DIM

LANES = 128
SUBLANES = 8
ROW_TILE = 1024
ATT_TILE = 256
VMEM_LIMIT = 48 * 1024 * 1024
LOG2E = math.log2(math.e)
SB_BATCH, MLA_BATCH, SWA_BATCH = 4, 2, 8
SB_SKEW = MLA_SKEW = SWA_SKEW = 2
DEAD_CARRY = -256.0
SOFTPLUS_CLAMP = 64.0

BF16 = jnp.bfloat16
F32 = jnp.float32


def _dot(a, b):
    return jnp.dot(a, b, preferred_element_type=F32)


def _dot_nt(a, b):
    return lax.dot_general(a, b, (((1,), (1,)), ((), ())), preferred_element_type=F32)


def _rms(x, g):
    ms = jnp.mean(x * x, axis=-1, keepdims=True)
    return x * lax.rsqrt(ms + NORM_EPS) * g


def _silu(g):
    return g * (1.0 / (1.0 + jnp.exp(-g)))


def _params(sem):
    return pltpu.CompilerParams(dimension_semantics=sem, vmem_limit_bytes=VMEM_LIMIT)


def _iota(shape, dim):
    return lax.broadcasted_iota(jnp.int32, shape, dim)


def _split_pair(x):
    lane = _iota(x.shape, 1)
    zero = jnp.zeros_like(x)
    return jnp.where(lane < LANES // 2, x, zero), jnp.where(lane >= LANES // 2, x, zero)


def _proj0_kernel(x_ref, g_ref, w_ref, qg_ref, kvg_ref, wuq_ref, wukv_ref,
                  cq_ref, sq_ref, ck_ref, sk1_ref, sk2_ref,
                  qsb_ref, ksb_ref, vsb_ref, gate_ref, qm_ref, km_ref, vm_ref):
    rows = x_ref.shape[1]
    halves = [slice(0, rows // 2), slice(rows // 2, rows)] if rows >= 2 * BLOCK else [slice(0, rows)]
    qk_w = MLA_HEADS * LANES
    for r in halves:
        xn = _rms(x_ref[0, r, :], g_ref[...]).astype(BF16)
        qsb_ref[0, r, :] = (_dot(xn, w_ref[:, 0:512]) * LOG2E).astype(BF16)
        ksb_ref[0, r, :] = _dot(xn, w_ref[:, 512:1024]).astype(BF16)
        vsb_ref[0, r, :] = _dot(xn, w_ref[:, 1024:1536]).astype(BF16)
        gate_ref[0, r, :] = _dot(xn, w_ref[:, 1536:2560]).astype(BF16)
        lat = _dot(xn, w_ref[:, 2560:3072])
        cqn = _rms(lat[:, 0:256], qg_ref[...]).astype(BF16)
        ckvn = _rms(lat[:, 256:384], kvg_ref[...]).astype(BF16)
        kr = lat[:, 384:512]
        k_rope = (kr * ck_ref[r, :] + pltpu.roll(kr, LANES - MLA_ROPE // 2, 1) * sk1_ref[r, :]
                  + pltpu.roll(kr, MLA_ROPE // 2, 1) * sk2_ref[r, :])
        q_all = _dot(cqn, wuq_ref[...])
        kv_all = _dot(ckvn, wukv_ref[...])
        cq = cq_ref[r, :]
        sq = sq_ref[r, :]
        for h in range(MLA_HEADS):
            lo, hi = h * LANES, (h + 1) * LANES
            qm_ref[0, r, lo:hi] = (q_all[:, lo:hi] * cq + q_all[:, qk_w + lo:qk_w + hi] * sq).astype(BF16)
            km_ref[0, r, lo:hi] = (kv_all[:, lo:hi] + k_rope).astype(BF16)
        vm_ref[0, r, :] = kv_all[:, qk_w:].astype(BF16)


def _proj0(h, tile, g, w, qg, kvg, wuq, wukv, tables):
    B, L, _ = h.shape
    assert L % tile == 0, (L, tile)
    row = lambda width: pl.BlockSpec((1, tile, width), lambda b, t: (b, t, 0))
    full = lambda a: pl.BlockSpec(a.shape, lambda b, t: (0,) * a.ndim)
    tab = pl.BlockSpec((tile, LANES), lambda b, t: (t, 0))
    out = lambda width: jax.ShapeDtypeStruct((B, L, width), BF16)
    return pl.pallas_call(
        _proj0_kernel,
        grid=(B, L // tile),
        in_specs=[row(D_MODEL), full(g), full(w), full(qg), full(kvg), full(wuq), full(wukv),
                  tab, tab, tab, tab, tab],
        out_specs=[row(512), row(512), row(512), row(1024), row(1024), row(1024), row(512)],
        out_shape=[out(512), out(512), out(512), out(1024), out(1024), out(1024), out(512)],
        compiler_params=_params(("parallel", "parallel")),
        name="proj0",
    )(h, g, w, qg, kvg, wuq, wukv, *tables)


def _sb_chunks(qh_ref, chunks, acc_ref, c_ref):
    nb = len(chunks[0][0])
    items = [(ci, b, h) for ci in range(len(chunks)) for b in range(nb) for h in range(SB_HEADS)]
    sl = lambda h: slice((h // 2) * LANES, (h // 2 + 1) * LANES)
    rep = lambda row: jnp.broadcast_to(row, c_ref.shape[2:])

    def scores(ci, b, h):
        return _dot_nt(chunks[ci][0][b][0][:, sl(h)], qh_ref[b, h])

    def suffix(ci, z):
        _, u, mask, _ = chunks[ci]
        sp = jnp.maximum(jnp.log2(1.0 + jnp.exp2(jnp.minimum(z, SOFTPLUS_CLAMP))), z)
        if mask is not None:
            sp = jnp.where(mask, sp, 0.0)
        return _dot(u, sp.astype(BF16))

    def weights(ci, b, h, z, sfx):
        kv, _, mask, first = chunks[ci]
        total = sfx[0:1, :]
        if first:
            a = jnp.exp2(z + sfx)
        else:
            c = c_ref[b, h, 0:1, :]
            a = jnp.exp2(z + sfx + c)
        if mask is not None:
            a = jnp.where(mask, a, 0.0)
        av = lax.dot_general(kv[b][1][:, sl(h)], a.astype(BF16), (((0,), (0,)), ((), ())),
                             preferred_element_type=F32)
        if first:
            acc_ref[b, h], c_ref[b, h] = av, rep(total)
        else:
            acc_ref[b, h] += av
            c_ref[b, h] = rep(c + total)

    zs, sfxs = {}, {}
    for step in range(len(items) + 2 * SB_SKEW):
        if step < len(items):
            zs[step] = scores(*items[step])
        if 0 <= step - SB_SKEW < len(items):
            sfxs[step - SB_SKEW] = suffix(items[step - SB_SKEW][0], zs[step - SB_SKEW])
        if 0 <= step - 2 * SB_SKEW < len(items):
            weights(*items[step - 2 * SB_SKEW], zs.pop(step - 2 * SB_SKEW), sfxs.pop(step - 2 * SB_SKEW))


def _sb_prologue(q_ref, qh_ref):
    for b in range(q_ref.shape[0]):
        for p in range(SB_HEADS // 2):
            qh_ref[b, 2 * p], qh_ref[b, 2 * p + 1] = _split_pair(q_ref[b, :, p * LANES:(p + 1) * LANES])


def _sb_epilogue(acc_ref, g_ref, o_ref):
    for b in range(o_ref.shape[0]):
        for p in range(SB_HEADS // 2):
            sl = slice(p * LANES, (p + 1) * LANES)
            o = jnp.concatenate([acc_ref[b, 2 * p, :LANES // 2, :], acc_ref[b, 2 * p + 1, LANES // 2:, :]], axis=0).T
            o_ref[b, :, sl] = (o * _silu(g_ref[b, :, sl].astype(F32))).astype(BF16)


def _sb_kernel(q_ref, k_ref, v_ref, km_ref, vm_ref, g_ref, u_ref, um_ref, o_ref, qh_ref, acc_ref, c_ref):
    i = pl.program_id(1)
    T = ATT_TILE
    nb = q_ref.shape[0]
    _sb_prologue(q_ref, qh_ref)

    def real_chunk(j, mask=None, first=False):
        start = pl.multiple_of(j * T, T)
        kv = [(k_ref[b, pl.ds(start, T), :], v_ref[b, pl.ds(start, T), :]) for b in range(nb)]
        return kv, u_ref[...], mask, first

    diagonal = lambda: real_chunk(i, _iota((T, T), 0) < _iota((T, T), 1), first=True)

    @pl.when(i == 0)
    def _():
        _sb_chunks(qh_ref, [diagonal()], acc_ref, c_ref)

    @pl.when(i > 0)
    def _():
        _sb_chunks(qh_ref, [diagonal(), real_chunk(i - 1)], acc_ref, c_ref)

    def alive():
        return jnp.max(c_ref[...]) > DEAD_CARRY

    def earlier(state):
        idx, _ = state
        _sb_chunks(qh_ref, [real_chunk(i - 1 - idx)], acc_ref, c_ref)
        return idx + 1, alive()

    _, still_alive = lax.while_loop(lambda state: (state[0] < i) & state[1], earlier, (jnp.int32(1), alive()))

    @pl.when(still_alive)
    def _():
        _sb_chunks(qh_ref, [([(km_ref[0, N_PAD:, :], vm_ref[0, N_PAD:, :])] * nb, um_ref[...], None, False)],
                   acc_ref, c_ref)

    _sb_epilogue(acc_ref, g_ref, o_ref)


def _sb_attn(q, k, v, k_meta, v_meta, gate, u, u_meta):
    B, L, _ = q.shape
    T, nb = ATT_TILE, SB_BATCH
    assert B % nb == 0 and L % T == 0, (B, L)
    tile = pl.BlockSpec((nb, T, SB_WIDTH), lambda b, i: (b, i, 0))
    seq = pl.BlockSpec((nb, L, SB_WIDTH), lambda b, i: (b, 0, 0))
    meta = pl.BlockSpec((1, BLOCK, SB_WIDTH), lambda b, i: (0, 0, 0))
    const = lambda a: pl.BlockSpec(a.shape, lambda b, i: (0, 0))
    return pl.pallas_call(
        _sb_kernel,
        grid=(B // nb, L // T),
        in_specs=[tile, seq, seq, meta, meta, tile, const(u), const(u_meta)],
        out_specs=tile,
        out_shape=jax.ShapeDtypeStruct((B, L, SB_WIDTH), BF16),
        scratch_shapes=[pltpu.VMEM((nb, SB_HEADS, T, LANES), BF16), pltpu.VMEM((nb, SB_HEADS, LANES, T), F32),
                        pltpu.VMEM((nb, SB_HEADS, SUBLANES, T), F32)],
        compiler_params=_params(("parallel", "arbitrary")),
        name="sb_attn",
    )(q, k, v, k_meta, v_meta, gate, u, u_meta)


def _sb_meta_kernel(q_ref, k_ref, v_ref, g_ref, u_ref, o_ref, qh_ref, acc_ref, c_ref):
    _sb_prologue(q_ref, qh_ref)
    key, query = _iota((BLOCK, BLOCK), 0), _iota((BLOCK, BLOCK), 1)
    _sb_chunks(qh_ref, [([(k_ref[0], v_ref[0])], u_ref[...], (key < query) & (key >= N_PAD), True)], acc_ref, c_ref)
    _sb_epilogue(acc_ref, g_ref, o_ref)


def _sb_meta(q, k, v, gate, u_meta):
    blk = pl.BlockSpec((1, BLOCK, SB_WIDTH), lambda i: (0, 0, 0))
    return pl.pallas_call(
        _sb_meta_kernel,
        grid=(1,),
        in_specs=[blk, blk, blk, blk, pl.BlockSpec(u_meta.shape, lambda i: (0, 0))],
        out_specs=blk,
        out_shape=jax.ShapeDtypeStruct((1, BLOCK, SB_WIDTH), BF16),
        scratch_shapes=[pltpu.VMEM((1, SB_HEADS, BLOCK, LANES), BF16), pltpu.VMEM((1, SB_HEADS, LANES, BLOCK), F32),
                        pltpu.VMEM((1, SB_HEADS, SUBLANES, BLOCK), F32)],
        compiler_params=_params(("arbitrary",)),
        name="sb_meta",
    )(q, k, v, gate, u_meta)


def _mla_chunks(q_ref, chunks, acc_ref, m_ref, l_ref):
    nb = len(chunks[0][0])
    items = [(ci, b, h) for ci in range(len(chunks)) for b in range(nb) for h in range(MLA_HEADS)]
    rep = lambda row: jnp.broadcast_to(row, m_ref.shape[2:])

    def scores(ci, b, h):
        kv, mask, _ = chunks[ci]
        hs = slice(h * LANES, (h + 1) * LANES)
        s = _dot_nt(kv[b][0][:, hs], q_ref[b, :, hs])
        return s if mask is None else jnp.where(mask, s, NEG)

    def probs(ci, b, h, s):
        m_new = jnp.max(s, axis=0, keepdims=True)
        if chunks[ci][2]:
            alpha = None
        else:
            m_old = m_ref[b, h, 0:1, :]
            m_new = jnp.maximum(m_old, m_new)
            alpha = jnp.exp2(m_old - m_new)
        m_ref[b, h] = rep(m_new)
        p = jnp.exp2(s - m_new)
        return p.astype(BF16), jnp.sum(p, axis=0, keepdims=True), alpha

    def update(ci, b, h, p, l_new, alpha):
        kv, _, first = chunks[ci]
        vs = slice((h // 2) * LANES, (h // 2 + 1) * LANES)
        pv = lax.dot_general(kv[b][1][:, vs], p, (((0,), (0,)), ((), ())), preferred_element_type=F32)
        if first:
            acc_ref[b, h], l_ref[b, h] = pv, rep(l_new)
        else:
            acc_ref[b, h] = alpha * acc_ref[b, h] + pv
            l_ref[b, h] = rep(alpha * l_ref[b, h, 0:1, :] + l_new)

    ss, ps = {}, {}
    for step in range(len(items) + 2 * MLA_SKEW):
        if step < len(items):
            ss[step] = scores(*items[step])
        if 0 <= step - MLA_SKEW < len(items):
            ps[step - MLA_SKEW] = probs(*items[step - MLA_SKEW], ss.pop(step - MLA_SKEW))
        if 0 <= step - 2 * MLA_SKEW < len(items):
            update(*items[step - 2 * MLA_SKEW], *ps.pop(step - 2 * MLA_SKEW))


def _mla_finish(acc_ref, l_ref, g_ref, o_ref):
    half = LANES // 2
    for b in range(o_ref.shape[0]):
        for p in range(MLA_HEADS // 2):
            sl = slice(p * LANES, (p + 1) * LANES)
            lo = acc_ref[b, 2 * p, :half, :] * (1.0 / l_ref[b, 2 * p, 0:1, :])
            hi = acc_ref[b, 2 * p + 1, half:, :] * (1.0 / l_ref[b, 2 * p + 1, 0:1, :])
            o = jnp.concatenate([lo, hi], axis=0).T
            gate = g_ref[b, :, MLA_WIDTH + p * LANES:MLA_WIDTH + (p + 1) * LANES].astype(F32)
            o_ref[b, :, sl] = (o * _silu(gate)).astype(BF16)


def _mla_kernel(q_ref, k_ref, v_ref, km_ref, vm_ref, g_ref, o_ref, acc_ref, m_ref, l_ref):
    i = pl.program_id(1)
    T = ATT_TILE
    nb = q_ref.shape[0]

    def real_kv(j):
        start = pl.multiple_of(j * T, T)
        return [(k_ref[b, pl.ds(start, T), :], v_ref[b, pl.ds(start, T), :]) for b in range(nb)]

    def pair_kv(idx, odd):
        start = pl.multiple_of((odd + 2 * idx) * T, T)
        return [(k_ref[b, pl.ds(start, 2 * T), :], v_ref[b, pl.ds(start, 2 * T), :]) for b in range(nb)]

    def first_chunks(extra, with_pair):
        parts = [real_kv(i)] + ([real_kv(0)] if extra else []) + [[(km_ref[0, N_PAD:, :], vm_ref[0, N_PAD:, :])] * nb]
        kv = [tuple(jnp.concatenate([part[b][j] for part in parts], axis=0) for j in range(2)) for b in range(nb)]
        masks = ([_iota((T, T), 0) <= _iota((T, T), 1)] + ([_iota((T, T), 0) >= 0] if extra else [])
                 + [_iota((N_META, T), 0) >= 0])
        chunks = [(kv, jnp.concatenate(masks, axis=0), True)]
        if with_pair:
            chunks.append((pair_kv(0, int(extra)), None, False))
        _mla_chunks(q_ref, chunks, acc_ref, m_ref, l_ref)

    odd = i % 2
    for extra in (False, True):
        for with_pair in (False, True):
            @pl.when((odd == int(extra)) & ((i >= 2) == with_pair))
            def _(extra=extra, with_pair=with_pair):
                first_chunks(extra, with_pair)

    def earlier(idx, carry):
        _mla_chunks(q_ref, [(pair_kv(idx, odd), None, False)], acc_ref, m_ref, l_ref)
        return carry

    lax.fori_loop(1, i // 2, earlier, 0)
    _mla_finish(acc_ref, l_ref, g_ref, o_ref)


def _mla_attn(q, k, v, k_meta, v_meta, gate):
    B, L, _ = q.shape
    T, nb = ATT_TILE, MLA_BATCH
    assert B % nb == 0 and L % (2 * T) == 0, (B, L)
    qk_w = MLA_HEADS * LANES
    return pl.pallas_call(
        _mla_kernel,
        grid=(B // nb, L // T),
        in_specs=[pl.BlockSpec((nb, T, qk_w), lambda b, i: (b, i, 0)),
                  pl.BlockSpec((nb, L, qk_w), lambda b, i: (b, 0, 0)),
                  pl.BlockSpec((nb, L, MLA_WIDTH), lambda b, i: (b, 0, 0)),
                  pl.BlockSpec((1, BLOCK, qk_w), lambda b, i: (0, 0, 0)),
                  pl.BlockSpec((1, BLOCK, MLA_WIDTH), lambda b, i: (0, 0, 0)),
                  pl.BlockSpec((nb, T, 2 * MLA_WIDTH), lambda b, i: (b, i, 0))],
        out_specs=pl.BlockSpec((nb, T, MLA_WIDTH), lambda b, i: (b, i, 0)),
        out_shape=jax.ShapeDtypeStruct((B, L, MLA_WIDTH), BF16),
        scratch_shapes=[pltpu.VMEM((nb, MLA_HEADS, LANES, T), F32), pltpu.VMEM((nb, MLA_HEADS, SUBLANES, T), F32),
                        pltpu.VMEM((nb, MLA_HEADS, SUBLANES, T), F32)],
        compiler_params=_params(("parallel", "arbitrary")),
        name="mla_attn",
    )(q, k, v, k_meta, v_meta, gate)


def _mla_meta_kernel(q_ref, k_ref, v_ref, g_ref, o_ref, acc_ref, m_ref, l_ref):
    key, query = _iota((BLOCK, BLOCK), 0), _iota((BLOCK, BLOCK), 1)
    _mla_chunks(q_ref, [([(k_ref[0], v_ref[0])], (key <= query) & (key >= N_PAD), True)], acc_ref, m_ref, l_ref)
    _mla_finish(acc_ref, l_ref, g_ref, o_ref)


def _mla_meta(q, k, v, gate):
    qk_w = MLA_HEADS * LANES
    blk = lambda w: pl.BlockSpec((1, BLOCK, w), lambda i: (0, 0, 0))
    return pl.pallas_call(
        _mla_meta_kernel,
        grid=(1,),
        in_specs=[blk(qk_w), blk(qk_w), blk(MLA_WIDTH), blk(2 * MLA_WIDTH)],
        out_specs=blk(MLA_WIDTH),
        out_shape=jax.ShapeDtypeStruct((1, BLOCK, MLA_WIDTH), BF16),
        scratch_shapes=[pltpu.VMEM((1, MLA_HEADS, LANES, BLOCK), F32), pltpu.VMEM((1, MLA_HEADS, SUBLANES, BLOCK), F32),
                        pltpu.VMEM((1, MLA_HEADS, SUBLANES, BLOCK), F32)],
        compiler_params=_params(("arbitrary",)),
        name="mla_meta",
    )(q, k, v, gate)


def _mid_kernel(osb_ref, omla_ref, h_ref, wo_ref, g_ref, w_ref, h1_ref, q_ref, kv_ref, gate_ref):
    rows = h_ref.shape[1]
    halves = [slice(0, rows // 2), slice(rows // 2, rows)] if rows >= 2 * BLOCK else [slice(0, rows)]
    ys = [_dot(osb_ref[0, r, :], wo_ref[0:512, :]) + _dot(omla_ref[0, r, :], wo_ref[512:1024, :]) for r in halves]
    for r, y in zip(halves, ys):
        h1 = h_ref[0, r, :] + y
        h1_ref[0, r, :] = h1
        xn = _rms(h1, g_ref[...]).astype(BF16)
        q_ref[0, r, :] = (_dot(xn, w_ref[:, 0:1024]) * LOG2E).astype(BF16)
        kv_ref[0, r, :] = _dot(xn, w_ref[:, 1024:1280]).astype(BF16)
        gate_ref[0, r, :] = _dot(xn, w_ref[:, 1280:2304]).astype(BF16)


def _mid(osb, omla, h, tile, wo, g, w):
    B, L, _ = h.shape
    assert L % tile == 0, (L, tile)
    row = lambda width: pl.BlockSpec((1, tile, width), lambda b, t: (b, t, 0))
    full = lambda a: pl.BlockSpec(a.shape, lambda b, t: (0,) * a.ndim)
    return pl.pallas_call(
        _mid_kernel,
        grid=(B, L // tile),
        in_specs=[row(512), row(512), row(D_MODEL), full(wo), full(g), full(w)],
        out_specs=[row(D_MODEL), row(1024), row(256), row(1024)],
        out_shape=[jax.ShapeDtypeStruct((B, L, D_MODEL), F32),
                   jax.ShapeDtypeStruct((B, L, 1024), BF16),
                   jax.ShapeDtypeStruct((B, L, 256), BF16),
                   jax.ShapeDtypeStruct((B, L, 1024), BF16)],
        compiler_params=_params(("parallel", "parallel")),
        name="mid",
    )(osb, omla, h, wo, g, w)


def _swa_kernel(sink_ref, q_ref, kvp_ref, kvc_ref, kvm_ref, g_ref, bb_ref, mb_ref, h_ref, wo_ref, fg_ref,
                out_ref, o_scr):
    n = pl.program_id(1)
    first = jnp.where(n > 0, 0, 1)
    in_cur = _iota((BLOCK, BLOCK), 0) <= _iota((BLOCK, BLOCK), 1)
    blocks_before = (n + 1).astype(F32) * float(BLOCK)
    pairs = SWA_HEADS // 2
    half = LANES // 2
    items = [(b, p) for b in range(q_ref.shape[0]) for p in range(pairs)]
    k_sl, v_sl = slice(0, LANES), slice(LANES, 2 * LANES)
    keys_of = lambda b, sl: jnp.concatenate([kvp_ref[b, :, sl], kvc_ref[b, :, sl], kvm_ref[0, N_PAD:, sl]], axis=0)

    def scores(b, p):
        q_stack = jnp.concatenate(_split_pair(q_ref[b, :, p * LANES:(p + 1) * LANES]), axis=0)
        return _dot_nt(keys_of(b, k_sl), q_stack)

    def probs(p, z_pair):
        p_cols, invs = [], []
        for hh in range(2):
            h = p + pairs * hh
            z = z_pair[:, hh * BLOCK:(hh + 1) * BLOCK]
            slope = 2.0 ** (-8.0 * (h + 1.0) / SWA_HEADS) * LOG2E
            s_band = jnp.where(in_cur, z[BLOCK:2 * BLOCK], z[:BLOCK]) - bb_ref[first, h]
            s_meta = z[2 * BLOCK:] - mb_ref[h] - slope * blocks_before
            sink = sink_ref[h] * LOG2E
            m = jnp.maximum(jnp.maximum(jnp.max(s_band, axis=0, keepdims=True),
                                        jnp.max(s_meta, axis=0, keepdims=True)), sink)
            p_band, p_meta = jnp.exp2(s_band - m), jnp.exp2(s_meta - m)
            denom = (jnp.sum(p_band, axis=0, keepdims=True) + jnp.sum(p_meta, axis=0, keepdims=True)
                     + jnp.exp2(sink - m))
            p_cols.append(jnp.concatenate(
                [jnp.where(in_cur, 0.0, p_band), jnp.where(in_cur, p_band, 0.0), p_meta], axis=0).astype(BF16))
            invs.append(1.0 / denom)
        return jnp.concatenate(p_cols, axis=1), jnp.concatenate(invs, axis=1)

    def values(b, p, p_pair, inv):
        o_t = lax.dot_general(keys_of(b, v_sl), p_pair, (((0,), (0,)), ((), ())), preferred_element_type=F32) * inv
        o = jnp.concatenate([o_t[:half, :BLOCK], o_t[half:, BLOCK:]], axis=0).T
        sl = slice(p * LANES, (p + 1) * LANES)
        o_scr[b, :, sl] = (o * _silu(g_ref[b, :, sl].astype(F32))).astype(BF16)

    zs, ps = {}, {}
    for step in range(len(items) + 2 * SWA_SKEW):
        if step < len(items):
            zs[step] = scores(*items[step])
        if 0 <= step - SWA_SKEW < len(items):
            ps[step - SWA_SKEW] = probs(items[step - SWA_SKEW][1], zs.pop(step - SWA_SKEW))
        t = step - 2 * SWA_SKEW
        if 0 <= t < len(items):
            b, p = items[t]
            values(b, p, *ps.pop(t))
            if p == pairs - 1:
                out_ref[b] = _rms(h_ref[b] + _dot(o_scr[b], wo_ref[...]), fg_ref[...])


def _swa_bias_tables():
    col, row = np.arange(BLOCK)[:, None], np.arange(BLOCK)[None, :]
    slopes = (2.0 ** (-8.0 * (np.arange(SWA_HEADS) + 1.0) / SWA_HEADS) * LOG2E)[:, None, None]
    band = slopes * np.where(col <= row, row - col, row - col + BLOCK)
    band_first = band + np.where(col <= row, 0.0, -NEG)
    meta = (slopes * (row - col))[:, N_PAD:, :]
    return jnp.asarray(np.stack([band, band_first]), F32), jnp.asarray(meta, F32)


def _swa_attn(sinks, q, kv, kv_meta, gate, h1, wo, final_g):
    B, L, _ = q.shape
    kvw = kv.shape[-1]
    nb = SWA_BATCH
    assert B % nb == 0 and L % BLOCK == 0, (B, L)
    band_bias, meta_bias = _swa_bias_tables()
    row = lambda w: pl.BlockSpec((nb, BLOCK, w), lambda b, n, s: (b, n, 0))
    const = lambda a: pl.BlockSpec(a.shape, lambda b, n, s: (0,) * a.ndim)
    grid_spec = pltpu.PrefetchScalarGridSpec(
        num_scalar_prefetch=1,
        grid=(B // nb, L // BLOCK),
        in_specs=[row(SWA_WIDTH),
                  pl.BlockSpec((nb, BLOCK, kvw), lambda b, n, s: (b, jnp.maximum(n - 1, 0), 0)),
                  row(kvw), const(kv_meta), row(SWA_WIDTH), const(band_bias), const(meta_bias),
                  row(D_MODEL), const(wo), const(final_g)],
        out_specs=row(D_MODEL),
        scratch_shapes=[pltpu.VMEM((nb, BLOCK, SWA_WIDTH), BF16)],
    )
    return pl.pallas_call(
        _swa_kernel,
        grid_spec=grid_spec,
        out_shape=jax.ShapeDtypeStruct((B, L, D_MODEL), F32),
        compiler_params=_params(("parallel", "parallel")),
        name="swa_attn",
    )(sinks, q, kv, kv, kv_meta, gate, band_bias, meta_bias, h1, wo, final_g)


def _layer0_weights(w_in, w_uq, w_ukv):
    w_in, w_uq, w_ukv = w_in.astype(BF16), w_uq.astype(BF16), w_ukv.astype(BF16)
    q, k, v, g_sb = (w_in[:, i * 512:(i + 1) * 512] for i in range(4))
    c_q, c_kv = w_in[:, 2048:2304], w_in[:, 2304:2432]
    k_r, g_mla = w_in[:, 2432:2464], w_in[:, 2464:2976]
    half = MLA_ROPE // 2
    r1, r2 = k_r[:, :half], k_r[:, half:]
    z = lambda n: jnp.zeros((D_MODEL, n), w_in.dtype)
    kr_blk = jnp.concatenate([z(MLA_NOPE), r1, r2, z(LANES - MLA_NOPE - MLA_ROPE)], axis=1)
    w0 = jnp.concatenate([q * (SB_DIM ** -0.5), k, v, g_sb, g_mla, c_q, c_kv, kr_blk], axis=1)

    uq = w_uq.reshape(MLA_Q_LORA, MLA_HEADS, MLA_NOPE + MLA_ROPE)
    nope, u1, u2 = uq[..., :MLA_NOPE], uq[..., MLA_NOPE:MLA_NOPE + half], uq[..., MLA_NOPE + half:]
    zq = lambda n: jnp.zeros((MLA_Q_LORA, MLA_HEADS, n), w_uq.dtype)
    uq_main = jnp.concatenate([nope, u1, u2, zq(LANES - MLA_NOPE - MLA_ROPE)], axis=-1)
    uq_rot = jnp.concatenate([zq(MLA_NOPE), -u2, u1, zq(LANES - MLA_NOPE - MLA_ROPE)], axis=-1)
    wuq = jnp.concatenate([uq_main.reshape(MLA_Q_LORA, -1), uq_rot.reshape(MLA_Q_LORA, -1)], axis=1)

    ukv = w_ukv.reshape(MLA_KV_LORA, MLA_HEADS, MLA_NOPE + MLA_V)
    k_nope = jnp.concatenate([ukv[..., :MLA_NOPE],
                              jnp.zeros((MLA_KV_LORA, MLA_HEADS, LANES - MLA_NOPE), w_ukv.dtype)], axis=-1)
    wukv = jnp.concatenate([k_nope.reshape(MLA_KV_LORA, -1),
                            ukv[..., MLA_NOPE:].reshape(MLA_KV_LORA, -1)], axis=1)
    return w0, wuq, wukv


def _pair_heads(w, axis):
    shape = w.shape
    w = w.reshape(shape[:axis] + (SWA_KV_HEADS, SWA_HEADS // SWA_KV_HEADS, SWA_DIM) + shape[axis + 1:])
    return jnp.swapaxes(w, axis, axis + 1).reshape(shape)


def _layer1_weights(w_in, w_out):
    w_in, w_out = w_in.astype(BF16), w_out.astype(BF16)
    q, kv, g = w_in[:, :1024], w_in[:, 1024:1280], w_in[:, 1280:2304]
    w1 = jnp.concatenate([_pair_heads(q * (SWA_DIM ** -0.5), 1), kv, _pair_heads(g, 1)], axis=1)
    return w1, _pair_heads(w_out, 0)


def _rope_tables():
    half = MLA_ROPE // 2
    pos = np.arange(N_META + SEQ, dtype=np.float64)
    inv = ROPE_BASE ** (-np.arange(half, dtype=np.float64) / half)
    ang = pos[:, None] * inv[None, :]
    cos, sin = np.cos(ang), np.sin(ang)
    n = pos.shape[0]
    z = lambda w: np.zeros((n, w))
    tail = LANES - MLA_NOPE - MLA_ROPE
    c = np.concatenate([np.ones((n, MLA_NOPE)), cos, cos, z(tail)], axis=1)
    s = np.concatenate([z(MLA_NOPE), sin, sin, z(tail)], axis=1)
    s1 = np.concatenate([z(MLA_NOPE), -sin, z(half), z(tail)], axis=1)
    s2 = np.concatenate([z(MLA_NOPE), z(half), sin, z(tail)], axis=1)
    scale = (MLA_NOPE + MLA_ROPE) ** -0.5 * LOG2E
    tabs = (c * scale, s * scale, c, s1, s2)
    pad = lambda t: np.concatenate([np.zeros((N_PAD, LANES)), t[:N_META]], axis=0)
    return (tuple(jnp.asarray(pad(t), F32) for t in tabs), tuple(jnp.asarray(t[N_META:], F32) for t in tabs))


def _suffix_matrix(n):
    return jnp.asarray(np.where(np.arange(n)[:, None] <= np.arange(n)[None, :], -1.0, 0.0), BF16)


def kernel(x, meta, norm_g, final_g, ev_w_in, ev_q_norm_g, ev_kv_norm_g, ev_w_uq, ev_w_ukv,
           ev_w_out, od_w_in, od_sinks, od_w_out):
    w0, wuq, wukv = _layer0_weights(ev_w_in[0], ev_w_uq[0], ev_w_ukv[0])
    w1, wo1 = _layer1_weights(od_w_in[0], od_w_out[0])
    wo0 = ev_w_out[0].astype(BF16)
    tabs_meta, tabs_real = _rope_tables()
    u, u_block, u_meta = _suffix_matrix(ATT_TILE), _suffix_matrix(BLOCK), _suffix_matrix(N_META)
    l0 = (norm_g[0:1], w0, ev_q_norm_g[0:1], ev_kv_norm_g[0:1], wuq, wukv)

    hm = jnp.concatenate([jnp.zeros((N_PAD, D_MODEL), x.dtype), meta.astype(x.dtype)], axis=0)[None]
    qsb_m, ksb_m, vsb_m, gate_m, qm_m, km_m, vm_m = _proj0(hm, BLOCK, *l0, tabs_meta)
    osb_m = _sb_meta(qsb_m, ksb_m, vsb_m, gate_m, u_block)
    omla_m = _mla_meta(qm_m, km_m, vm_m, gate_m)
    _, _, kv1_m, _ = _mid(osb_m, omla_m, hm, BLOCK, wo0, norm_g[1:2], w1)

    q_sb, k_sb, v_sb, gate0, q_mla, k_mla, v_mla = _proj0(x, ROW_TILE, *l0, tabs_real)
    o_sb = _sb_attn(q_sb, k_sb, v_sb, ksb_m, vsb_m, gate0, u, u_meta)
    o_mla = _mla_attn(q_mla, k_mla, v_mla, km_m, vm_m, gate0)
    h1, q1, kv1, gate1 = _mid(o_sb, o_mla, x, ROW_TILE, wo0, norm_g[1:2], w1)
    return _swa_attn(od_sinks[0], q1, kv1, kv1_m, gate1, h1, wo1, final_g[None, :])
```
